```python
import math
import jax
import jax.numpy as jnp
from jax import lax
import numpy as np

D_MODEL = 1024
BATCH = 2
SEQ = 8192
DEPTH = 2
DEC_BATCH = 32
DEC_SEQ = 1
PAST_LEN = 8192
PAGE_SIZE = 128

F32 = jnp.float32
EPS = 1e-6
NEG_BIG = -1e30
LB_FLOOR = 1e-30
LB_CEIL = 1.0 - 1e-6
PLE_DIM = 256
MIX_WIDTH = D_MODEL
N_MIX_GROUPS = 4
GROUP_WIDTH = MIX_WIDTH // N_MIX_GROUPS
HEAD_DIM = 64
A_HEADS = GROUP_WIDTH // HEAD_DIM
A_KDIM = HEAD_DIM
A_VDIM = HEAD_DIM
B_HEADS = GROUP_WIDTH // HEAD_DIM
B_KDIM = HEAD_DIM
B_VDIM = HEAD_DIM
C_HEADS = GROUP_WIDTH // HEAD_DIM
C_HDIM = HEAD_DIM
D_HEADS = GROUP_WIDTH // HEAD_DIM
D_HDIM = HEAD_DIM
D_GROUPS = 2
D_STATE = 128
CONV_WIDTH = 4
B_CONV_CH = 3 * GROUP_WIDTH
D_CONV_CH = GROUP_WIDTH + 2 * D_GROUPS * D_STATE
SIZE_A = 4 * GROUP_WIDTH
SIZE_B = 4 * GROUP_WIDTH + 2 * B_HEADS
SIZE_C = 4 * GROUP_WIDTH + C_HEADS
SIZE_D = GROUP_WIDTH + D_CONV_CH + D_HEADS
OFF_B = SIZE_A
OFF_C = OFF_B + SIZE_B
OFF_D = OFF_C + SIZE_C
N_IN = OFF_D + SIZE_D
CHUNK_A = 16
CHUNK_B = 64
CHUNK_D = 128
Q_BLOCK = 128
N_EXPERT_GROUPS = 4
EXPERTS_PER_GROUP = 4
N_EXPERTS = N_EXPERT_GROUPS * EXPERTS_PER_GROUP
TOP_K_IN_GROUP = 2
D_EXPERT = 512
DISPATCH_BLOCK = 128

kernel_name = 'hymba_four_mixer_hmoe_decode_step'


def head_rmsnorm(x, g):
    xf = x.astype(F32)
    y = xf * lax.rsqrt(jnp.mean(xf * xf, axis=-1, keepdims=True) + EPS)
    return y * g.astype(F32).reshape((-1, x.shape[-1]))


def rmsnorm(x, g):
    return head_rmsnorm(x, g).astype(x.dtype)


def l2norm(x):
    return x * lax.rsqrt(jnp.sum(x * x, axis=-1, keepdims=True) + EPS)


def masked_exp(logit, mask):
    return jnp.where(mask, jnp.exp(jnp.where(mask, logit, 0.0)), 0.0)


def causal_conv(x, w, prev):
    t = x.shape[1]
    xp = jnp.concatenate([prev.astype(x.dtype), x], axis=1)
    y = xp[:, 0:t] * w[0]
    for j in range(1, w.shape[0]):
        y = y + xp[:, j:j + t] * w[j]
    return y, xp[:, t:]


def _pad_time(t, pad):
    return jnp.pad(t, [(0, 0), (0, pad)] + [(0, 0)] * (t.ndim - 2))


def _to_chunks(t, chunk):
    b, tt, h = t.shape[:3]
    t = t.reshape((b, tt // chunk, chunk, h) + t.shape[3:])
    return jnp.moveaxis(t, 3, 1)


def _from_chunks(t):
    b, h, n, c = t.shape[:4]
    t = jnp.moveaxis(t, 1, 3)
    return t.reshape((b, n * c, h) + t.shape[4:])


def _chunk_inputs(arrs, chunk):
    t = arrs[0].shape[1]
    pad = (-t) % chunk
    return [_to_chunks(_pad_time(a.astype(F32), pad), chunk) for a in arrs]


def chunked_gla(q, k, v, log_f, s0, chunk):
    t = q.shape[1]
    q, k, v, log_f = _chunk_inputs([q, k, v, log_f], chunk)
    b = jnp.cumsum(log_f, axis=3)
    causal = jnp.tril(jnp.ones((chunk, chunk), bool))[:, :, None]
    rel = masked_exp(b[..., :, None, :] - b[..., None, :, :], causal)
    att = jnp.einsum('bhntk,bhnsk,bhntsk->bhnts', q, k, rel)
    o_intra = jnp.einsum('bhnts,bhnsv->bhntv', att, v)
    b_last = b[..., -1:, :]
    ds = jnp.einsum('bhnsk,bhnsv->nbhkv', k * jnp.exp(b_last - b), v)
    a_chunk = jnp.moveaxis(jnp.exp(b_last[..., 0, :]), 2, 0)

    def step(s, inp):
        a, d = inp
        return a[..., None] * s + d, s

    s_final, s_prev = lax.scan(step, s0.astype(F32), (a_chunk, ds))
    o_inter = jnp.einsum('bhntk,nbhkv->bhntv', q * jnp.exp(b), s_prev)
    return _from_chunks(o_intra + o_inter)[:, :t], s_final


def chunked_ssd(q, k, v, log_a, s0, chunk):
    t = q.shape[1]
    q, k, v, la = _chunk_inputs([q, k, v, log_a], chunk)
    b = jnp.cumsum(la, axis=-1)
    causal = jnp.tril(jnp.ones((chunk, chunk), bool))
    rel = masked_exp(b[..., :, None] - b[..., None, :], causal)
    att = jnp.einsum('bhntk,bhnsk->bhnts', q, k) * rel
    o_intra = jnp.einsum('bhnts,bhnsv->bhntv', att, v)
    b_last = b[..., -1:]
    ds = jnp.einsum('bhnsk,bhnsv->nbhkv', k * jnp.exp(b_last - b)[..., None], v)
    a_chunk = jnp.moveaxis(jnp.exp(b_last[..., 0]), 2, 0)

    def step(s, inp):
        a, d = inp
        return a[..., None, None] * s + d, s

    s_final, s_prev = lax.scan(step, s0.astype(F32), (a_chunk, ds))
    o_inter = jnp.einsum('bhntk,nbhkv->bhntv', q * jnp.exp(b)[..., None], s_prev)
    return _from_chunks(o_intra + o_inter)[:, :t], s_final


def chunked_gated_delta(q, k, v, beta, log_a, s0, chunk):
    t = q.shape[1]
    vd = v.shape[-1]
    q, k, v, beta, la = _chunk_inputs([q, k, v, beta, log_a], chunk)
    b = jnp.cumsum(la, axis=-1)
    causal = jnp.tril(jnp.ones((chunk, chunk), bool))
    strict = jnp.tril(jnp.ones((chunk, chunk), bool), k=-1)
    decay = masked_exp(b[..., :, None] - b[..., None, :], causal)
    kb = k * beta[..., None]
    m = jnp.where(strict, jnp.einsum('bhntk,bhnsk->bhnts', kb, k) * decay, 0.0)
    rhs = jnp.concatenate([v * beta[..., None], kb * jnp.exp(b)[..., None]], axis=-1)
    sol = lax.linalg.triangular_solve(m + jnp.eye(chunk, dtype=F32), rhs, left_side=True,
                                      lower=True, unit_diagonal=True)
    u, w = sol[..., :vd], sol[..., vd:]
    att = jnp.einsum('bhntk,bhnsk->bhnts', q, k) * decay
    q_dec = q * jnp.exp(b)[..., None]
    k_dec = k * jnp.exp(b[..., -1:] - b)[..., None]
    a_chunk = jnp.exp(b[..., -1])
    xs = tuple(jnp.moveaxis(a, 2, 0) for a in (u, w, att, q_dec, k_dec, a_chunk))

    def step(s, inp):
        u_c, w_c, att_c, qd_c, kd_c, a_c = inp
        v_new = u_c - jnp.einsum('bhtk,bhkv->bhtv', w_c, s)
        o = jnp.einsum('bhtk,bhkv->bhtv', qd_c, s) + jnp.einsum('bhts,bhsv->bhtv', att_c, v_new)
        s = a_c[..., None, None] * s + jnp.einsum('bhsk,bhsv->bhkv', kd_c, v_new)
        return s, o

    s_final, o = lax.scan(step, s0.astype(F32), xs)
    return _from_chunks(jnp.moveaxis(o, 0, 2))[:, :t], s_final


def hgrn_lower_bounds(lb_param):
    sm = jax.nn.softmax(lb_param.astype(F32), axis=0)
    return jnp.concatenate([jnp.zeros_like(sm[:1]), jnp.cumsum(sm[1:], axis=0)], axis=0)


def hgrn2_mixer(proj, lb, norm_g, s0):
    bsz, t, _ = proj.shape
    w = GROUP_WIDTH
    q, zf, inp, g = (proj[..., j * w:(j + 1) * w] for j in range(4))
    zf = zf.astype(F32)
    lb = jnp.clip(lb.astype(F32), 0.0, LB_CEIL)
    log_f = jnp.logaddexp(jnp.log(jnp.maximum(lb, LB_FLOOR)), jnp.log1p(-lb) + jax.nn.log_sigmoid(zf))
    k = (1.0 - lb) * jax.nn.sigmoid(-zf)
    heads = lambda a: a.reshape(bsz, t, A_HEADS, -1)
    o, s = chunked_gla(heads(q) * A_KDIM ** -0.5, heads(k), heads(inp), heads(log_f), s0, CHUNK_A)
    o = head_rmsnorm(o, norm_g) * jax.nn.silu(heads(g).astype(F32))
    return o.reshape(bsz, t, w).astype(proj.dtype), s


def gdn_mixer(proj, conv_w, conv_prev, a_log, dt_bias, norm_g, s0):
    bsz, t, _ = proj.shape
    w = GROUP_WIDTH
    qkv, conv_new = causal_conv(proj[..., :3 * w], conv_w, conv_prev)
    qkv = jax.nn.silu(qkv.astype(F32))
    heads = lambda a: a.reshape(bsz, t, B_HEADS, -1)
    q = l2norm(heads(qkv[..., :w])) * B_KDIM ** -0.5
    k = l2norm(heads(qkv[..., w:2 * w]))
    v = heads(qkv[..., 2 * w:3 * w])
    gate = heads(proj[..., 3 * w:4 * w]).astype(F32)
    beta = jax.nn.sigmoid(proj[..., 4 * w:4 * w + B_HEADS].astype(F32))
    log_a = -jnp.exp(a_log.astype(F32)) * jax.nn.softplus(proj[..., 4 * w + B_HEADS:].astype(F32) + dt_bias)
    o, s = chunked_gated_delta(q, k, v, beta, log_a, s0, CHUNK_B)
    o = head_rmsnorm(o, norm_g) * jax.nn.silu(gate)
    return o.reshape(bsz, t, w).astype(proj.dtype), s, conv_new


def fox_attention_prompt(q, k, v, log_f):
    b, t, h, d = q.shape
    n_blk = -(-t // Q_BLOCK)
    pad = n_blk * Q_BLOCK - t
    c = jnp.moveaxis(jnp.cumsum(log_f, axis=1), 2, 1)
    q_p = _pad_time(q, pad)
    c_p = jnp.pad(c, [(0, 0), (0, 0), (0, pad)])
    key_pos = jnp.arange(t)

    def block(i):
        start = i * Q_BLOCK
        q_i = lax.dynamic_slice_in_dim(q_p, start, Q_BLOCK, axis=1)
        c_i = lax.dynamic_slice_in_dim(c_p, start, Q_BLOCK, axis=2)
        s = jnp.einsum('bqhd,bkhd->bhqk', q_i, k) * (d ** -0.5) + c_i[..., :, None] - c[:, :, None, :]
        allowed = (start + jnp.arange(Q_BLOCK))[:, None] >= key_pos[None, :]
        p = jax.nn.softmax(jnp.where(allowed, s, NEG_BIG), axis=-1)
        return jnp.einsum('bhqk,bkhd->bqhd', p, v)

    o = lax.map(block, jnp.arange(n_blk))
    return jnp.moveaxis(o, 0, 1).reshape(b, n_blk * Q_BLOCK, h, d)[:, :t]


def fox_attention_sample(q, k, v, log_f, k_past, v_past, logf_past):
    n_past = k_past.shape[1]
    s_new = q.shape[1]
    d = q.shape[-1]
    k_all = jnp.concatenate([k_past.astype(F32), k], axis=1)
    v_all = jnp.concatenate([v_past.astype(F32), v], axis=1)
    c = jnp.cumsum(jnp.concatenate([logf_past.astype(F32), log_f], axis=1), axis=1)
    c = jnp.moveaxis(c, 2, 1)
    s = jnp.einsum('bqhd,bkhd->bhqk', q, k_all) * (d ** -0.5) + c[:, :, n_past:, None] - c[:, :, None, :]
    allowed = (n_past + jnp.arange(s_new))[:, None] >= jnp.arange(n_past + s_new)[None, :]
    p = jax.nn.softmax(jnp.where(allowed, s, NEG_BIG), axis=-1)
    return jnp.einsum('bhqk,bkhd->bqhd', p, v_all)


def fox_mixer(proj, b_f, q_norm, k_norm, out_norm, past):
    bsz, t, _ = proj.shape
    w = GROUP_WIDTH
    heads = lambda a: a.reshape(bsz, t, C_HEADS, C_HDIM)
    q = head_rmsnorm(heads(proj[..., :w]), q_norm)
    k = head_rmsnorm(heads(proj[..., w:2 * w]), k_norm)
    v = heads(proj[..., 2 * w:3 * w]).astype(F32)
    g = heads(proj[..., 3 * w:4 * w]).astype(F32)
    log_f = jax.nn.log_sigmoid(proj[..., 4 * w:].astype(F32) + b_f.astype(F32))
    if past is None:
        o = fox_attention_prompt(q, k, v, log_f)
    else:
        o = fox_attention_sample(q, k, v, log_f, past[0], past[1], past[2])
    o = head_rmsnorm(o, out_norm) * jax.nn.sigmoid(g)
    return o.reshape(bsz, t, w).astype(proj.dtype), k, v, log_f


def ssd_mixer(proj, conv_w, conv_b, conv_prev, a_log, dt_bias, d_skip, norm_g, s0):
    bsz, t, _ = proj.shape
    w = GROUP_WIDTH
    gn = D_GROUPS * D_STATE
    rep = D_HEADS // D_GROUPS
    z = proj[..., :w].reshape(bsz, t, D_HEADS, D_HDIM).astype(F32)
    xbc, conv_new = causal_conv(proj[..., w:w + D_CONV_CH], conv_w, conv_prev)
    xbc = jax.nn.silu((xbc + conv_b).astype(F32))
    xs = xbc[..., :w].reshape(bsz, t, D_HEADS, D_HDIM)
    bm = jnp.repeat(xbc[..., w:w + gn].reshape(bsz, t, D_GROUPS, D_STATE), rep, axis=2)
    cm = jnp.repeat(xbc[..., w + gn:].reshape(bsz, t, D_GROUPS, D_STATE), rep, axis=2)
    dt = jax.nn.softplus(proj[..., w + D_CONV_CH:].astype(F32) + dt_bias.astype(F32))
    log_a = -jnp.exp(a_log.astype(F32)) * dt
    y, s = chunked_ssd(cm, bm, xs * dt[..., None], log_a, s0, CHUNK_D)
    y = (y + xs * d_skip.astype(F32)[:, None]) * jax.nn.silu(z)
    y = head_rmsnorm(y.reshape(bsz, t, D_GROUPS, -1), norm_g)
    return y.reshape(bsz, t, w).astype(proj.dtype), s, conv_new


def hier_moe(u, w_group, b_group, w_expert, b_expert, w1, w3, w2):
    bsz, t, d = u.shape
    x = u.reshape(bsz * t, d)
    n_tok = bsz * t
    tok = jnp.arange(n_tok)
    group_logits = (x @ w_group).astype(F32) + b_group.astype(F32)
    group_idx = jnp.argmax(group_logits, axis=-1)
    group_w = jax.nn.softmax(group_logits, axis=-1)[tok, group_idx]
    exp_logits = ((x @ w_expert).astype(F32) + b_expert.astype(F32)).reshape(n_tok, N_EXPERT_GROUPS, EXPERTS_PER_GROUP)
    top_logits, top_idx = lax.top_k(exp_logits[tok, group_idx], TOP_K_IN_GROUP)
    gates = jax.nn.softmax(top_logits, axis=-1) * group_w[:, None]
    expert_idx = group_idx[:, None] * EXPERTS_PER_GROUP + top_idx
    n_assign = n_tok * TOP_K_IN_GROUP
    flat_e = expert_idx.reshape(-1)
    order = jnp.argsort(flat_e)
    se = flat_e[order]
    stok = (jnp.arange(n_assign) // TOP_K_IN_GROUP)[order]
    sw = gates.reshape(-1)[order]
    counts = jnp.bincount(flat_e, length=N_EXPERTS)
    starts = jnp.cumsum(counts) - counts
    padded = ((counts + DISPATCH_BLOCK - 1) // DISPATCH_BLOCK) * DISPATCH_BLOCK
    p_ends = jnp.cumsum(padded)
    dest = (p_ends - padded)[se] + jnp.arange(n_assign) - starts[se]
    n_blocks = (n_assign + N_EXPERTS * (DISPATCH_BLOCK - 1) + DISPATCH_BLOCK - 1) // DISPATCH_BLOCK
    rows = n_blocks * DISPATCH_BLOCK
    xs = jnp.zeros((rows, d), x.dtype).at[dest].set(x[stok])
    blk_e = jnp.clip(jnp.searchsorted(p_ends, jnp.arange(n_blocks) * DISPATCH_BLOCK, side='right'), 0, N_EXPERTS - 1)

    def expert_block(args):
        xb, e = args
        hid = jax.nn.silu(xb @ w1[e]) * (xb @ w3[e])
        return hid @ w2[e]

    ys = lax.map(expert_block, (xs.reshape(n_blocks, DISPATCH_BLOCK, d), blk_e)).reshape(rows, d)
    y = jax.ops.segment_sum(ys[dest] * sw[:, None].astype(ys.dtype), stok, num_segments=n_tok)
    return y.reshape(bsz, t, d).astype(u.dtype)


def gather_pages(pool, page_table):
    g = pool[page_table]
    return g.reshape((g.shape[0], g.shape[1] * g.shape[2]) + g.shape[3:])


def decoder_trunk(x, p, init_state, fox_cache, lb_all, prm):
    s_hgrn0, s_gdn0, c_gdn0, s_ssd0, c_ssd0 = init_state
    h = x
    outs = [[] for _ in range(8)]
    for l in range(DEPTH):
        u = rmsnorm(h, prm['g_mix'][l])
        proj = u @ prm['w_in'][l]
        o_a, s_a = hgrn2_mixer(proj[..., :OFF_B], lb_all[l], prm['hgrn_norm'][l], s_hgrn0[l])
        o_b, s_b, c_b = gdn_mixer(proj[..., OFF_B:OFF_C], prm['gdn_conv_w'][l], c_gdn0[l], prm['gdn_a_log'][l],
                                  prm['gdn_dt_bias'][l], prm['gdn_norm'][l], s_gdn0[l])
        if fox_cache is None:
            past = None
        else:
            cache_k, cache_v, cache_logf, page_table = fox_cache
            past = (gather_pages(cache_k[l], page_table), gather_pages(cache_v[l], page_table),
                    gather_pages(cache_logf[l], page_table))
        o_c, k_c, v_c, lf_c = fox_mixer(proj[..., OFF_C:OFF_D], prm['fox_b_f'][l], prm['fox_q_norm'][l],
                                        prm['fox_k_norm'][l], prm['fox_out_norm'][l], past)
        o_d, s_d, c_d = ssd_mixer(proj[..., OFF_D:], prm['ssd_conv_w'][l], prm['ssd_conv_b'][l], c_ssd0[l],
                                  prm['ssd_a_log'][l], prm['ssd_dt_bias'][l], prm['ssd_d'][l], prm['ssd_norm'][l],
                                  s_ssd0[l])
        h = h + jnp.concatenate([o_a, o_b, o_c, o_d], axis=-1) @ prm['w_out'][l]
        h = h + hier_moe(rmsnorm(h, prm['g_ffn'][l]), prm['moe_w_group'][l], prm['moe_b_group'][l],
                         prm['moe_w_expert'][l], prm['moe_b_expert'][l], prm['moe_w1'][l], prm['moe_w3'][l],
                         prm['moe_w2'][l])
        gate = jax.nn.sigmoid((rmsnorm(h, prm['g_ple'][l]) @ prm['ple_w_gate'][l]).astype(F32))
        h = h + (gate * (p[l] @ prm['ple_w_proj'][l]).astype(F32)).astype(h.dtype)
        for acc, val in zip(outs, (k_c, v_c, lf_c, s_a, s_b, c_b, s_d, c_d)):
            acc.append(val)
    y = rmsnorm(h, prm['g_final'])
    return (y,) + tuple(jnp.stack(acc) for acc in outs)


def setup_inputs(seed: int = 0) -> dict:
    key = jax.random.key(seed)
    key_pt, key_rest = jax.random.split(key)
    ks = iter(jax.random.split(key_rest, 64))
    nrm = lambda shape, scale=1.0: scale * jax.random.normal(next(ks), shape, F32)
    gain = lambda shape: 1.0 + 0.02 * jax.random.normal(next(ks), shape, F32)

    def dt_bias(shape):
        dt = jnp.exp(jax.random.uniform(next(ks), shape, F32, math.log(1e-3), math.log(1e-1)))
        return dt + jnp.log(-jnp.expm1(-dt))

    def a_log(shape):
        return jnp.log(jax.random.uniform(next(ks), shape, F32, 1.0, 16.0))

    n_pages = PAST_LEN // PAGE_SIZE
    n_used = DEC_BATCH * n_pages
    n_phys = n_used + max(1, n_used // 4)
    page_table = jax.random.permutation(key_pt, n_phys)[:n_used].reshape(DEC_BATCH, n_pages).astype(jnp.int32)
    return {
        'x_prompt': nrm((BATCH, SEQ, D_MODEL)),
        'x_sample': nrm((DEC_BATCH, DEC_SEQ, D_MODEL)),
        'cache_fox_k': nrm((DEPTH, n_phys, PAGE_SIZE, C_HEADS, C_HDIM)),
        'cache_fox_v': nrm((DEPTH, n_phys, PAGE_SIZE, C_HEADS, C_HDIM)),
        'cache_fox_logf': jax.nn.log_sigmoid(1.0 + nrm((DEPTH, n_phys, PAGE_SIZE, C_HEADS))),
        'state_hgrn': nrm((DEPTH, DEC_BATCH, A_HEADS, A_KDIM, A_VDIM), 0.1),
        'state_gdn': nrm((DEPTH, DEC_BATCH, B_HEADS, B_KDIM, B_VDIM), 0.1),
        'state_gdn_conv': nrm((DEPTH, DEC_BATCH, CONV_WIDTH - 1, B_CONV_CH)),
        'state_ssd': nrm((DEPTH, DEC_BATCH, D_HEADS, D_STATE, D_HDIM), 0.1),
        'state_ssd_conv': nrm((DEPTH, DEC_BATCH, CONV_WIDTH - 1, D_CONV_CH)),
        'page_table': page_table,
        'p_prompt': nrm((DEPTH, BATCH, SEQ, PLE_DIM)),
        'p_sample': nrm((DEPTH, DEC_BATCH, DEC_SEQ, PLE_DIM)),
        'w_in': nrm((DEPTH, D_MODEL, N_IN), D_MODEL ** -0.5),
        'w_out': nrm((DEPTH, MIX_WIDTH, D_MODEL), MIX_WIDTH ** -0.5),
        'g_mix': gain((DEPTH, D_MODEL)),
        'g_ffn': gain((DEPTH, D_MODEL)),
        'g_ple': gain((DEPTH, D_MODEL)),
        'g_final': gain((D_MODEL,)),
        'hgrn_lb': nrm((DEPTH, GROUP_WIDTH)),
        'hgrn_norm': gain((DEPTH, GROUP_WIDTH)),
        'gdn_conv_w': nrm((DEPTH, CONV_WIDTH, B_CONV_CH), CONV_WIDTH ** -0.5),
        'gdn_a_log': a_log((DEPTH, B_HEADS)),
        'gdn_dt_bias': dt_bias((DEPTH, B_HEADS)),
        'gdn_norm': gain((DEPTH, GROUP_WIDTH)),
        'fox_b_f': 1.0 + nrm((DEPTH, C_HEADS), 0.1),
        'fox_q_norm': gain((DEPTH, C_HDIM)),
        'fox_k_norm': gain((DEPTH, C_HDIM)),
        'fox_out_norm': gain((DEPTH, GROUP_WIDTH)),
        'ssd_conv_w': nrm((DEPTH, CONV_WIDTH, D_CONV_CH), CONV_WIDTH ** -0.5),
        'ssd_conv_b': nrm((DEPTH, D_CONV_CH), 0.01),
        'ssd_a_log': a_log((DEPTH, D_HEADS)),
        'ssd_dt_bias': dt_bias((DEPTH, D_HEADS)),
        'ssd_d': 1.0 + nrm((DEPTH, D_HEADS), 0.1),
        'ssd_norm': gain((DEPTH, GROUP_WIDTH)),
        'moe_w_group': nrm((DEPTH, D_MODEL, N_EXPERT_GROUPS), D_MODEL ** -0.5),
        'moe_b_group': nrm((DEPTH, N_EXPERT_GROUPS), 0.01),
        'moe_w_expert': nrm((DEPTH, D_MODEL, N_EXPERTS), D_MODEL ** -0.5),
        'moe_b_expert': nrm((DEPTH, N_EXPERTS), 0.01),
        'moe_w1': nrm((DEPTH, N_EXPERTS, D_MODEL, D_EXPERT), D_MODEL ** -0.5),
        'moe_w3': nrm((DEPTH, N_EXPERTS, D_MODEL, D_EXPERT), D_MODEL ** -0.5),
        'moe_w2': nrm((DEPTH, N_EXPERTS, D_EXPERT, D_MODEL), D_EXPERT ** -0.5),
        'ple_w_gate': nrm((DEPTH, D_MODEL, D_MODEL), D_MODEL ** -0.5),
        'ple_w_proj': nrm((DEPTH, PLE_DIM, D_MODEL), PLE_DIM ** -0.5),
    }


def reference(x_prompt, x_sample, cache_fox_k, cache_fox_v, cache_fox_logf, state_hgrn, state_gdn,
              state_gdn_conv, state_ssd, state_ssd_conv, page_table, p_prompt, p_sample, w_in, w_out,
              g_mix, g_ffn, g_ple, g_final, hgrn_lb, hgrn_norm, gdn_conv_w, gdn_a_log, gdn_dt_bias,
              gdn_norm, fox_b_f, fox_q_norm, fox_k_norm, fox_out_norm, ssd_conv_w, ssd_conv_b, ssd_a_log,
              ssd_dt_bias, ssd_d, ssd_norm, moe_w_group, moe_b_group, moe_w_expert, moe_b_expert, moe_w1,
              moe_w3, moe_w2, ple_w_gate, ple_w_proj):
    prm = dict(w_in=w_in, w_out=w_out, g_mix=g_mix, g_ffn=g_ffn, g_ple=g_ple, g_final=g_final,
               hgrn_norm=hgrn_norm, gdn_conv_w=gdn_conv_w, gdn_a_log=gdn_a_log, gdn_dt_bias=gdn_dt_bias,
               gdn_norm=gdn_norm, fox_b_f=fox_b_f, fox_q_norm=fox_q_norm, fox_k_norm=fox_k_norm,
               fox_out_norm=fox_out_norm, ssd_conv_w=ssd_conv_w, ssd_conv_b=ssd_conv_b, ssd_a_log=ssd_a_log,
               ssd_dt_bias=ssd_dt_bias, ssd_d=ssd_d, ssd_norm=ssd_norm, moe_w_group=moe_w_group,
               moe_b_group=moe_b_group, moe_w_expert=moe_w_expert, moe_b_expert=moe_b_expert,
               moe_w1=moe_w1, moe_w3=moe_w3, moe_w2=moe_w2, ple_w_gate=ple_w_gate, ple_w_proj=ple_w_proj)
    lb_all = hgrn_lower_bounds(hgrn_lb)
    bp = x_prompt.shape[0]
    zero_state = (jnp.zeros((DEPTH, bp, A_HEADS, A_KDIM, A_VDIM), F32),
                  jnp.zeros((DEPTH, bp, B_HEADS, B_KDIM, B_VDIM), F32),
                  jnp.zeros((DEPTH, bp, CONV_WIDTH - 1, B_CONV_CH), x_prompt.dtype),
                  jnp.zeros((DEPTH, bp, D_HEADS, D_STATE, D_HDIM), F32),
                  jnp.zeros((DEPTH, bp, CONV_WIDTH - 1, D_CONV_CH), x_prompt.dtype))
    (y_prompt, pr_k, pr_v, pr_logf, pr_hgrn, pr_gdn, pr_gdn_conv, pr_ssd, pr_ssd_conv) = decoder_trunk(
        x_prompt, p_prompt, zero_state, None, lb_all, prm)
    (y_sample, sm_k, sm_v, sm_logf, sm_hgrn, sm_gdn, sm_gdn_conv, sm_ssd, sm_ssd_conv) = decoder_trunk(
        x_sample, p_sample, (state_hgrn, state_gdn, state_gdn_conv, state_ssd, state_ssd_conv),
        (cache_fox_k, cache_fox_v, cache_fox_logf, page_table), lb_all, prm)
    return (y_prompt, y_sample, pr_k, pr_v, pr_logf, pr_hgrn, pr_gdn, pr_gdn_conv, pr_ssd, pr_ssd_conv,
            sm_k, sm_v, sm_logf, sm_hgrn, sm_gdn, sm_gdn_conv, sm_ssd, sm_ssd_conv)
```

```python
import functools
import math

import jax
import jax.numpy as jnp
import numpy as np
from jax import lax
from jax.experimental import pallas as pl
from jax.experimental.pallas import tpu as pltpu

F32 = jnp.float32
BF16 = jnp.bfloat16

D_MODEL = 1024
DEPTH = 2
PAGE_SIZE = 128
EPS = 1e-6
NEG_BIG = -1e30
LB_FLOOR = 1e-30
LB_CEIL = 1.0 - 1e-6
PLE_DIM = 256
GROUP_WIDTH = 256
HEAD_DIM = 64
N_HEADS = 4
D_GROUPS = 2
D_STATE = 128
CONV_WIDTH = 4
B_CONV_CH = 3 * GROUP_WIDTH
D_CONV_CH = GROUP_WIDTH + 2 * D_GROUPS * D_STATE
SIZE_A = 4 * GROUP_WIDTH
SIZE_B = 4 * GROUP_WIDTH + 2 * N_HEADS
SIZE_C = 4 * GROUP_WIDTH + N_HEADS
SIZE_D = GROUP_WIDTH + D_CONV_CH + N_HEADS
OFF_B = SIZE_A
OFF_C = OFF_B + SIZE_B
OFF_D = OFF_C + SIZE_C
N_IN = OFF_D + SIZE_D
CHUNK_A = 16
CHUNK_B = 64
CHUNK_D = 128
Q_BLOCK = 128
N_EXPERT_GROUPS = 4
EXPERTS_PER_GROUP = 4
N_EXPERTS = 16
D_EXPERT = 512

LANES = 128
N_MAIN = 4096
VMEM_LIMIT = 48 * 1024 * 1024

_MAIN_COLS = np.concatenate([
    np.arange(0, SIZE_A),
    np.arange(OFF_B, OFF_B + 4 * GROUP_WIDTH),
    np.arange(OFF_C, OFF_C + 4 * GROUP_WIDTH),
    np.arange(OFF_D, OFF_D + 4 * GROUP_WIDTH),
])
_SMALL_COLS = np.concatenate([
    np.arange(OFF_B + 4 * GROUP_WIDTH, OFF_B + SIZE_B),
    np.arange(OFF_C + 4 * GROUP_WIDTH, OFF_C + SIZE_C),
    np.arange(OFF_D + 4 * GROUP_WIDTH, OFF_D + SIZE_D),
])


def _split_bf16(x):
    hi = x.astype(BF16)
    lo = (x - hi.astype(F32)).astype(BF16)
    return hi, lo


def _dot(a, b):
    return jnp.dot(a, b, preferred_element_type=F32)


def _dot3(a_hi, a_lo, b_hi, b_lo):
    return _dot(a_hi, b_hi) + _dot(a_lo, b_hi) + _dot(a_hi, b_lo)


def _dotp(a, b_hi, b_lo, hp):
    if hp:
        a_hi, a_lo = _split_bf16(a)
        return _dot3(a_hi, a_lo, b_hi, b_lo)
    return _dot(a.astype(BF16), b_hi)


def _rms(x, g):
    return x * lax.rsqrt(jnp.mean(x * x, axis=-1, keepdims=True) + EPS) * g


def _inproj_kernel(x_ref, g_ref, wmh_ref, wml_ref, wsh_ref, wsl_ref, om_ref, os_ref, uh_ref, ul_ref, *, hp):
    @pl.when(pl.program_id(1) == 0)
    def _():
        u = _rms(x_ref[...], g_ref[...])
        uh, ul = _split_bf16(u)
        uh_ref[...] = uh
        ul_ref[...] = ul
        os_ref[...] = _dot3(uh, ul, wsh_ref[...], wsl_ref[...])

    if hp:
        om_ref[...] = _dot3(uh_ref[...], ul_ref[...], wmh_ref[...], wml_ref[...])
    else:
        om_ref[...] = _dot(uh_ref[...], wmh_ref[...])


def _inproj(x, g, wmh, wml, wsh, wsl, tm, hp):
    n = x.shape[0]
    tn = 1024
    return pl.pallas_call(
        functools.partial(_inproj_kernel, hp=hp),
        grid=(n // tm, N_MAIN // tn),
        in_specs=[
            pl.BlockSpec((tm, D_MODEL), lambda i, j: (i, 0)),
            pl.BlockSpec((1, D_MODEL), lambda i, j: (0, 0)),
            pl.BlockSpec((D_MODEL, tn), lambda i, j: (0, j)),
            pl.BlockSpec((D_MODEL, tn), lambda i, j: (0, j if hp else 0)),
            pl.BlockSpec((D_MODEL, LANES), lambda i, j: (0, 0)),
            pl.BlockSpec((D_MODEL, LANES), lambda i, j: (0, 0)),
        ],
        out_specs=[
            pl.BlockSpec((tm, tn), lambda i, j: (i, j)),
            pl.BlockSpec((tm, LANES), lambda i, j: (i, 0)),
        ],
        out_shape=[
            jax.ShapeDtypeStruct((n, N_MAIN), F32),
            jax.ShapeDtypeStruct((n, LANES), F32),
        ],
        scratch_shapes=[pltpu.VMEM((tm, D_MODEL), BF16), pltpu.VMEM((tm, D_MODEL), BF16)],
        compiler_params=pltpu.CompilerParams(
            dimension_semantics=("parallel", "arbitrary"), vmem_limit_bytes=VMEM_LIMIT),
        name="inproj",
    )(x, g, wmh, wml, wsh, wsl)


def _route(logits):
    lane = lax.broadcasted_iota(jnp.int32, logits.shape, 1)
    gmask = (lane >= N_EXPERTS) & (lane < N_EXPERTS + N_EXPERT_GROUPS)
    gl = jnp.where(gmask, logits, -jnp.inf)
    gmax = jnp.max(gl, axis=-1, keepdims=True)
    gidx = jnp.min(jnp.where(gl == gmax, lane, 4 * LANES), axis=-1, keepdims=True) - N_EXPERTS
    gw = 1.0 / jnp.sum(jnp.where(gmask, jnp.exp(gl - gmax), 0.0), axis=-1, keepdims=True)
    lo = gidx * EXPERTS_PER_GROUP
    emask = (lane >= lo) & (lane < lo + EXPERTS_PER_GROUP)
    el = jnp.where(emask, logits, -jnp.inf)
    m1 = jnp.max(el, axis=-1, keepdims=True)
    i1 = jnp.min(jnp.where(el == m1, lane, 4 * LANES), axis=-1, keepdims=True)
    el2 = jnp.where(lane == i1, -jnp.inf, el)
    m2 = jnp.max(el2, axis=-1, keepdims=True)
    i2 = jnp.min(jnp.where(el2 == m2, lane, 4 * LANES), axis=-1, keepdims=True)
    e2 = jnp.exp(m2 - m1)
    den = 1.0 + e2
    g1 = gw / den
    g2 = gw * e2 / den
    return jnp.where(lane == i1, g1, jnp.where(lane == i2, g2, 0.0))


def _outproj_kernel(h_ref, mix_ref, woh_ref, wol_ref, g_ref, wrh_ref, wrl_ref, br_ref,
                    h2_ref, u2_ref, cw_ref, *, hp):
    h2 = h_ref[...] + _dotp(mix_ref[...], woh_ref[...], wol_ref[...], hp)
    h2_ref[...] = h2
    u = _rms(h2, g_ref[...])
    u2_ref[...] = u
    uh, ul = _split_bf16(u)
    logits = _dot3(uh, ul, wrh_ref[...], wrl_ref[...]) + br_ref[...]
    cw_ref[...] = _route(logits)


def _outproj(h, mix, woh, wol, g, wrh, wrl, br, tm, hp):
    n = h.shape[0]
    row = lambda i: (i, 0)
    fixed = lambda i: (0, 0)
    return pl.pallas_call(
        functools.partial(_outproj_kernel, hp=hp),
        grid=(n // tm,),
        in_specs=[
            pl.BlockSpec((tm, D_MODEL), row),
            pl.BlockSpec((tm, D_MODEL), row),
            pl.BlockSpec((D_MODEL, D_MODEL), fixed),
            pl.BlockSpec((D_MODEL, D_MODEL), fixed),
            pl.BlockSpec((1, D_MODEL), fixed),
            pl.BlockSpec((D_MODEL, LANES), fixed),
            pl.BlockSpec((D_MODEL, LANES), fixed),
            pl.BlockSpec((1, LANES), fixed),
        ],
        out_specs=[
            pl.BlockSpec((tm, D_MODEL), row),
            pl.BlockSpec((tm, D_MODEL), row),
            pl.BlockSpec((tm, LANES), row),
        ],
        out_shape=[
            jax.ShapeDtypeStruct((n, D_MODEL), F32),
            jax.ShapeDtypeStruct((n, D_MODEL), F32),
            jax.ShapeDtypeStruct((n, LANES), F32),
        ],
        compiler_params=pltpu.CompilerParams(
            dimension_semantics=("parallel",), vmem_limit_bytes=VMEM_LIMIT),
        name="outproj",
    )(h, mix, woh, wol, g, wrh, wrl, br)


def _moe_kernel(x_ref, cw_ref, w1h_ref, w1l_ref, w3h_ref, w3l_ref, w2h_ref, w2l_ref, y_ref,
                xh_ref, xl_ref, *, hp):
    e = pl.program_id(1)

    @pl.when(e == 0)
    def _():
        y_ref[...] = jnp.zeros_like(y_ref)
        xh, xl = _split_bf16(x_ref[...])
        xh_ref[...] = xh
        xl_ref[...] = xl

    cw = cw_ref[...]
    lane = lax.broadcasted_iota(jnp.int32, cw.shape, 1)
    col = jnp.sum(jnp.where(lane == e, cw, 0.0), axis=-1, keepdims=True)
    xh = xh_ref[...]
    if hp:
        xl = xl_ref[...]
        a = _dot3(xh, xl, w1h_ref[0], w1l_ref[0])
        b = _dot3(xh, xl, w3h_ref[0], w3l_ref[0])
    else:
        a = _dot(xh, w1h_ref[0])
        b = _dot(xh, w3h_ref[0])
    hid = (a * jax.nn.sigmoid(a)) * b
    y_ref[...] += col * _dotp(hid, w2h_ref[0], w2l_ref[0], hp)


def _moe(u2, cw, w1h, w1l, w3h, w3l, w2h, w2l, tm, hp):
    n = u2.shape[0]
    lo = (lambda i, e: (e, 0, 0)) if hp else (lambda i, e: (0, 0, 0))
    return pl.pallas_call(
        functools.partial(_moe_kernel, hp=hp),
        grid=(n // tm, N_EXPERTS),
        in_specs=[
            pl.BlockSpec((tm, D_MODEL), lambda i, e: (i, 0)),
            pl.BlockSpec((tm, LANES), lambda i, e: (i, 0)),
            pl.BlockSpec((1, D_MODEL, D_EXPERT), lambda i, e: (e, 0, 0)),
            pl.BlockSpec((1, D_MODEL, D_EXPERT), lo),
            pl.BlockSpec((1, D_MODEL, D_EXPERT), lambda i, e: (e, 0, 0)),
            pl.BlockSpec((1, D_MODEL, D_EXPERT), lo),
            pl.BlockSpec((1, D_EXPERT, D_MODEL), lambda i, e: (e, 0, 0)),
            pl.BlockSpec((1, D_EXPERT, D_MODEL), lo),
        ],
        out_specs=pl.BlockSpec((tm, D_MODEL), lambda i, e: (i, 0)),
        out_shape=jax.ShapeDtypeStruct((n, D_MODEL), F32),
        scratch_shapes=[pltpu.VMEM((tm, D_MODEL), BF16), pltpu.VMEM((tm, D_MODEL), BF16)],
        compiler_params=pltpu.CompilerParams(
            dimension_semantics=("parallel", "arbitrary"), vmem_limit_bytes=VMEM_LIMIT),
        name="moe",
    )(u2, cw, w1h, w1l, w3h, w3l, w2h, w2l)


def _ple_kernel(h_ref, y_ref, p_ref, g_ref, wgh_ref, wgl_ref, wph_ref, wpl_ref, gf_ref, o_ref, *, final, hp):
    h3 = h_ref[...] + y_ref[...]
    u = _rms(h3, g_ref[...])
    gate = jax.nn.sigmoid(_dotp(u, wgh_ref[...], wgl_ref[...], hp))
    h4 = h3 + gate * _dotp(p_ref[...], wph_ref[...], wpl_ref[...], hp)
    if final:
        h4 = _rms(h4, gf_ref[...])
    o_ref[...] = h4


def _ple(h2, y, p, g, wgh, wgl, wph, wpl, gf, tm, final, hp):
    n = h2.shape[0]
    row = lambda i: (i, 0)
    fixed = lambda i: (0, 0)
    return pl.pallas_call(
        functools.partial(_ple_kernel, final=final, hp=hp),
        grid=(n // tm,),
        in_specs=[
            pl.BlockSpec((tm, D_MODEL), row),
            pl.BlockSpec((tm, D_MODEL), row),
            pl.BlockSpec((tm, PLE_DIM), row),
            pl.BlockSpec((1, D_MODEL), fixed),
            pl.BlockSpec((D_MODEL, D_MODEL), fixed),
            pl.BlockSpec((D_MODEL, D_MODEL), fixed),
            pl.BlockSpec((PLE_DIM, D_MODEL), fixed),
            pl.BlockSpec((PLE_DIM, D_MODEL), fixed),
            pl.BlockSpec((1, D_MODEL), fixed),
        ],
        out_specs=pl.BlockSpec((tm, D_MODEL), row),
        out_shape=jax.ShapeDtypeStruct((n, D_MODEL), F32),
        compiler_params=pltpu.CompilerParams(
            dimension_semantics=("parallel",), vmem_limit_bytes=VMEM_LIMIT),
        name="ple",
    )(h2, y, p, g, wgh, wgl, wph, wpl, gf)


def _head_rmsnorm(x, g):
    y = x * lax.rsqrt(jnp.mean(x * x, axis=-1, keepdims=True) + EPS)
    return y * g.reshape((-1, x.shape[-1]))


def _l2norm(x):
    return x * lax.rsqrt(jnp.sum(x * x, axis=-1, keepdims=True) + EPS)


def _masked_exp(logit, mask):
    return jnp.where(mask, jnp.exp(jnp.where(mask, logit, 0.0)), 0.0)


def _causal_conv(x, w, prev):
    t = x.shape[1]
    xp = jnp.concatenate([prev.astype(x.dtype), x], axis=1)
    y = xp[:, 0:t] * w[0]
    for j in range(1, w.shape[0]):
        y = y + xp[:, j:j + t] * w[j]
    return y, xp[:, t:]


def _pad_time(t, pad):
    return jnp.pad(t, [(0, 0), (0, pad)] + [(0, 0)] * (t.ndim - 2))


def _to_chunks(t, chunk):
    b, tt, h = t.shape[:3]
    t = t.reshape((b, tt // chunk, chunk, h) + t.shape[3:])
    return jnp.moveaxis(t, 3, 1)


def _from_chunks(t):
    b, h, n, c = t.shape[:4]
    t = jnp.moveaxis(t, 1, 3)
    return t.reshape((b, n * c, h) + t.shape[4:])


def _chunk_inputs(arrs, chunk):
    t = arrs[0].shape[1]
    pad = (-t) % chunk
    return [_to_chunks(_pad_time(a.astype(F32), pad), chunk) for a in arrs]


def _chunked_gla(q, k, v, log_f, s0, chunk):
    t = q.shape[1]
    q, k, v, log_f = _chunk_inputs([q, k, v, log_f], chunk)
    b = jnp.cumsum(log_f, axis=3)
    causal = jnp.tril(jnp.ones((chunk, chunk), bool))[:, :, None]
    rel = _masked_exp(b[..., :, None, :] - b[..., None, :, :], causal)
    att = jnp.einsum('bhntk,bhnsk,bhntsk->bhnts', q, k, rel)
    o_intra = jnp.einsum('bhnts,bhnsv->bhntv', att, v)
    b_last = b[..., -1:, :]
    ds = jnp.einsum('bhnsk,bhnsv->nbhkv', k * jnp.exp(b_last - b), v)
    a_chunk = jnp.moveaxis(jnp.exp(b_last[..., 0, :]), 2, 0)

    def step(s, inp):
        a, d = inp
        return a[..., None] * s + d, s

    s_final, s_prev = lax.scan(step, s0.astype(F32), (a_chunk, ds))
    o_inter = jnp.einsum('bhntk,nbhkv->bhntv', q * jnp.exp(b), s_prev)
    return _from_chunks(o_intra + o_inter)[:, :t], s_final


def _chunked_ssd(q, k, v, log_a, s0, chunk):
    t = q.shape[1]
    q, k, v, la = _chunk_inputs([q, k, v, log_a], chunk)
    b = jnp.cumsum(la, axis=-1)
    causal = jnp.tril(jnp.ones((chunk, chunk), bool))
    rel = _masked_exp(b[..., :, None] - b[..., None, :], causal)
    att = jnp.einsum('bhntk,bhnsk->bhnts', q, k) * rel
    o_intra = jnp.einsum('bhnts,bhnsv->bhntv', att, v)
    b_last = b[..., -1:]
    ds = jnp.einsum('bhnsk,bhnsv->nbhkv', k * jnp.exp(b_last - b)[..., None], v)
    a_chunk = jnp.moveaxis(jnp.exp(b_last[..., 0]), 2, 0)

    def step(s, inp):
        a, d = inp
        return a[..., None, None] * s + d, s

    s_final, s_prev = lax.scan(step, s0.astype(F32), (a_chunk, ds))
    o_inter = jnp.einsum('bhntk,nbhkv->bhntv', q * jnp.exp(b)[..., None], s_prev)
    return _from_chunks(o_intra + o_inter)[:, :t], s_final


def _chunked_gated_delta(q, k, v, beta, log_a, s0, chunk):
    t = q.shape[1]
    vd = v.shape[-1]
    q, k, v, beta, la = _chunk_inputs([q, k, v, beta, log_a], chunk)
    b = jnp.cumsum(la, axis=-1)
    causal = jnp.tril(jnp.ones((chunk, chunk), bool))
    strict = jnp.tril(jnp.ones((chunk, chunk), bool), k=-1)
    decay = _masked_exp(b[..., :, None] - b[..., None, :], causal)
    kb = k * beta[..., None]
    m = jnp.where(strict, jnp.einsum('bhntk,bhnsk->bhnts', kb, k) * decay, 0.0)
    rhs = jnp.concatenate([v * beta[..., None], kb * jnp.exp(b)[..., None]], axis=-1)
    sol = lax.linalg.triangular_solve(m + jnp.eye(chunk, dtype=F32), rhs, left_side=True,
                                      lower=True, unit_diagonal=True)
    u, w = sol[..., :vd], sol[..., vd:]
    att = jnp.einsum('bhntk,bhnsk->bhnts', q, k) * decay
    q_dec = q * jnp.exp(b)[..., None]
    k_dec = k * jnp.exp(b[..., -1:] - b)[..., None]
    a_chunk = jnp.exp(b[..., -1])
    xs = tuple(jnp.moveaxis(a, 2, 0) for a in (u, w, att, q_dec, k_dec, a_chunk))

    def step(s, inp):
        u_c, w_c, att_c, qd_c, kd_c, a_c = inp
        v_new = u_c - jnp.einsum('bhtk,bhkv->bhtv', w_c, s)
        o = jnp.einsum('bhtk,bhkv->bhtv', qd_c, s) + jnp.einsum('bhts,bhsv->bhtv', att_c, v_new)
        s = a_c[..., None, None] * s + jnp.einsum('bhsk,bhsv->bhkv', kd_c, v_new)
        return s, o

    s_final, o = lax.scan(step, s0.astype(F32), xs)
    return _from_chunks(jnp.moveaxis(o, 0, 2))[:, :t], s_final


def _hgrn2_mixer(pa, lb, norm_g, s0):
    bsz, t, _ = pa.shape
    w = GROUP_WIDTH
    q, zf, inp, g = (pa[..., j * w:(j + 1) * w] for j in range(4))
    lb = jnp.clip(lb, 0.0, LB_CEIL)
    log_f = jnp.logaddexp(jnp.log(jnp.maximum(lb, LB_FLOOR)), jnp.log1p(-lb) + jax.nn.log_sigmoid(zf))
    k = (1.0 - lb) * jax.nn.sigmoid(-zf)
    heads = lambda a: a.reshape(bsz, t, N_HEADS, -1)
    o, s = _chunked_gla(heads(q) * HEAD_DIM ** -0.5, heads(k), heads(inp), heads(log_f), s0, CHUNK_A)
    o = _head_rmsnorm(o, norm_g) * jax.nn.silu(heads(g))
    return o.reshape(bsz, t, w), s


def _gdn_mixer(pb, beta_pre, dt_pre, conv_w, conv_prev, a_log, dt_bias, norm_g, s0):
    bsz, t, _ = pb.shape
    w = GROUP_WIDTH
    qkv, conv_new = _causal_conv(pb[..., :3 * w], conv_w, conv_prev)
    qkv = jax.nn.silu(qkv)
    heads = lambda a: a.reshape(bsz, t, N_HEADS, -1)
    q = _l2norm(heads(qkv[..., :w])) * HEAD_DIM ** -0.5
    k = _l2norm(heads(qkv[..., w:2 * w]))
    v = heads(qkv[..., 2 * w:3 * w])
    gate = heads(pb[..., 3 * w:4 * w])
    beta = jax.nn.sigmoid(beta_pre)
    log_a = -jnp.exp(a_log) * jax.nn.softplus(dt_pre + dt_bias)
    o, s = _chunked_gated_delta(q, k, v, beta, log_a, s0, CHUNK_B)
    o = _head_rmsnorm(o, norm_g) * jax.nn.silu(gate)
    return o.reshape(bsz, t, w), s, conv_new


def _fox_attention_prompt(q, k, v, log_f):
    b, t, h, d = q.shape
    n_blk = -(-t // Q_BLOCK)
    c = jnp.moveaxis(jnp.cumsum(log_f, axis=1), 2, 1)
    key_pos = jnp.arange(t)

    def block(i):
        start = i * Q_BLOCK
        q_i = lax.dynamic_slice_in_dim(q, start, Q_BLOCK, axis=1)
        c_i = lax.dynamic_slice_in_dim(c, start, Q_BLOCK, axis=2)
        s = jnp.einsum('bqhd,bkhd->bhqk', q_i, k) * (d ** -0.5) + c_i[..., :, None] - c[:, :, None, :]
        allowed = (start + jnp.arange(Q_BLOCK))[:, None] >= key_pos[None, :]
        p = jax.nn.softmax(jnp.where(allowed, s, NEG_BIG), axis=-1)
        return jnp.einsum('bhqk,bkhd->bqhd', p, v)

    o = lax.map(block, jnp.arange(n_blk))
    return jnp.moveaxis(o, 0, 1).reshape(b, n_blk * Q_BLOCK, h, d)[:, :t]


def _fox_attention_sample(q, k, v, log_f, k_past, v_past, logf_past):
    n_past = k_past.shape[1]
    s_new = q.shape[1]
    d = q.shape[-1]
    k_all = jnp.concatenate([k_past, k], axis=1)
    v_all = jnp.concatenate([v_past, v], axis=1)
    c = jnp.cumsum(jnp.concatenate([logf_past, log_f], axis=1), axis=1)
    c = jnp.moveaxis(c, 2, 1)
    s = jnp.einsum('bqhd,bkhd->bhqk', q, k_all) * (d ** -0.5) + c[:, :, n_past:, None] - c[:, :, None, :]
    allowed = (n_past + jnp.arange(s_new))[:, None] >= jnp.arange(n_past + s_new)[None, :]
    p = jax.nn.softmax(jnp.where(allowed, s, NEG_BIG), axis=-1)
    return jnp.einsum('bhqk,bkhd->bqhd', p, v_all)


def _fox_mixer(pc, lf_pre, b_f, q_norm, k_norm, out_norm, past):
    bsz, t, _ = pc.shape
    w = GROUP_WIDTH
    heads = lambda a: a.reshape(bsz, t, N_HEADS, HEAD_DIM)
    q = _head_rmsnorm(heads(pc[..., :w]), q_norm)
    k = _head_rmsnorm(heads(pc[..., w:2 * w]), k_norm)
    v = heads(pc[..., 2 * w:3 * w])
    g = heads(pc[..., 3 * w:4 * w])
    log_f = jax.nn.log_sigmoid(lf_pre + b_f)
    if past is None:
        o = _fox_attention_prompt(q, k, v, log_f)
    else:
        o = _fox_attention_sample(q, k, v, log_f, past[0], past[1], past[2])
    o = _head_rmsnorm(o, out_norm) * jax.nn.sigmoid(g)
    return o.reshape(bsz, t, w), k, v, log_f


def _ssd_mixer(pd, dt_pre, conv_w, conv_b, conv_prev, a_log, dt_bias, d_skip, norm_g, s0):
    bsz, t, _ = pd.shape
    w = GROUP_WIDTH
    gn = D_GROUPS * D_STATE
    rep = N_HEADS // D_GROUPS
    z = pd[..., :w].reshape(bsz, t, N_HEADS, HEAD_DIM)
    xbc, conv_new = _causal_conv(pd[..., w:w + D_CONV_CH], conv_w, conv_prev)
    xbc = jax.nn.silu(xbc + conv_b)
    xs = xbc[..., :w].reshape(bsz, t, N_HEADS, HEAD_DIM)
    bm = jnp.repeat(xbc[..., w:w + gn].reshape(bsz, t, D_GROUPS, D_STATE), rep, axis=2)
    cm = jnp.repeat(xbc[..., w + gn:].reshape(bsz, t, D_GROUPS, D_STATE), rep, axis=2)
    dt = jax.nn.softplus(dt_pre + dt_bias)
    log_a = -jnp.exp(a_log) * dt
    y, s = _chunked_ssd(cm, bm, xs * dt[..., None], log_a, s0, CHUNK_D)
    y = (y + xs * d_skip[:, None]) * jax.nn.silu(z)
    y = _head_rmsnorm(y.reshape(bsz, t, D_GROUPS, -1), norm_g)
    return y.reshape(bsz, t, w), s, conv_new


def _gather_pages(pool, page_table):
    g = pool[page_table]
    return g.reshape((g.shape[0], g.shape[1] * g.shape[2]) + g.shape[3:])


def _prep_weights(prm):
    w_in = prm['w_in']
    wmh, wml = _split_bf16(w_in[:, :, _MAIN_COLS])
    ws = jnp.pad(w_in[:, :, _SMALL_COLS], ((0, 0), (0, 0), (0, LANES - len(_SMALL_COLS))))
    wsh, wsl = _split_bf16(ws)
    wr = jnp.concatenate([prm['moe_w_expert'], prm['moe_w_group']], axis=-1)
    n_r = N_EXPERTS + N_EXPERT_GROUPS
    wr = jnp.pad(wr, ((0, 0), (0, 0), (0, LANES - n_r)))
    wrh, wrl = _split_bf16(wr)
    br = jnp.pad(jnp.concatenate([prm['moe_b_expert'], prm['moe_b_group']], axis=-1),
                 ((0, 0), (0, LANES - n_r)))[:, None, :]
    wts = dict(wmh=wmh, wml=wml, wsh=wsh, wsl=wsl, wrh=wrh, wrl=wrl, br=br)
    for name, key in (('wo', 'w_out'), ('w1', 'moe_w1'), ('w3', 'moe_w3'), ('w2', 'moe_w2'),
                      ('wg', 'ple_w_gate'), ('wp', 'ple_w_proj')):
        wts[name + 'h'], wts[name + 'l'] = _split_bf16(prm[key])
    return wts


def _trunk(x, p, init_state, fox_cache, lb_all, prm, wts, tm, hp_layers):
    s_hgrn0, s_gdn0, c_gdn0, s_ssd0, c_ssd0 = init_state
    bsz, t, _ = x.shape
    n = bsz * t
    h = x.reshape(n, D_MODEL)
    outs = [[] for _ in range(8)]
    row = lambda a: a.reshape(1, -1)
    for l in range(DEPTH):
        hp_mix, hp_ffn = hp_layers[l]
        pm, ps = _inproj(h, row(prm['g_mix'][l]), wts['wmh'][l], wts['wml'][l], wts['wsh'][l],
                         wts['wsl'][l], tm, hp_mix)
        pm = pm.reshape(bsz, t, N_MAIN)
        ps = ps.reshape(bsz, t, LANES)
        with jax.default_matmul_precision('highest' if hp_mix else 'default'):
            o_a, s_a = _hgrn2_mixer(pm[..., 0:1024], lb_all[l], prm['hgrn_norm'][l], s_hgrn0[l])
            o_b, s_b, c_b = _gdn_mixer(pm[..., 1024:2048], ps[..., 0:4], ps[..., 4:8], prm['gdn_conv_w'][l],
                                       c_gdn0[l], prm['gdn_a_log'][l], prm['gdn_dt_bias'][l],
                                       prm['gdn_norm'][l], s_gdn0[l])
            if fox_cache is None:
                past = None
            else:
                cache_k, cache_v, cache_logf, page_table = fox_cache
                past = (_gather_pages(cache_k[l], page_table), _gather_pages(cache_v[l], page_table),
                        _gather_pages(cache_logf[l], page_table))
            o_c, k_c, v_c, lf_c = _fox_mixer(pm[..., 2048:3072], ps[..., 8:12], prm['fox_b_f'][l],
                                             prm['fox_q_norm'][l], prm['fox_k_norm'][l],
                                             prm['fox_out_norm'][l], past)
            o_d, s_d, c_d = _ssd_mixer(pm[..., 3072:4096], ps[..., 12:16], prm['ssd_conv_w'][l],
                                       prm['ssd_conv_b'][l], c_ssd0[l], prm['ssd_a_log'][l],
                                       prm['ssd_dt_bias'][l], prm['ssd_d'][l], prm['ssd_norm'][l], s_ssd0[l])
        mix = jnp.concatenate([o_a, o_b, o_c, o_d], axis=-1).reshape(n, D_MODEL)
        h2, u2, cw = _outproj(h, mix, wts['woh'][l], wts['wol'][l], row(prm['g_ffn'][l]), wts['wrh'][l],
                              wts['wrl'][l], wts['br'][l], tm, hp_mix)
        y = _moe(u2, cw, wts['w1h'][l], wts['w1l'][l], wts['w3h'][l], wts['w3l'][l], wts['w2h'][l],
                 wts['w2l'][l], tm, hp_ffn)
        h = _ple(h2, y, p[l].reshape(n, PLE_DIM), row(prm['g_ple'][l]), wts['wgh'][l], wts['wgl'][l],
                 wts['wph'][l], wts['wpl'][l], row(prm['g_final']), tm, final=(l == DEPTH - 1), hp=hp_ffn)
        for acc, val in zip(outs, (k_c, v_c, lf_c, s_a, s_b, c_b, s_d, c_d)):
            acc.append(val)
    return (h.reshape(bsz, t, D_MODEL),) + tuple(jnp.stack(acc) for acc in outs)


def _hgrn_lower_bounds(lb_param):
    sm = jax.nn.softmax(lb_param, axis=0)
    return jnp.concatenate([jnp.zeros_like(sm[:1]), jnp.cumsum(sm[1:], axis=0)], axis=0)


def kernel(x_prompt, x_sample, cache_fox_k, cache_fox_v, cache_fox_logf, state_hgrn, state_gdn,
           state_gdn_conv, state_ssd, state_ssd_conv, page_table, p_prompt, p_sample, w_in, w_out,
           g_mix, g_ffn, g_ple, g_final, hgrn_lb, hgrn_norm, gdn_conv_w, gdn_a_log, gdn_dt_bias,
           gdn_norm, fox_b_f, fox_q_norm, fox_k_norm, fox_out_norm, ssd_conv_w, ssd_conv_b, ssd_a_log,
           ssd_dt_bias, ssd_d, ssd_norm, moe_w_group, moe_b_group, moe_w_expert, moe_b_expert, moe_w1,
           moe_w3, moe_w2, ple_w_gate, ple_w_proj):
    prm = dict(w_in=w_in, w_out=w_out, g_mix=g_mix, g_ffn=g_ffn, g_ple=g_ple, g_final=g_final,
               hgrn_norm=hgrn_norm, gdn_conv_w=gdn_conv_w, gdn_a_log=gdn_a_log, gdn_dt_bias=gdn_dt_bias,
               gdn_norm=gdn_norm, fox_b_f=fox_b_f, fox_q_norm=fox_q_norm, fox_k_norm=fox_k_norm,
               fox_out_norm=fox_out_norm, ssd_conv_w=ssd_conv_w, ssd_conv_b=ssd_conv_b, ssd_a_log=ssd_a_log,
               ssd_dt_bias=ssd_dt_bias, ssd_d=ssd_d, ssd_norm=ssd_norm, moe_w_group=moe_w_group,
               moe_b_group=moe_b_group, moe_w_expert=moe_w_expert, moe_b_expert=moe_b_expert,
               moe_w1=moe_w1, moe_w3=moe_w3, moe_w2=moe_w2, ple_w_gate=ple_w_gate, ple_w_proj=ple_w_proj)
    wts = _prep_weights(prm)
    lb_all = _hgrn_lower_bounds(hgrn_lb)
    bp = x_prompt.shape[0]
    zero_state = (jnp.zeros((DEPTH, bp, N_HEADS, HEAD_DIM, HEAD_DIM), F32),
                  jnp.zeros((DEPTH, bp, N_HEADS, HEAD_DIM, HEAD_DIM), F32),
                  jnp.zeros((DEPTH, bp, CONV_WIDTH - 1, B_CONV_CH), F32),
                  jnp.zeros((DEPTH, bp, N_HEADS, D_STATE, HEAD_DIM), F32),
                  jnp.zeros((DEPTH, bp, CONV_WIDTH - 1, D_CONV_CH), F32))
    pr = _trunk(x_prompt, p_prompt, zero_state, None, lb_all, prm, wts, tm=512,
                hp_layers=((True, False), (False, False)))
    sm = _trunk(x_sample, p_sample, (state_hgrn, state_gdn, state_gdn_conv, state_ssd, state_ssd_conv),
                (cache_fox_k, cache_fox_v, cache_fox_logf, page_table), lb_all, prm, wts, tm=32,
                hp_layers=((True, True), (True, True)))
    return (pr[0], sm[0]) + tuple(pr[1:]) + tuple(sm[1:])
```

```python
import functools
import math

import jax
import jax.numpy as jnp
import numpy as np
from jax import lax
from jax.experimental import pallas as pl
from jax.experimental.pallas import tpu as pltpu

F32 = jnp.float32
BF16 = jnp.bfloat16

D_MODEL = 1024
DEPTH = 2
PAGE_SIZE = 128
EPS = 1e-6
NEG_BIG = -1e30
LB_FLOOR = 1e-30
LB_CEIL = 1.0 - 1e-6
PLE_DIM = 256
GROUP_WIDTH = 256
HEAD_DIM = 64
N_HEADS = 4
D_GROUPS = 2
D_STATE = 128
CONV_WIDTH = 4
B_CONV_CH = 3 * GROUP_WIDTH
D_CONV_CH = GROUP_WIDTH + 2 * D_GROUPS * D_STATE
SIZE_A = 4 * GROUP_WIDTH
SIZE_B = 4 * GROUP_WIDTH + 2 * N_HEADS
SIZE_C = 4 * GROUP_WIDTH + N_HEADS
SIZE_D = GROUP_WIDTH + D_CONV_CH + N_HEADS
OFF_B = SIZE_A
OFF_C = OFF_B + SIZE_B
OFF_D = OFF_C + SIZE_C
N_IN = OFF_D + SIZE_D
CHUNK_A = 16
CHUNK_B = 64
CHUNK_D = 128
Q_BLOCK = 128
N_EXPERT_GROUPS = 4
EXPERTS_PER_GROUP = 4
N_EXPERTS = 16
D_EXPERT = 512

LANES = 128
N_MAIN = 4096
VMEM_LIMIT = 48 * 1024 * 1024

_MAIN_COLS = np.concatenate([
    np.arange(0, SIZE_A),
    np.arange(OFF_B, OFF_B + 4 * GROUP_WIDTH),
    np.arange(OFF_C, OFF_C + 4 * GROUP_WIDTH),
    np.arange(OFF_D, OFF_D + 4 * GROUP_WIDTH),
])
_SMALL_COLS = np.concatenate([
    np.arange(OFF_B + 4 * GROUP_WIDTH, OFF_B + SIZE_B),
    np.arange(OFF_C + 4 * GROUP_WIDTH, OFF_C + SIZE_C),
    np.arange(OFF_D + 4 * GROUP_WIDTH, OFF_D + SIZE_D),
])


def _split_bf16(x):
    hi = x.astype(BF16)
    lo = (x - hi.astype(F32)).astype(BF16)
    return hi, lo


def _split_weights(w):
    u = lax.bitcast_convert_type(w, jnp.uint32)
    r = (u + jnp.uint32(0x7FFF) + ((u >> 16) & jnp.uint32(1))) & jnp.uint32(0xFFFF0000)
    hi = lax.bitcast_convert_type(r, F32)
    return hi.astype(BF16), (w - hi).astype(BF16)


def _dot(a, b):
    return jnp.dot(a, b, preferred_element_type=F32)


def _dot3(a_hi, a_lo, b_hi, b_lo):
    return _dot(a_hi, b_hi) + _dot(a_lo, b_hi) + _dot(a_hi, b_lo)


def _dotp(a, b_hi, b_lo, hp):
    if hp:
        a_hi, a_lo = _split_bf16(a)
        return _dot3(a_hi, a_lo, b_hi, b_lo)
    return _dot(a.astype(BF16), b_hi)


def _rms(x, g):
    return x * lax.rsqrt(jnp.mean(x * x, axis=-1, keepdims=True) + EPS) * g


def _inproj_kernel(x_ref, g_ref, wmh_ref, wml_ref, wsh_ref, wsl_ref, om_ref, os_ref, uh_ref, ul_ref, *, hp):
    @pl.when(pl.program_id(1) == 0)
    def _():
        u = _rms(x_ref[...], g_ref[...])
        uh, ul = _split_bf16(u)
        uh_ref[...] = uh
        ul_ref[...] = ul
        os_ref[...] = _dot3(uh, ul, wsh_ref[...], wsl_ref[...])

    if hp:
        om_ref[...] = _dot3(uh_ref[...], ul_ref[...], wmh_ref[...], wml_ref[...])
    else:
        om_ref[...] = _dot(uh_ref[...], wmh_ref[...])


def _inproj(x, g, wmh, wml, wsh, wsl, tm, hp):
    n = x.shape[0]
    tn = 1024
    return pl.pallas_call(
        functools.partial(_inproj_kernel, hp=hp),
        grid=(n // tm, N_MAIN // tn),
        in_specs=[
            pl.BlockSpec((tm, D_MODEL), lambda i, j: (i, 0)),
            pl.BlockSpec((1, D_MODEL), lambda i, j: (0, 0)),
            pl.BlockSpec((D_MODEL, tn), lambda i, j: (0, j)),
            pl.BlockSpec((D_MODEL, tn), lambda i, j: (0, j if hp else 0)),
            pl.BlockSpec((D_MODEL, LANES), lambda i, j: (0, 0)),
            pl.BlockSpec((D_MODEL, LANES), lambda i, j: (0, 0)),
        ],
        out_specs=[
            pl.BlockSpec((tm, tn), lambda i, j: (i, j)),
            pl.BlockSpec((tm, LANES), lambda i, j: (i, 0)),
        ],
        out_shape=[
            jax.ShapeDtypeStruct((n, N_MAIN), F32),
            jax.ShapeDtypeStruct((n, LANES), F32),
        ],
        scratch_shapes=[pltpu.VMEM((tm, D_MODEL), BF16), pltpu.VMEM((tm, D_MODEL), BF16)],
        compiler_params=pltpu.CompilerParams(
            dimension_semantics=("parallel", "arbitrary"), vmem_limit_bytes=VMEM_LIMIT),
        name="inproj",
    )(x, g, wmh, wml, wsh, wsl)


def _route(logits):
    lane = lax.broadcasted_iota(jnp.int32, logits.shape, 1)
    gmask = (lane >= N_EXPERTS) & (lane < N_EXPERTS + N_EXPERT_GROUPS)
    gl = jnp.where(gmask, logits, -jnp.inf)
    gmax = jnp.max(gl, axis=-1, keepdims=True)
    gidx = jnp.min(jnp.where(gl == gmax, lane, 4 * LANES), axis=-1, keepdims=True) - N_EXPERTS
    gw = 1.0 / jnp.sum(jnp.where(gmask, jnp.exp(gl - gmax), 0.0), axis=-1, keepdims=True)
    lo = gidx * EXPERTS_PER_GROUP
    emask = (lane >= lo) & (lane < lo + EXPERTS_PER_GROUP)
    el = jnp.where(emask, logits, -jnp.inf)
    m1 = jnp.max(el, axis=-1, keepdims=True)
    i1 = jnp.min(jnp.where(el == m1, lane, 4 * LANES), axis=-1, keepdims=True)
    el2 = jnp.where(lane == i1, -jnp.inf, el)
    m2 = jnp.max(el2, axis=-1, keepdims=True)
    i2 = jnp.min(jnp.where(el2 == m2, lane, 4 * LANES), axis=-1, keepdims=True)
    e2 = jnp.exp(m2 - m1)
    den = 1.0 + e2
    g1 = gw / den
    g2 = gw * e2 / den
    return jnp.where(lane == i1, g1, jnp.where(lane == i2, g2, 0.0))


def _outproj_kernel(h_ref, mix_ref, woh_ref, wol_ref, g_ref, wrh_ref, wrl_ref, br_ref,
                    h2_ref, u2_ref, cw_ref, *, hp):
    h2 = h_ref[...] + _dotp(mix_ref[...], woh_ref[...], wol_ref[...], hp)
    h2_ref[...] = h2
    u = _rms(h2, g_ref[...])
    u2_ref[...] = u
    uh, ul = _split_bf16(u)
    logits = _dot3(uh, ul, wrh_ref[...], wrl_ref[...]) + br_ref[...]
    cw_ref[...] = _route(logits)


def _outproj(h, mix, woh, wol, g, wrh, wrl, br, tm, hp):
    n = h.shape[0]
    row = lambda i: (i, 0)
    fixed = lambda i: (0, 0)
    return pl.pallas_call(
        functools.partial(_outproj_kernel, hp=hp),
        grid=(n // tm,),
        in_specs=[
            pl.BlockSpec((tm, D_MODEL), row),
            pl.BlockSpec((tm, D_MODEL), row),
            pl.BlockSpec((D_MODEL, D_MODEL), fixed),
            pl.BlockSpec((D_MODEL, D_MODEL), fixed),
            pl.BlockSpec((1, D_MODEL), fixed),
            pl.BlockSpec((D_MODEL, LANES), fixed),
            pl.BlockSpec((D_MODEL, LANES), fixed),
            pl.BlockSpec((1, LANES), fixed),
        ],
        out_specs=[
            pl.BlockSpec((tm, D_MODEL), row),
            pl.BlockSpec((tm, D_MODEL), row),
            pl.BlockSpec((tm, LANES), row),
        ],
        out_shape=[
            jax.ShapeDtypeStruct((n, D_MODEL), F32),
            jax.ShapeDtypeStruct((n, D_MODEL), F32),
            jax.ShapeDtypeStruct((n, LANES), F32),
        ],
        compiler_params=pltpu.CompilerParams(
            dimension_semantics=("parallel",), vmem_limit_bytes=VMEM_LIMIT),
        name="outproj",
    )(h, mix, woh, wol, g, wrh, wrl, br)


def _moe_kernel(x_ref, cw_ref, w1h_ref, w1l_ref, w3h_ref, w3l_ref, w2h_ref, w2l_ref, y_ref,
                xh_ref, xl_ref, *, hp):
    e = pl.program_id(1)

    @pl.when(e == 0)
    def _():
        y_ref[...] = jnp.zeros_like(y_ref)
        xh, xl = _split_bf16(x_ref[...])
        xh_ref[...] = xh
        xl_ref[...] = xl

    cw = cw_ref[...]
    lane = lax.broadcasted_iota(jnp.int32, cw.shape, 1)
    col = jnp.sum(jnp.where(lane == e, cw, 0.0), axis=-1, keepdims=True)
    xh = xh_ref[...]
    if hp:
        xl = xl_ref[...]
        a = _dot3(xh, xl, w1h_ref[0], w1l_ref[0])
        b = _dot3(xh, xl, w3h_ref[0], w3l_ref[0])
    else:
        a = _dot(xh, w1h_ref[0])
        b = _dot(xh, w3h_ref[0])
    hid = (a * jax.nn.sigmoid(a)) * b
    y_ref[...] += col * _dotp(hid, w2h_ref[0], w2l_ref[0], hp)


def _moe(u2, cw, w1h, w1l, w3h, w3l, w2h, w2l, tm, hp):
    n = u2.shape[0]
    lo = (lambda i, e: (e, 0, 0)) if hp else (lambda i, e: (0, 0, 0))
    return pl.pallas_call(
        functools.partial(_moe_kernel, hp=hp),
        grid=(n // tm, N_EXPERTS),
        in_specs=[
            pl.BlockSpec((tm, D_MODEL), lambda i, e: (i, 0)),
            pl.BlockSpec((tm, LANES), lambda i, e: (i, 0)),
            pl.BlockSpec((1, D_MODEL, D_EXPERT), lambda i, e: (e, 0, 0)),
            pl.BlockSpec((1, D_MODEL, D_EXPERT), lo),
            pl.BlockSpec((1, D_MODEL, D_EXPERT), lambda i, e: (e, 0, 0)),
            pl.BlockSpec((1, D_MODEL, D_EXPERT), lo),
            pl.BlockSpec((1, D_EXPERT, D_MODEL), lambda i, e: (e, 0, 0)),
            pl.BlockSpec((1, D_EXPERT, D_MODEL), lo),
        ],
        out_specs=pl.BlockSpec((tm, D_MODEL), lambda i, e: (i, 0)),
        out_shape=jax.ShapeDtypeStruct((n, D_MODEL), F32),
        scratch_shapes=[pltpu.VMEM((tm, D_MODEL), BF16), pltpu.VMEM((tm, D_MODEL), BF16)],
        compiler_params=pltpu.CompilerParams(
            dimension_semantics=("parallel", "arbitrary"), vmem_limit_bytes=VMEM_LIMIT),
        name="moe",
    )(u2, cw, w1h, w1l, w3h, w3l, w2h, w2l)


def _ple_kernel(h_ref, y_ref, p_ref, g_ref, wgh_ref, wgl_ref, wph_ref, wpl_ref, gf_ref, o_ref, *, final, hp):
    h3 = h_ref[...] + y_ref[...]
    u = _rms(h3, g_ref[...])
    gate = jax.nn.sigmoid(_dotp(u, wgh_ref[...], wgl_ref[...], hp))
    h4 = h3 + gate * _dotp(p_ref[...], wph_ref[...], wpl_ref[...], hp)
    if final:
        h4 = _rms(h4, gf_ref[...])
    o_ref[...] = h4


def _ple(h2, y, p, g, wgh, wgl, wph, wpl, gf, tm, final, hp):
    n = h2.shape[0]
    row = lambda i: (i, 0)
    fixed = lambda i: (0, 0)
    return pl.pallas_call(
        functools.partial(_ple_kernel, final=final, hp=hp),
        grid=(n // tm,),
        in_specs=[
            pl.BlockSpec((tm, D_MODEL), row),
            pl.BlockSpec((tm, D_MODEL), row),
            pl.BlockSpec((tm, PLE_DIM), row),
            pl.BlockSpec((1, D_MODEL), fixed),
            pl.BlockSpec((D_MODEL, D_MODEL), fixed),
            pl.BlockSpec((D_MODEL, D_MODEL), fixed),
            pl.BlockSpec((PLE_DIM, D_MODEL), fixed),
            pl.BlockSpec((PLE_DIM, D_MODEL), fixed),
            pl.BlockSpec((1, D_MODEL), fixed),
        ],
        out_specs=pl.BlockSpec((tm, D_MODEL), row),
        out_shape=jax.ShapeDtypeStruct((n, D_MODEL), F32),
        compiler_params=pltpu.CompilerParams(
            dimension_semantics=("parallel",), vmem_limit_bytes=VMEM_LIMIT),
        name="ple",
    )(h2, y, p, g, wgh, wgl, wph, wpl, gf)


FOX_LF_LANE = 8
FOX_TB = 256
FOX_TQ = 256


def _split3_bf16(x):
    hi = x.astype(BF16)
    r = x - hi.astype(F32)
    mid = r.astype(BF16)
    lo = (r - mid.astype(F32)).astype(BF16)
    return hi, mid, lo


def _log_sigmoid(x):
    return jnp.minimum(x, 0.0) - jnp.log1p(jnp.exp(-jnp.abs(x)))


def _group_mean(x2, ones_blk, width):
    hi, mid, lo = _split3_bf16(x2)
    return (_dot(hi, ones_blk) + _dot(mid, ones_blk) + _dot(lo, ones_blk)) * (1.0 / width)


def _fox_layout(hp):
    ka = 256 if hp else 128
    a0 = 192 if hp else 64
    nx = 512 + 3 * LANES
    mq = np.zeros((nx, 4 * ka), np.float32)
    mk = np.zeros((nx, 4 * ka), np.float32)
    rq = np.zeros((1, 4 * ka), np.float32)
    rk = np.zeros((1, 4 * ka), np.float32)
    mv = np.zeros((512, 4 * LANES), np.float32)
    for h in range(N_HEADS):
        for d in range(HEAD_DIM):
            src_hi, src_lo = 64 * h + d, 256 + 64 * h + d
            mq[src_hi, h * ka + d] = 1.0
            mk[src_hi, h * ka + d] = 1.0
            if hp:
                mq[src_lo, h * ka + 64 + d] = 1.0
                mq[src_hi, h * ka + 128 + d] = 1.0
                mk[src_hi, h * ka + 64 + d] = 1.0
                mk[src_lo, h * ka + 128 + d] = 1.0
            mv[src_hi, h * LANES + d] = 1.0
            if hp:
                mv[src_lo, h * LANES + 64 + d] = 1.0
        for part in range(3):
            src = 512 + part * LANES + FOX_LF_LANE + h
            mq[src, h * ka + a0 + part] = 1.0
            mk[src, h * ka + a0 + 3 + part] = -1.0
            rq[0, h * ka + a0 + 3 + part] = 1.0
            rk[0, h * ka + a0 + part] = 1.0
    blk = (np.arange(256)[:, None] // HEAD_DIM == np.arange(256)[None, :] // HEAD_DIM).astype(np.float32)
    tri = np.tril(np.ones((FOX_TB, FOX_TB), np.float32))
    as_bf = lambda a: jnp.asarray(a, BF16)
    return dict(ka=ka, mq=as_bf(mq), mk=as_bf(mk), mv=as_bf(mv), rq=jnp.asarray(rq), rk=jnp.asarray(rk),
                blk=as_bf(blk), tri=as_bf(tri))


def _fox_prep_kernel(pc_ref, ps_ref, bf_ref, qn_ref, kn_ref, blk_ref, tri_ref, mq_ref, mk_ref, mv_ref,
                     rq_ref, rk_ref, krow_ref, lf_ref, qa_ref, ka_ref, vv_ref, carry_ref, *, ka):
    @pl.when(pl.program_id(1) == 0)
    def _():
        carry_ref[...] = jnp.zeros_like(carry_ref)

    blk = blk_ref[...]
    q = pc_ref[:, 0:256]
    k = pc_ref[:, 256:512]
    v = pc_ref[:, 512:768]
    qn = q * lax.rsqrt(_group_mean(q * q, blk, HEAD_DIM) + EPS) * qn_ref[...]
    kn = k * lax.rsqrt(_group_mean(k * k, blk, HEAD_DIM) + EPS) * kn_ref[...]
    krow_ref[...] = kn
    lf = _log_sigmoid(ps_ref[...] + bf_ref[...])
    lf_ref[...] = lf
    l_hi, l_mid, l_lo = _split3_bf16(lf)
    tri = tri_ref[...]
    c = _dot(tri, l_hi) + _dot(tri, l_mid) + _dot(tri, l_lo) + carry_ref[...]
    carry_ref[...] = c[FOX_TB - 1:FOX_TB, :]
    c_hi, c_mid, c_lo = _split3_bf16(c)
    q_hi, q_lo = _split_bf16(qn * (HEAD_DIM ** -0.5))
    k_hi, k_lo = _split_bf16(kn)
    xq = jnp.concatenate([q_hi, q_lo, c_hi, c_mid, c_lo], axis=-1)
    xk = jnp.concatenate([k_hi, k_lo, c_hi, c_mid, c_lo], axis=-1)
    qa = (_dot(xq, mq_ref[...]) + rq_ref[...]).astype(BF16)
    kk = (_dot(xk, mk_ref[...]) + rk_ref[...]).astype(BF16)
    v_hi, v_lo = _split_bf16(v)
    vv = _dot(jnp.concatenate([v_hi, v_lo], axis=-1), mv_ref[...]).astype(BF16)
    for h in range(N_HEADS):
        qa_ref[0, h] = qa[:, h * ka:(h + 1) * ka]
        ka_ref[0, h] = kk[:, h * ka:(h + 1) * ka]
        vv_ref[0, h] = vv[:, h * LANES:(h + 1) * LANES]


def _fox_prep(pm, ps, bf_row, qn_row, kn_row, lay, bsz, t):
    ka = lay['ka']
    nt = t // FOX_TB
    fixed = lambda b, i: (0, 0)
    rows = lambda b, i: (b * nt + i, 0)
    hm = lambda b, i: (b, 0, i, 0)
    return pl.pallas_call(
        functools.partial(_fox_prep_kernel, ka=ka),
        grid=(bsz, nt),
        in_specs=[
            pl.BlockSpec((FOX_TB, 1024), lambda b, i: (b * nt + i, 2)),
            pl.BlockSpec((FOX_TB, LANES), rows),
            pl.BlockSpec((1, LANES), fixed),
            pl.BlockSpec((1, 256), fixed),
            pl.BlockSpec((1, 256), fixed),
            pl.BlockSpec((256, 256), fixed),
            pl.BlockSpec((FOX_TB, FOX_TB), fixed),
            pl.BlockSpec(lay['mq'].shape, fixed),
            pl.BlockSpec(lay['mk'].shape, fixed),
            pl.BlockSpec(lay['mv'].shape, fixed),
            pl.BlockSpec((1, 4 * ka), fixed),
            pl.BlockSpec((1, 4 * ka), fixed),
        ],
        out_specs=[
            pl.BlockSpec((FOX_TB, 256), rows),
            pl.BlockSpec((FOX_TB, LANES), rows),
            pl.BlockSpec((1, N_HEADS, FOX_TB, ka), hm),
            pl.BlockSpec((1, N_HEADS, FOX_TB, ka), hm),
            pl.BlockSpec((1, N_HEADS, FOX_TB, LANES), hm),
        ],
        out_shape=[
            jax.ShapeDtypeStruct((bsz * t, 256), F32),
            jax.ShapeDtypeStruct((bsz * t, LANES), F32),
            jax.ShapeDtypeStruct((bsz, N_HEADS, t, ka), BF16),
            jax.ShapeDtypeStruct((bsz, N_HEADS, t, ka), BF16),
            jax.ShapeDtypeStruct((bsz, N_HEADS, t, LANES), BF16),
        ],
        scratch_shapes=[pltpu.VMEM((1, LANES), F32)],
        compiler_params=pltpu.CompilerParams(
            dimension_semantics=("parallel", "arbitrary"), vmem_limit_bytes=VMEM_LIMIT),
        name="fox_prep",
    )(pm, ps, bf_row, qn_row, kn_row, lay['blk'], lay['tri'], lay['mq'], lay['mk'], lay['mv'],
      lay['rq'], lay['rk'])


def _fox_flash_kernel(qa_ref, ka_ref, vv_ref, g_ref, on_ref, o_ref, *, hp):
    i = pl.program_id(1)
    tq = FOX_TQ
    row = lax.broadcasted_iota(jnp.int32, (tq, tq), 0)
    col = lax.broadcasted_iota(jnp.int32, (tq, tq), 1)
    outs = []
    for h in range(N_HEADS):
        qa = qa_ref[0, h]

        def tile(j, carry, masked, h=h, qa=qa):
            m, l, acc = carry
            start = pl.multiple_of(j * tq, tq)
            kt = ka_ref[0, h, pl.ds(start, tq), :]
            s = lax.dot_general(qa, kt, (((1,), (1,)), ((), ())), preferred_element_type=F32)
            if masked:
                s = jnp.where(row >= col, s, NEG_BIG)
            m_new = jnp.maximum(m, jnp.max(s, axis=-1, keepdims=True))
            alpha = jnp.exp(m - m_new)
            p = jnp.exp(s - m_new)
            l = alpha * l + jnp.sum(p, axis=-1, keepdims=True)
            vt = vv_ref[0, h, pl.ds(start, tq), :]
            if hp:
                p_hi, p_lo = _split_bf16(p)
                pv = _dot(p_hi, vt) + _dot(p_lo, vt)
            else:
                pv = _dot(p.astype(BF16), vt)
            return m_new, l, alpha * acc + pv

        init = (jnp.full((tq, 1), NEG_BIG, F32), jnp.zeros((tq, 1), F32), jnp.zeros((tq, LANES), F32))
        carry = lax.fori_loop(0, i, lambda j, c: tile(j, c, False), init)
        m, l, acc = tile(i, carry, True)
        o = (acc[:, :HEAD_DIM] + acc[:, HEAD_DIM:]) / l
        o = o * lax.rsqrt(jnp.mean(o * o, axis=-1, keepdims=True) + EPS)
        sl = slice(h * HEAD_DIM, (h + 1) * HEAD_DIM)
        outs.append(o * on_ref[:, sl] * jax.nn.sigmoid(g_ref[:, sl]))
    o_ref[...] = jnp.concatenate(outs, axis=-1)


def _fox_flash(qa, ka, vv, pm, on_row, bsz, t, hp):
    kad = qa.shape[-1]
    nq = t // FOX_TQ
    whole = lambda b, i: (b, 0, 0, 0)
    return pl.pallas_call(
        functools.partial(_fox_flash_kernel, hp=hp),
        grid=(bsz, nq),
        in_specs=[
            pl.BlockSpec((1, N_HEADS, FOX_TQ, kad), lambda b, i: (b, 0, i, 0)),
            pl.BlockSpec((1, N_HEADS, t, kad), whole, pipeline_mode=pl.Buffered(1)),
            pl.BlockSpec((1, N_HEADS, t, LANES), whole, pipeline_mode=pl.Buffered(1)),
            pl.BlockSpec((FOX_TQ, 256), lambda b, i: (b * nq + i, 11)),
            pl.BlockSpec((1, 256), lambda b, i: (0, 0)),
        ],
        out_specs=pl.BlockSpec((FOX_TQ, 256), lambda b, i: (b * nq + i, 0)),
        out_shape=jax.ShapeDtypeStruct((bsz * t, 256), F32),
        compiler_params=pltpu.CompilerParams(
            dimension_semantics=("parallel", "arbitrary"), vmem_limit_bytes=56 * 1024 * 1024),
        name="fox_flash",
    )(qa, ka, vv, pm, on_row)


def _fox_prompt(pm, ps, b_f, q_norm, k_norm, out_norm, bsz, t, hp):
    lay = _fox_layout(hp)
    bf_row = jnp.zeros((1, LANES), F32).at[0, FOX_LF_LANE:FOX_LF_LANE + N_HEADS].set(b_f)
    tile4 = lambda g: jnp.tile(g, N_HEADS).reshape(1, 256)
    krow, lf, qa, ka, vv = _fox_prep(pm, ps, bf_row, tile4(q_norm), tile4(k_norm), lay, bsz, t)
    o = _fox_flash(qa, ka, vv, pm, out_norm.reshape(1, 256), bsz, t, hp)
    return o, krow, lf[:, FOX_LF_LANE:FOX_LF_LANE + N_HEADS]


CHUNK = 128
ROW0 = 8


def _mm(a, b, hp):
    if hp:
        a_hi, a_lo = _split_bf16(a)
        b_hi, b_lo = _split_bf16(b)
        return _dot3(a_hi, a_lo, b_hi, b_lo)
    return _dot(a.astype(BF16), b.astype(BF16))


def _mm_nt(a, b, hp):
    dn = (((1,), (1,)), ((), ()))
    if hp:
        a_hi, a_lo = _split_bf16(a)
        b_hi, b_lo = _split_bf16(b)
        a3 = jnp.concatenate([a_hi, a_lo, a_hi], axis=-1)
        b3 = jnp.concatenate([b_hi, b_hi, b_lo], axis=-1)
        return lax.dot_general(a3, b3, dn, preferred_element_type=F32)
    return lax.dot_general(a.astype(BF16), b.astype(BF16), dn, preferred_element_type=F32)


def _mm_exact(sel, x):
    hi, mid, lo = _split3_bf16(x)
    return _dot(sel, hi) + _dot(sel, mid) + _dot(sel, lo)


def _mm_exact_r(x, sel):
    hi, mid, lo = _split3_bf16(x)
    return _dot(hi, sel) + _dot(mid, sel) + _dot(lo, sel)


def _softplus(x):
    return jnp.maximum(x, 0.0) + jnp.log1p(jnp.exp(-jnp.abs(x)))


def _silu(x):
    return x * jax.nn.sigmoid(x)


def _stage_rows(buf, blk_ref, lo, hi, prev_ref, rows_in, first):
    @pl.when(first)
    def _():
        buf[ROW0 - 3:ROW0, :] = prev_ref[0]
        if rows_in < CHUNK:
            buf[ROW0 + rows_in:ROW0 + CHUNK, :] = jnp.zeros((CHUNK - rows_in, hi - lo), F32)

    @pl.when(jnp.logical_not(first))
    def _():
        buf[ROW0 - 3:ROW0, :] = buf[ROW0 + CHUNK - 3:ROW0 + CHUNK, :]

    buf[ROW0:ROW0 + rows_in, :] = blk_ref[:, lo:hi]


def _conv4(buf, w_ref):
    acc = w_ref[0:1, :] * buf[ROW0 - 3:ROW0 - 3 + CHUNK, :]
    for j in range(1, CONV_WIDTH):
        acc = acc + w_ref[j:j + 1, :] * buf[ROW0 - 3 + j:ROW0 - 3 + j + CHUNK, :]
    return acc


def _pad_rows(x, rows_in):
    if rows_in == CHUNK:
        return x
    return jnp.concatenate([x, jnp.zeros((CHUNK - rows_in, x.shape[1]), x.dtype)], axis=0)


def _head_expand(first_lane, width):
    e = np.zeros((LANES, N_HEADS * width), np.float32)
    for h in range(N_HEADS):
        e[first_lane + h, h * width:(h + 1) * width] = 1.0
    return jnp.asarray(e, BF16)


_TRI = np.tril(np.ones((CHUNK, CHUNK), np.float32))


SSD_DT_LANE = 12


def _ssd_kernel(pd_ref, ps_ref, cprev_ref, s0_ref, w_ref, cb_ref, dtb_ref, alog_ref, dsk_ref, gn_ref,
                e4_ref, tri_ref, y_ref, sfin_ref, buf, s_scr, *, rows_in, t_valid, hp):
    i = pl.program_id(1)
    first = i == 0

    @pl.when(first)
    def _():
        s_scr[...] = s0_ref[0]

    _stage_rows(buf, pd_ref, 256, 1024, cprev_ref, rows_in, first)
    xbc = _silu(_conv4(buf, w_ref) + cb_ref[...])
    xs = xbc[:, 0:256]
    bm = xbc[:, 256:512]
    cm = xbc[:, 512:768]
    z = _pad_rows(pd_ref[:, 0:256], rows_in)
    pre = _mm_exact_r(_pad_rows(ps_ref[...], rows_in), e4_ref[...])
    dt = _softplus(pre + dtb_ref[...])
    tpos = i * CHUNK + lax.broadcasted_iota(jnp.int32, (CHUNK, 1), 0)
    dt = jnp.where(tpos < t_valid, dt, 0.0)
    la = -jnp.exp(alog_ref[...]) * dt
    b = _mm_exact(tri_ref[...], la)
    row = lax.broadcasted_iota(jnp.int32, (CHUNK, CHUNK), 0)
    col = lax.broadcasted_iota(jnp.int32, (CHUNK, CHUNK), 1)
    causal = row >= col
    lane = lax.broadcasted_iota(jnp.int32, (1, 256), 1)
    s_prev = s_scr[...]
    cb = [_mm_nt(cm[:, g * 128:(g + 1) * 128], bm[:, g * 128:(g + 1) * 128], hp) for g in range(D_GROUPS)]
    y = jnp.zeros((CHUNK, 256), F32)
    s_new = jnp.zeros((D_STATE, 256), F32)
    for h in range(N_HEADS):
        g = h // (N_HEADS // D_GROUPS)
        bh = b[:, h * 128:(h + 1) * 128]
        dth = dt[:, h * 128:(h + 1) * 128]
        hmask = (lane >= h * HEAD_DIM) & (lane < (h + 1) * HEAD_DIM)
        dt2 = jnp.concatenate([dth, dth], axis=-1)
        xdt = jnp.where(hmask, xs * dt2, 0.0)
        rel = jnp.where(causal, jnp.exp(jnp.where(causal, bh - bh.T, 0.0)), 0.0)
        y = y + _mm(cb[g] * rel, xdt, hp)
        eb = jnp.exp(bh)
        y = y + _mm(cm[:, g * 128:(g + 1) * 128] * eb, jnp.where(hmask, s_prev, 0.0), hp)
        b_last = bh[CHUNK - 1:CHUNK, :]
        kdec = bm[:, g * 128:(g + 1) * 128] * jnp.exp(b_last - bh)
        s_new = s_new + _mm(kdec.T, xdt, hp)
        a2 = jnp.exp(jnp.concatenate([b_last, b_last], axis=-1))
        s_new = s_new + jnp.where(hmask, a2 * s_prev, 0.0)
    s_scr[...] = s_new
    sfin_ref[0] = s_new
    y = (y + xs * dsk_ref[...]) * _silu(z)
    outs = []
    for g in range(D_GROUPS):
        yg = y[:, g * 128:(g + 1) * 128]
        outs.append(yg * lax.rsqrt(jnp.mean(yg * yg, axis=-1, keepdims=True) + EPS))
    yn = jnp.concatenate(outs, axis=-1) * gn_ref[...]
    y_ref[...] = yn[0:rows_in, :]


def _ssd(pm, ps, conv_prev, s0, conv_w, conv_b, a_log, dt_bias, d_skip, norm_g, bsz, t_valid, rows_in, hp):
    nt = pm.shape[0] // (bsz * rows_in)
    fixed = lambda b, i: (0, 0)
    rows = lambda b, i: (b * nt + i, 0)
    rep128 = lambda v: jnp.repeat(v, 128).reshape(1, 512)
    s0l = jnp.transpose(s0, (0, 2, 1, 3)).reshape(bsz, D_STATE, 256)
    y, sfin = pl.pallas_call(
        functools.partial(_ssd_kernel, rows_in=rows_in, t_valid=t_valid, hp=hp),
        grid=(bsz, nt),
        in_specs=[
            pl.BlockSpec((rows_in, 1024), lambda b, i: (b * nt + i, 3)),
            pl.BlockSpec((rows_in, LANES), rows),
            pl.BlockSpec((1, 3, D_CONV_CH), lambda b, i: (b, 0, 0)),
            pl.BlockSpec((1, D_STATE, 256), lambda b, i: (b, 0, 0)),
            pl.BlockSpec((CONV_WIDTH, D_CONV_CH), fixed),
            pl.BlockSpec((1, D_CONV_CH), fixed),
            pl.BlockSpec((1, 512), fixed),
            pl.BlockSpec((1, 512), fixed),
            pl.BlockSpec((1, 256), fixed),
            pl.BlockSpec((1, 256), fixed),
            pl.BlockSpec((LANES, 512), fixed),
            pl.BlockSpec((CHUNK, CHUNK), fixed),
        ],
        out_specs=[
            pl.BlockSpec((rows_in, 256), rows),
            pl.BlockSpec((1, D_STATE, 256), lambda b, i: (b, 0, 0)),
        ],
        out_shape=[
            jax.ShapeDtypeStruct((pm.shape[0], 256), F32),
            jax.ShapeDtypeStruct((bsz, D_STATE, 256), F32),
        ],
        scratch_shapes=[pltpu.VMEM((ROW0 + CHUNK, D_CONV_CH), F32), pltpu.VMEM((D_STATE, 256), F32)],
        compiler_params=pltpu.CompilerParams(
            dimension_semantics=("parallel", "arbitrary"), vmem_limit_bytes=VMEM_LIMIT),
        name="ssd",
    )(pm, ps, conv_prev, s0l, conv_w, conv_b.reshape(1, -1), rep128(dt_bias), rep128(a_log),
      jnp.repeat(d_skip, HEAD_DIM).reshape(1, 256), norm_g.reshape(1, 256),
      _head_expand(SSD_DT_LANE, 128), jnp.asarray(_TRI, BF16))
    return y, jnp.transpose(sfin.reshape(bsz, D_STATE, N_HEADS, HEAD_DIM), (0, 2, 1, 3))


SUB = CHUNK_A
_SUB_ID = np.arange(CHUNK) // SUB
_SAME_SUB = (_SUB_ID[:, None] == _SUB_ID[None, :]).astype(np.float32)
_BLK256 = (np.arange(256)[:, None] // HEAD_DIM == np.arange(256)[None, :] // HEAD_DIM).astype(np.float32)


def _mm_tn(a, b, hp):
    dn = (((0,), (0,)), ((), ()))
    if hp:
        a_hi, a_lo = _split_bf16(a)
        b_hi, b_lo = _split_bf16(b)
        a3 = jnp.concatenate([a_hi, a_lo, a_hi], axis=0)
        b3 = jnp.concatenate([b_hi, b_hi, b_lo], axis=0)
        return lax.dot_general(a3, b3, dn, preferred_element_type=F32)
    return lax.dot_general(a.astype(BF16), b.astype(BF16), dn, preferred_element_type=F32)


def _group_sum(x, ones_blk, hp):
    if hp:
        return _mm_exact_r(x, ones_blk)
    return _dot(x.astype(BF16), ones_blk)


def _hgrn_kernel(pa_ref, s0_ref, c1_ref, c2_ref, oml_ref, gn_ref, blk_ref, t16_ref, l16_ref,
                 o_ref, sfin_ref, kbuf, vbuf, lbuf, st_scr, *, rows_in, t_valid, hp):
    i = pl.program_id(1)

    @pl.when(i == 0)
    def _():
        st_scr[...] = s0_ref[0]
        zeros = jnp.zeros((SUB, 256), F32)
        kbuf[0:SUB, :] = zeros
        vbuf[0:SUB, :] = zeros
        lbuf[0:SUB, :] = zeros

    q = _pad_rows(pa_ref[:, 0:256], rows_in) * (HEAD_DIM ** -0.5)
    zf = _pad_rows(pa_ref[:, 256:512], rows_in)
    v = _pad_rows(pa_ref[:, 512:768], rows_in)
    g = _pad_rows(pa_ref[:, 768:1024], rows_in)
    la = c1_ref[...]
    lb = c2_ref[...] + _log_sigmoid(zf)
    lf = jnp.maximum(la, lb) + jnp.log1p(jnp.exp(-jnp.abs(la - lb)))
    k = oml_ref[...] * jax.nn.sigmoid(-zf)
    tpos = i * CHUNK + lax.broadcasted_iota(jnp.int32, (CHUNK, 1), 0)
    valid = tpos < t_valid
    lf = jnp.where(valid, lf, 0.0)
    k = jnp.where(valid, k, 0.0)
    v = jnp.where(valid, v, 0.0)
    kbuf[SUB:SUB + CHUNK, :] = k
    vbuf[SUB:SUB + CHUNK, :] = v
    lbuf[SUB:SUB + CHUNK, :] = lf

    blk = blk_ref[...]
    sub = lax.broadcasted_iota(jnp.int32, (CHUNK, 1), 0) % SUB
    o = _group_sum(q * k, blk, hp) * v
    bd = jnp.zeros((CHUNK, 256), F32)
    for d in range(1, SUB):
        bd = bd + lbuf[SUB - d + 1:SUB - d + 1 + CHUNK, :]
        m = sub >= d
        ks = kbuf[SUB - d:SUB - d + CHUNK, :]
        tmp = jnp.where(m, q * ks * jnp.exp(jnp.where(m, bd, 0.0)), 0.0)
        o = o + _group_sum(tmp, blk, hp) * vbuf[SUB - d:SUB - d + CHUNK, :]

    b = _mm_exact(t16_ref[...], lf)
    bl = _mm_exact(l16_ref[...], lf)
    qe = q * jnp.exp(b)
    kd = k * jnp.exp(bl - b)
    st = st_scr[...]
    bdmask = blk > 0
    o_rows = []
    for n in range(CHUNK // SUB):
        r = slice(n * SUB, (n + 1) * SUB)
        o_rows.append(_mm_nt(qe[r], st, hp))
        ds = _mm_tn(v[r], kd[r], hp)
        st = st * jnp.exp(bl[n * SUB:n * SUB + 1, :]) + jnp.where(bdmask, ds, 0.0)
    st_scr[...] = st
    sfin_ref[0] = st
    o = o + jnp.concatenate(o_rows, axis=0)
    o = o * lax.rsqrt(_group_mean(o * o, blk, HEAD_DIM) + EPS) * gn_ref[...] * _silu(g)
    o_ref[...] = o[0:rows_in, :]


def _hgrn(pm, lb, norm_g, s0, bsz, t_valid, rows_in, hp):
    nt = pm.shape[0] // (bsz * rows_in)
    fixed = lambda b, i: (0, 0)
    rows = lambda b, i: (b * nt + i, 0)
    lb = jnp.clip(lb, 0.0, LB_CEIL)
    c1 = jnp.log(jnp.maximum(lb, LB_FLOOR)).reshape(1, 256)
    c2 = jnp.log1p(-lb).reshape(1, 256)
    oml = (1.0 - lb).reshape(1, 256)
    eye = jnp.eye(N_HEADS, dtype=bool)[None, :, None, :, None]
    st0 = jnp.where(eye, jnp.swapaxes(s0, 2, 3)[:, :, :, None, :], 0.0).reshape(bsz, 256, 256)
    o, sfin = pl.pallas_call(
        functools.partial(_hgrn_kernel, rows_in=rows_in, t_valid=t_valid, hp=hp),
        grid=(bsz, nt),
        in_specs=[
            pl.BlockSpec((rows_in, 1024), lambda b, i: (b * nt + i, 0)),
            pl.BlockSpec((1, 256, 256), lambda b, i: (b, 0, 0)),
            pl.BlockSpec((1, 256), fixed),
            pl.BlockSpec((1, 256), fixed),
            pl.BlockSpec((1, 256), fixed),
            pl.BlockSpec((1, 256), fixed),
            pl.BlockSpec((256, 256), fixed),
            pl.BlockSpec((CHUNK, CHUNK), fixed),
            pl.BlockSpec((CHUNK, CHUNK), fixed),
        ],
        out_specs=[
            pl.BlockSpec((rows_in, 256), rows),
            pl.BlockSpec((1, 256, 256), lambda b, i: (b, 0, 0)),
        ],
        out_shape=[
            jax.ShapeDtypeStruct((pm.shape[0], 256), F32),
            jax.ShapeDtypeStruct((bsz, 256, 256), F32),
        ],
        scratch_shapes=[pltpu.VMEM((SUB + CHUNK, 256), F32)] * 3 + [pltpu.VMEM((256, 256), F32)],
        compiler_params=pltpu.CompilerParams(
            dimension_semantics=("parallel", "arbitrary"), vmem_limit_bytes=VMEM_LIMIT),
        name="hgrn",
    )(pm, st0, c1, c2, oml, norm_g.reshape(1, 256), jnp.asarray(_BLK256, BF16),
      jnp.asarray(_TRI * _SAME_SUB, BF16), jnp.asarray(_SAME_SUB, BF16))
    sf = sfin.reshape(bsz, N_HEADS, HEAD_DIM, N_HEADS, HEAD_DIM)
    sf = jnp.stack([sf[:, h, :, h, :] for h in range(N_HEADS)], axis=1)
    return o, jnp.swapaxes(sf, 2, 3)


def _head_rmsnorm(x, g):
    y = x * lax.rsqrt(jnp.mean(x * x, axis=-1, keepdims=True) + EPS)
    return y * g.reshape((-1, x.shape[-1]))


def _l2norm(x):
    return x * lax.rsqrt(jnp.sum(x * x, axis=-1, keepdims=True) + EPS)


def _masked_exp(logit, mask):
    return jnp.where(mask, jnp.exp(jnp.where(mask, logit, 0.0)), 0.0)


def _causal_conv(x, w, prev):
    t = x.shape[1]
    xp = jnp.concatenate([prev.astype(x.dtype), x], axis=1)
    y = xp[:, 0:t] * w[0]
    for j in range(1, w.shape[0]):
        y = y + xp[:, j:j + t] * w[j]
    return y, xp[:, t:]


def _pad_time(t, pad):
    return jnp.pad(t, [(0, 0), (0, pad)] + [(0, 0)] * (t.ndim - 2))


def _to_chunks(t, chunk):
    b, tt, h = t.shape[:3]
    t = t.reshape((b, tt // chunk, chunk, h) + t.shape[3:])
    return jnp.moveaxis(t, 3, 1)


def _from_chunks(t):
    b, h, n, c = t.shape[:4]
    t = jnp.moveaxis(t, 1, 3)
    return t.reshape((b, n * c, h) + t.shape[4:])


def _chunk_inputs(arrs, chunk):
    t = arrs[0].shape[1]
    pad = (-t) % chunk
    return [_to_chunks(_pad_time(a.astype(F32), pad), chunk) for a in arrs]


def _chunked_gla(q, k, v, log_f, s0, chunk):
    t = q.shape[1]
    q, k, v, log_f = _chunk_inputs([q, k, v, log_f], chunk)
    b = jnp.cumsum(log_f, axis=3)
    causal = jnp.tril(jnp.ones((chunk, chunk), bool))[:, :, None]
    rel = _masked_exp(b[..., :, None, :] - b[..., None, :, :], causal)
    att = jnp.einsum('bhntk,bhnsk,bhntsk->bhnts', q, k, rel)
    o_intra = jnp.einsum('bhnts,bhnsv->bhntv', att, v)
    b_last = b[..., -1:, :]
    ds = jnp.einsum('bhnsk,bhnsv->nbhkv', k * jnp.exp(b_last - b), v)
    a_chunk = jnp.moveaxis(jnp.exp(b_last[..., 0, :]), 2, 0)

    def step(s, inp):
        a, d = inp
        return a[..., None] * s + d, s

    s_final, s_prev = lax.scan(step, s0.astype(F32), (a_chunk, ds))
    o_inter = jnp.einsum('bhntk,nbhkv->bhntv', q * jnp.exp(b), s_prev)
    return _from_chunks(o_intra + o_inter)[:, :t], s_final


def _chunked_ssd(q, k, v, log_a, s0, chunk):
    t = q.shape[1]
    q, k, v, la = _chunk_inputs([q, k, v, log_a], chunk)
    b = jnp.cumsum(la, axis=-1)
    causal = jnp.tril(jnp.ones((chunk, chunk), bool))
    rel = _masked_exp(b[..., :, None] - b[..., None, :], causal)
    att = jnp.einsum('bhntk,bhnsk->bhnts', q, k) * rel
    o_intra = jnp.einsum('bhnts,bhnsv->bhntv', att, v)
    b_last = b[..., -1:]
    ds = jnp.einsum('bhnsk,bhnsv->nbhkv', k * jnp.exp(b_last - b)[..., None], v)
    a_chunk = jnp.moveaxis(jnp.exp(b_last[..., 0]), 2, 0)

    def step(s, inp):
        a, d = inp
        return a[..., None, None] * s + d, s

    s_final, s_prev = lax.scan(step, s0.astype(F32), (a_chunk, ds))
    o_inter = jnp.einsum('bhntk,nbhkv->bhntv', q * jnp.exp(b)[..., None], s_prev)
    return _from_chunks(o_intra + o_inter)[:, :t], s_final


def _chunked_gated_delta(q, k, v, beta, log_a, s0, chunk):
    t = q.shape[1]
    vd = v.shape[-1]
    q, k, v, beta, la = _chunk_inputs([q, k, v, beta, log_a], chunk)
    b = jnp.cumsum(la, axis=-1)
    causal = jnp.tril(jnp.ones((chunk, chunk), bool))
    strict = jnp.tril(jnp.ones((chunk, chunk), bool), k=-1)
    decay = _masked_exp(b[..., :, None] - b[..., None, :], causal)
    kb = k * beta[..., None]
    m = jnp.where(strict, jnp.einsum('bhntk,bhnsk->bhnts', kb, k) * decay, 0.0)
    rhs = jnp.concatenate([v * beta[..., None], kb * jnp.exp(b)[..., None]], axis=-1)
    sol = lax.linalg.triangular_solve(m + jnp.eye(chunk, dtype=F32), rhs, left_side=True,
                                      lower=True, unit_diagonal=True)
    u, w = sol[..., :vd], sol[..., vd:]
    att = jnp.einsum('bhntk,bhnsk->bhnts', q, k) * decay
    q_dec = q * jnp.exp(b)[..., None]
    k_dec = k * jnp.exp(b[..., -1:] - b)[..., None]
    a_chunk = jnp.exp(b[..., -1])
    xs = tuple(jnp.moveaxis(a, 2, 0) for a in (u, w, att, q_dec, k_dec, a_chunk))

    def step(s, inp):
        u_c, w_c, att_c, qd_c, kd_c, a_c = inp
        v_new = u_c - jnp.einsum('bhtk,bhkv->bhtv', w_c, s)
        o = jnp.einsum('bhtk,bhkv->bhtv', qd_c, s) + jnp.einsum('bhts,bhsv->bhtv', att_c, v_new)
        s = a_c[..., None, None] * s + jnp.einsum('bhsk,bhsv->bhkv', kd_c, v_new)
        return s, o

    s_final, o = lax.scan(step, s0.astype(F32), xs)
    return _from_chunks(jnp.moveaxis(o, 0, 2))[:, :t], s_final


def _hgrn2_mixer(pa, lb, norm_g, s0):
    bsz, t, _ = pa.shape
    w = GROUP_WIDTH
    q, zf, inp, g = (pa[..., j * w:(j + 1) * w] for j in range(4))
    lb = jnp.clip(lb, 0.0, LB_CEIL)
    log_f = jnp.logaddexp(jnp.log(jnp.maximum(lb, LB_FLOOR)), jnp.log1p(-lb) + jax.nn.log_sigmoid(zf))
    k = (1.0 - lb) * jax.nn.sigmoid(-zf)
    heads = lambda a: a.reshape(bsz, t, N_HEADS, -1)
    o, s = _chunked_gla(heads(q) * HEAD_DIM ** -0.5, heads(k), heads(inp), heads(log_f), s0, CHUNK_A)
    o = _head_rmsnorm(o, norm_g) * jax.nn.silu(heads(g))
    return o.reshape(bsz, t, w), s


def _gdn_mixer(pb, beta_pre, dt_pre, conv_w, conv_prev, a_log, dt_bias, norm_g, s0):
    bsz, t, _ = pb.shape
    w = GROUP_WIDTH
    qkv, conv_new = _causal_conv(pb[..., :3 * w], conv_w, conv_prev)
    qkv = jax.nn.silu(qkv)
    heads = lambda a: a.reshape(bsz, t, N_HEADS, -1)
    q = _l2norm(heads(qkv[..., :w])) * HEAD_DIM ** -0.5
    k = _l2norm(heads(qkv[..., w:2 * w]))
    v = heads(qkv[..., 2 * w:3 * w])
    gate = heads(pb[..., 3 * w:4 * w])
    beta = jax.nn.sigmoid(beta_pre)
    log_a = -jnp.exp(a_log) * jax.nn.softplus(dt_pre + dt_bias)
    o, s = _chunked_gated_delta(q, k, v, beta, log_a, s0, CHUNK_B)
    o = _head_rmsnorm(o, norm_g) * jax.nn.silu(gate)
    return o.reshape(bsz, t, w), s, conv_new


def _fox_attention_prompt(q, k, v, log_f):
    b, t, h, d = q.shape
    n_blk = -(-t // Q_BLOCK)
    c = jnp.moveaxis(jnp.cumsum(log_f, axis=1), 2, 1)
    key_pos = jnp.arange(t)

    def block(i):
        start = i * Q_BLOCK
        q_i = lax.dynamic_slice_in_dim(q, start, Q_BLOCK, axis=1)
        c_i = lax.dynamic_slice_in_dim(c, start, Q_BLOCK, axis=2)
        s = jnp.einsum('bqhd,bkhd->bhqk', q_i, k) * (d ** -0.5) + c_i[..., :, None] - c[:, :, None, :]
        allowed = (start + jnp.arange(Q_BLOCK))[:, None] >= key_pos[None, :]
        p = jax.nn.softmax(jnp.where(allowed, s, NEG_BIG), axis=-1)
        return jnp.einsum('bhqk,bkhd->bqhd', p, v)

    o = lax.map(block, jnp.arange(n_blk))
    return jnp.moveaxis(o, 0, 1).reshape(b, n_blk * Q_BLOCK, h, d)[:, :t]


def _fox_attention_sample(q, k, v, log_f, k_past, v_past, logf_past):
    n_past = k_past.shape[1]
    s_new = q.shape[1]
    d = q.shape[-1]
    k_all = jnp.concatenate([k_past, k], axis=1)
    v_all = jnp.concatenate([v_past, v], axis=1)
    c = jnp.cumsum(jnp.concatenate([logf_past, log_f], axis=1), axis=1)
    c = jnp.moveaxis(c, 2, 1)
    s = jnp.einsum('bqhd,bkhd->bhqk', q, k_all) * (d ** -0.5) + c[:, :, n_past:, None] - c[:, :, None, :]
    allowed = (n_past + jnp.arange(s_new))[:, None] >= jnp.arange(n_past + s_new)[None, :]
    p = jax.nn.softmax(jnp.where(allowed, s, NEG_BIG), axis=-1)
    return jnp.einsum('bhqk,bkhd->bqhd', p, v_all)


def _fox_mixer(pc, lf_pre, b_f, q_norm, k_norm, out_norm, past):
    bsz, t, _ = pc.shape
    w = GROUP_WIDTH
    heads = lambda a: a.reshape(bsz, t, N_HEADS, HEAD_DIM)
    q = _head_rmsnorm(heads(pc[..., :w]), q_norm)
    k = _head_rmsnorm(heads(pc[..., w:2 * w]), k_norm)
    v = heads(pc[..., 2 * w:3 * w])
    g = heads(pc[..., 3 * w:4 * w])
    log_f = jax.nn.log_sigmoid(lf_pre + b_f)
    if past is None:
        o = _fox_attention_prompt(q, k, v, log_f)
    else:
        o = _fox_attention_sample(q, k, v, log_f, past[0], past[1], past[2])
    o = _head_rmsnorm(o, out_norm) * jax.nn.sigmoid(g)
    return o.reshape(bsz, t, w), k, v, log_f


def _ssd_mixer(pd, dt_pre, conv_w, conv_b, conv_prev, a_log, dt_bias, d_skip, norm_g, s0):
    bsz, t, _ = pd.shape
    w = GROUP_WIDTH
    gn = D_GROUPS * D_STATE
    rep = N_HEADS // D_GROUPS
    z = pd[..., :w].reshape(bsz, t, N_HEADS, HEAD_DIM)
    xbc, conv_new = _causal_conv(pd[..., w:w + D_CONV_CH], conv_w, conv_prev)
    xbc = jax.nn.silu(xbc + conv_b)
    xs = xbc[..., :w].reshape(bsz, t, N_HEADS, HEAD_DIM)
    bm = jnp.repeat(xbc[..., w:w + gn].reshape(bsz, t, D_GROUPS, D_STATE), rep, axis=2)
    cm = jnp.repeat(xbc[..., w + gn:].reshape(bsz, t, D_GROUPS, D_STATE), rep, axis=2)
    dt = jax.nn.softplus(dt_pre + dt_bias)
    log_a = -jnp.exp(a_log) * dt
    y, s = _chunked_ssd(cm, bm, xs * dt[..., None], log_a, s0, CHUNK_D)
    y = (y + xs * d_skip[:, None]) * jax.nn.silu(z)
    y = _head_rmsnorm(y.reshape(bsz, t, D_GROUPS, -1), norm_g)
    return y.reshape(bsz, t, w), s, conv_new


def _gather_pages(pool, page_table):
    g = pool[page_table]
    return g.reshape((g.shape[0], g.shape[1] * g.shape[2]) + g.shape[3:])


def _prep_weights(prm):
    w_in = prm['w_in']
    wmh, wml = _split_weights(w_in[:, :, _MAIN_COLS])
    ws = jnp.pad(w_in[:, :, _SMALL_COLS], ((0, 0), (0, 0), (0, LANES - len(_SMALL_COLS))))
    wsh, wsl = _split_weights(ws)
    wr = jnp.concatenate([prm['moe_w_expert'], prm['moe_w_group']], axis=-1)
    n_r = N_EXPERTS + N_EXPERT_GROUPS
    wr = jnp.pad(wr, ((0, 0), (0, 0), (0, LANES - n_r)))
    wrh, wrl = _split_weights(wr)
    br =jnp.pad(jnp.concatenate([prm['moe_b_expert'], prm['moe_b_group']], axis=-1),
                 ((0, 0), (0, LANES - n_r)))[:, None, :]
    wts = dict(wmh=wmh, wml=wml, wsh=wsh, wsl=wsl, wrh=wrh, wrl=wrl, br=br)
    for name, key in (('wo', 'w_out'), ('w1', 'moe_w1'), ('w3', 'moe_w3'), ('w2', 'moe_w2'),
                      ('wg', 'ple_w_gate'), ('wp', 'ple_w_proj')):
        wts[name + 'h'], wts[name + 'l'] = _split_weights(prm[key])
    return wts


def _trunk(x, p, init_state, fox_cache, lb_all, prm, wts, tm, hp_layers):
    s_hgrn0, s_gdn0, c_gdn0, s_ssd0, c_ssd0 = init_state
    bsz, t, _ = x.shape
    n = bsz * t
    h = x.reshape(n, D_MODEL)
    outs = [[] for _ in range(8)]
    row = lambda a: a.reshape(1, -1)
    for l in range(DEPTH):
        hp_mix, hp_ffn = hp_layers[l]
        pm, ps = _inproj(h, row(prm['g_mix'][l]), wts['wmh'][l], wts['wml'][l], wts['wsh'][l],
                         wts['wsl'][l], tm, hp_mix)
        if fox_cache is None:
            o_c, k_c, lf_c = _fox_prompt(pm, ps, prm['fox_b_f'][l], prm['fox_q_norm'][l], prm['fox_k_norm'][l],
                                         prm['fox_out_norm'][l], bsz, t, hp_mix)
            o_c = o_c.reshape(bsz, t, GROUP_WIDTH)
            k_c = k_c.reshape(bsz, t, N_HEADS, HEAD_DIM)
            lf_c = lf_c.reshape(bsz, t, N_HEADS)
            v_c = pm[:, 2560:2816].reshape(bsz, t, N_HEADS, HEAD_DIM)
        rows_in = CHUNK if t % CHUNK == 0 else 8
        if rows_in == CHUNK:
            pmr, psr = pm, ps
        else:
            padr = lambda a: jnp.pad(a.reshape(bsz, t, -1), ((0, 0), (0, rows_in - t), (0, 0))).reshape(
                bsz * rows_in, -1)
            pmr, psr = padr(pm), padr(ps)
        unpad = lambda a: a.reshape(bsz, -1, GROUP_WIDTH)[:, :t]
        o_a, s_a = _hgrn(pmr, lb_all[l], prm['hgrn_norm'][l], s_hgrn0[l], bsz, t, rows_in, hp_mix)
        o_d, s_d = _ssd(pmr, psr, c_ssd0[l], s_ssd0[l], prm['ssd_conv_w'][l], prm['ssd_conv_b'][l],
                        prm['ssd_a_log'][l], prm['ssd_dt_bias'][l], prm['ssd_d'][l], prm['ssd_norm'][l],
                        bsz, t, rows_in, hp_mix)
        o_a, o_d = unpad(o_a), unpad(o_d)
        pm = pm.reshape(bsz, t, N_MAIN)
        ps = ps.reshape(bsz, t, LANES)
        c_d = jnp.concatenate([c_ssd0[l], pm[:, max(0, t - 3):, 3328:4096]], axis=1)[:, -(CONV_WIDTH - 1):]
        with jax.default_matmul_precision('highest' if hp_mix else 'default'):
            o_b, s_b, c_b = _gdn_mixer(pm[..., 1024:2048], ps[..., 0:4], ps[..., 4:8], prm['gdn_conv_w'][l],
                                       c_gdn0[l], prm['gdn_a_log'][l], prm['gdn_dt_bias'][l],
                                       prm['gdn_norm'][l], s_gdn0[l])
            if fox_cache is not None:
                cache_k, cache_v, cache_logf, page_table = fox_cache
                past = (_gather_pages(cache_k[l], page_table), _gather_pages(cache_v[l], page_table),
                        _gather_pages(cache_logf[l], page_table))
                o_c, k_c, v_c, lf_c = _fox_mixer(pm[..., 2048:3072], ps[..., 8:12], prm['fox_b_f'][l],
                                                 prm['fox_q_norm'][l], prm['fox_k_norm'][l],
                                                 prm['fox_out_norm'][l], past)
        mix =jnp.concatenate([o_a, o_b, o_c, o_d], axis=-1).reshape(n, D_MODEL)
        h2, u2, cw = _outproj(h, mix, wts['woh'][l], wts['wol'][l], row(prm['g_ffn'][l]), wts['wrh'][l],
                              wts['wrl'][l], wts['br'][l], tm, hp_mix)
        y = _moe(u2, cw, wts['w1h'][l], wts['w1l'][l], wts['w3h'][l], wts['w3l'][l], wts['w2h'][l],
                 wts['w2l'][l], tm, hp_ffn)
        h = _ple(h2, y, p[l].reshape(n, PLE_DIM), row(prm['g_ple'][l]), wts['wgh'][l], wts['wgl'][l],
                 wts['wph'][l], wts['wpl'][l], row(prm['g_final']), tm, final=(l == DEPTH - 1), hp=hp_ffn)
        for acc, val in zip(outs, (k_c, v_c, lf_c, s_a, s_b, c_b, s_d, c_d)):
            acc.append(val)
    return (h.reshape(bsz, t, D_MODEL),) + tuple(jnp.stack(acc) for acc in outs)


def _hgrn_lower_bounds(lb_param):
    sm = jax.nn.softmax(lb_param, axis=0)
    return jnp.concatenate([jnp.zeros_like(sm[:1]), jnp.cumsum(sm[1:], axis=0)], axis=0)


def kernel(x_prompt, x_sample, cache_fox_k, cache_fox_v, cache_fox_logf, state_hgrn, state_gdn,
           state_gdn_conv, state_ssd, state_ssd_conv, page_table, p_prompt, p_sample, w_in, w_out,
           g_mix, g_ffn, g_ple, g_final, hgrn_lb, hgrn_norm, gdn_conv_w, gdn_a_log, gdn_dt_bias,
           gdn_norm, fox_b_f, fox_q_norm, fox_k_norm, fox_out_norm, ssd_conv_w, ssd_conv_b, ssd_a_log,
           ssd_dt_bias, ssd_d, ssd_norm, moe_w_group, moe_b_group, moe_w_expert, moe_b_expert, moe_w1,
           moe_w3, moe_w2, ple_w_gate, ple_w_proj):
    prm = dict(w_in=w_in, w_out=w_out, g_mix=g_mix, g_ffn=g_ffn, g_ple=g_ple, g_final=g_final,
               hgrn_norm=hgrn_norm, gdn_conv_w=gdn_conv_w, gdn_a_log=gdn_a_log, gdn_dt_bias=gdn_dt_bias,
               gdn_norm=gdn_norm, fox_b_f=fox_b_f, fox_q_norm=fox_q_norm, fox_k_norm=fox_k_norm,
               fox_out_norm=fox_out_norm, ssd_conv_w=ssd_conv_w, ssd_conv_b=ssd_conv_b, ssd_a_log=ssd_a_log,
               ssd_dt_bias=ssd_dt_bias, ssd_d=ssd_d, ssd_norm=ssd_norm, moe_w_group=moe_w_group,
               moe_b_group=moe_b_group, moe_w_expert=moe_w_expert, moe_b_expert=moe_b_expert,
               moe_w1=moe_w1, moe_w3=moe_w3, moe_w2=moe_w2, ple_w_gate=ple_w_gate, ple_w_proj=ple_w_proj)
    wts = _prep_weights(prm)
    lb_all = _hgrn_lower_bounds(hgrn_lb)
    bp = x_prompt.shape[0]
    zero_state = (jnp.zeros((DEPTH, bp, N_HEADS, HEAD_DIM, HEAD_DIM), F32),
                  jnp.zeros((DEPTH, bp, N_HEADS, HEAD_DIM, HEAD_DIM), F32),
                  jnp.zeros((DEPTH, bp, CONV_WIDTH - 1, B_CONV_CH), F32),
                  jnp.zeros((DEPTH, bp, N_HEADS, D_STATE, HEAD_DIM), F32),
                  jnp.zeros((DEPTH, bp, CONV_WIDTH - 1, D_CONV_CH), F32))
    pr = _trunk(x_prompt, p_prompt, zero_state, None, lb_all, prm, wts, tm=512,
                hp_layers=((True, False), (False, False)))
    sm = _trunk(x_sample, p_sample, (state_hgrn, state_gdn, state_gdn_conv, state_ssd, state_ssd_conv),
                (cache_fox_k, cache_fox_v, cache_fox_logf, page_table), lb_all, prm, wts, tm=32,
                hp_layers=((True, True), (True, True)))
    return (pr[0], sm[0]) + tuple(pr[1:]) + tuple(sm[1:])
```

```python
import functools
import math

import jax
import jax.numpy as jnp
import numpy as np
from jax import lax
from jax.experimental import pallas as pl
from jax.experimental.pallas import tpu as pltpu

F32 = jnp.float32
BF16 = jnp.bfloat16

D_MODEL = 1024
DEPTH = 2
PAGE_SIZE = 128
EPS = 1e-6
NEG_BIG = -1e30
LB_FLOOR = 1e-30
LB_CEIL = 1.0 - 1e-6
PLE_DIM = 256
GROUP_WIDTH = 256
HEAD_DIM = 64
N_HEADS = 4
D_GROUPS = 2
D_STATE = 128
CONV_WIDTH = 4
B_CONV_CH = 3 * GROUP_WIDTH
D_CONV_CH = GROUP_WIDTH + 2 * D_GROUPS * D_STATE
SIZE_A = 4 * GROUP_WIDTH
SIZE_B = 4 * GROUP_WIDTH + 2 * N_HEADS
SIZE_C = 4 * GROUP_WIDTH + N_HEADS
SIZE_D = GROUP_WIDTH + D_CONV_CH + N_HEADS
OFF_B = SIZE_A
OFF_C = OFF_B + SIZE_B
OFF_D = OFF_C + SIZE_C
N_IN = OFF_D + SIZE_D
CHUNK_A = 16
CHUNK_B = 64
CHUNK_D = 128
Q_BLOCK = 128
N_EXPERT_GROUPS = 4
EXPERTS_PER_GROUP = 4
N_EXPERTS = 16
D_EXPERT = 512

LANES = 128
N_MAIN = 4096
VMEM_LIMIT = 48 * 1024 * 1024

_MAIN_COLS = np.concatenate([
    np.arange(0, SIZE_A),
    np.arange(OFF_B, OFF_B + 4 * GROUP_WIDTH),
    np.arange(OFF_C, OFF_C + 4 * GROUP_WIDTH),
    np.arange(OFF_D, OFF_D + 4 * GROUP_WIDTH),
])
_SMALL_COLS = np.concatenate([
    np.arange(OFF_B + 4 * GROUP_WIDTH, OFF_B + SIZE_B),
    np.arange(OFF_C + 4 * GROUP_WIDTH, OFF_C + SIZE_C),
    np.arange(OFF_D + 4 * GROUP_WIDTH, OFF_D + SIZE_D),
])


def _split_bf16(x):
    hi = x.astype(BF16)
    lo = (x - hi.astype(F32)).astype(BF16)
    return hi, lo


def _dot(a, b):
    return jnp.dot(a, b, preferred_element_type=F32)


def _dot3(a_hi, a_lo, b_hi, b_lo):
    return _dot(a_hi, b_hi) + _dot(a_lo, b_hi) + _dot(a_hi, b_lo)


def _rms(x, g):
    return x * lax.rsqrt(jnp.mean(x * x, axis=-1, keepdims=True) + EPS) * g


def _wmode(w):
    if len(w) == 2:
        return 'x3'
    return 'f32' if w[0].dtype == F32 else 'x1'


def _wcount(mode):
    return 2 if mode == 'x3' else 1


def _wload(refs, mode, idx=None):
    get = (lambda r: r[...]) if idx is None else (lambda r: r[idx])
    if mode == 'x1':
        return get(refs[0]), None
    if mode == 'x3':
        return get(refs[0]), get(refs[1])
    return _split_bf16(get(refs[0]))


def _wdot(a, refs, mode, idx=None):
    w_hi, w_lo = _wload(refs, mode, idx)
    if w_lo is None:
        return _dot(a.astype(BF16), w_hi)
    a_hi, a_lo = _split_bf16(a)
    return _dot3(a_hi, a_lo, w_hi, w_lo)


def _split_kernel(w_ref, hi_ref, lo_ref):
    hi, lo = _split_bf16(w_ref[...])
    hi_ref[...] = hi
    lo_ref[...] = lo


def _split_weight(w):
    rows, cols = w.shape
    spec = pl.BlockSpec((256, cols), lambda i: (i, 0))
    return pl.pallas_call(
        _split_kernel,
        grid=(rows // 256,),
        in_specs=[spec],
        out_specs=[spec, spec],
        out_shape=[jax.ShapeDtypeStruct((rows, cols), BF16)] * 2,
        compiler_params=pltpu.CompilerParams(dimension_semantics=("parallel",)),
        name="split_weight",
    )(w)


def _inproj_kernel(x_ref, g_ref, *refs, mode):
    nw = _wcount(mode)
    wm_refs = refs[:nw]
    wsh_ref, wsl_ref, om_ref, os_ref, uh_ref, ul_ref = refs[nw:]

    @pl.when(pl.program_id(1) == 0)
    def _():
        u = _rms(x_ref[...], g_ref[...])
        uh, ul = _split_bf16(u)
        uh_ref[...] = uh
        ul_ref[...] = ul
        os_ref[...] = _dot3(uh, ul, wsh_ref[...], wsl_ref[...])

    w_hi, w_lo = _wload(wm_refs, mode)
    if w_lo is None:
        om_ref[...] = _dot(uh_ref[...], w_hi)
    else:
        om_ref[...] = _dot3(uh_ref[...], ul_ref[...], w_hi, w_lo)


def _inproj(x, g, wm, wsh, wsl, tm):
    n = x.shape[0]
    tn = 1024
    return pl.pallas_call(
        functools.partial(_inproj_kernel, mode=_wmode(wm)),
        grid=(n // tm, N_MAIN // tn),
        in_specs=[
            pl.BlockSpec((tm, D_MODEL), lambda i, j: (i, 0)),
            pl.BlockSpec((1, D_MODEL), lambda i, j: (0, 0)),
        ] + [pl.BlockSpec((D_MODEL, tn), lambda i, j: (0, j))] * len(wm) + [
            pl.BlockSpec((D_MODEL, LANES), lambda i, j: (0, 0)),
            pl.BlockSpec((D_MODEL, LANES), lambda i, j: (0, 0)),
        ],
        out_specs=[
            pl.BlockSpec((tm, tn), lambda i, j: (i, j)),
            pl.BlockSpec((tm, LANES), lambda i, j: (i, 0)),
        ],
        out_shape=[
            jax.ShapeDtypeStruct((n, N_MAIN), F32),
            jax.ShapeDtypeStruct((n, LANES), F32),
        ],
        scratch_shapes=[pltpu.VMEM((tm, D_MODEL), BF16), pltpu.VMEM((tm, D_MODEL), BF16)],
        compiler_params=pltpu.CompilerParams(
            dimension_semantics=("parallel", "arbitrary"), vmem_limit_bytes=VMEM_LIMIT),
        name="inproj",
    )(x, g, *wm, wsh, wsl)


def _route(logits):
    lane = lax.broadcasted_iota(jnp.int32, logits.shape, 1)
    gmask = (lane >= N_EXPERTS) & (lane < N_EXPERTS + N_EXPERT_GROUPS)
    gl = jnp.where(gmask, logits, -jnp.inf)
    gmax = jnp.max(gl, axis=-1, keepdims=True)
    gidx = jnp.min(jnp.where(gl == gmax, lane, 4 * LANES), axis=-1, keepdims=True) - N_EXPERTS
    gw = 1.0 / jnp.sum(jnp.where(gmask, jnp.exp(gl - gmax), 0.0), axis=-1, keepdims=True)
    lo = gidx * EXPERTS_PER_GROUP
    emask = (lane >= lo) & (lane < lo + EXPERTS_PER_GROUP)
    el = jnp.where(emask, logits, -jnp.inf)
    m1 = jnp.max(el, axis=-1, keepdims=True)
    i1 = jnp.min(jnp.where(el == m1, lane, 4 * LANES), axis=-1, keepdims=True)
    el2 = jnp.where(lane == i1, -jnp.inf, el)
    m2 = jnp.max(el2, axis=-1, keepdims=True)
    i2 = jnp.min(jnp.where(el2 == m2, lane, 4 * LANES), axis=-1, keepdims=True)
    e2 = jnp.exp(m2 - m1)
    den = 1.0 + e2
    g1 = gw / den
    g2 = gw * e2 / den
    return jnp.where(lane == i1, g1, jnp.where(lane == i2, g2, 0.0))


def _outproj_kernel(h_ref, mix_ref, *refs, mode):
    nw = _wcount(mode)
    g_ref, wrh_ref, wrl_ref, br_ref, h2_ref, u2_ref, cw_ref = refs[nw:]
    h2 = h_ref[...] + _wdot(mix_ref[...], refs[:nw], mode)
    h2_ref[...] = h2
    u = _rms(h2, g_ref[...])
    u2_ref[...] = u
    uh, ul = _split_bf16(u)
    logits = _dot3(uh, ul, wrh_ref[...], wrl_ref[...]) + br_ref[...]
    cw_ref[...] = _route(logits)


def _outproj(h, mix, wo, g, wrh, wrl, br, tm):
    n = h.shape[0]
    row = lambda i: (i, 0)
    fixed = lambda i: (0, 0)
    return pl.pallas_call(
        functools.partial(_outproj_kernel, mode=_wmode(wo)),
        grid=(n // tm,),
        in_specs=[
            pl.BlockSpec((tm, D_MODEL), row),
            pl.BlockSpec((tm, D_MODEL), row),
        ] + [pl.BlockSpec((D_MODEL, D_MODEL), fixed)] * len(wo) + [
            pl.BlockSpec((1, D_MODEL), fixed),
            pl.BlockSpec((D_MODEL, LANES), fixed),
            pl.BlockSpec((D_MODEL, LANES), fixed),
            pl.BlockSpec((1, LANES), fixed),
        ],
        out_specs=[
            pl.BlockSpec((tm, D_MODEL), row),
            pl.BlockSpec((tm, D_MODEL), row),
            pl.BlockSpec((tm, LANES), row),
        ],
        out_shape=[
            jax.ShapeDtypeStruct((n, D_MODEL), F32),
            jax.ShapeDtypeStruct((n, D_MODEL), F32),
            jax.ShapeDtypeStruct((n, LANES), F32),
        ],
        compiler_params=pltpu.CompilerParams(
            dimension_semantics=("parallel",), vmem_limit_bytes=VMEM_LIMIT),
        name="outproj",
    )(h, mix, *wo, g, wrh, wrl, br)


def _moe_kernel(x_ref, cw_ref, w1_ref, w3_ref, w2_ref, y_ref, xh_ref, xl_ref, *, mode):
    e = pl.program_id(1)

    @pl.when(e == 0)
    def _():
        y_ref[...] = jnp.zeros_like(y_ref)
        xh, xl = _split_bf16(x_ref[...])
        xh_ref[...] = xh
        xl_ref[...] = xl

    cw = cw_ref[...]
    lane = lax.broadcasted_iota(jnp.int32, cw.shape, 1)
    col = jnp.sum(jnp.where(lane == e, cw, 0.0), axis=-1, keepdims=True)
    xh = xh_ref[...]
    w1_hi, w1_lo = _wload((w1_ref,), mode, 0)
    w3_hi, w3_lo = _wload((w3_ref,), mode, 0)
    if w1_lo is None:
        a = _dot(xh, w1_hi)
        b = _dot(xh, w3_hi)
    else:
        xl = xl_ref[...]
        a = _dot3(xh, xl, w1_hi, w1_lo)
        b = _dot3(xh, xl, w3_hi, w3_lo)
    hid = (a * jax.nn.sigmoid(a)) * b
    y_ref[...] += col * _wdot(hid, (w2_ref,), mode, 0)


def _moe(u2, cw, w1, w3, w2, tm):
    n = u2.shape[0]
    mode = 'f32' if w1.dtype == F32 else 'x1'
    return pl.pallas_call(
        functools.partial(_moe_kernel, mode=mode),
        grid=(n // tm, N_EXPERTS),
        in_specs=[
            pl.BlockSpec((tm, D_MODEL), lambda i, e: (i, 0)),
            pl.BlockSpec((tm, LANES), lambda i, e: (i, 0)),
            pl.BlockSpec((1, D_MODEL, D_EXPERT), lambda i, e: (e, 0, 0)),
            pl.BlockSpec((1, D_MODEL, D_EXPERT), lambda i, e: (e, 0, 0)),
            pl.BlockSpec((1, D_EXPERT, D_MODEL), lambda i, e: (e, 0, 0)),
        ],
        out_specs=pl.BlockSpec((tm, D_MODEL), lambda i, e: (i, 0)),
        out_shape=jax.ShapeDtypeStruct((n, D_MODEL), F32),
        scratch_shapes=[pltpu.VMEM((tm, D_MODEL), BF16), pltpu.VMEM((tm, D_MODEL), BF16)],
        compiler_params=pltpu.CompilerParams(
            dimension_semantics=("parallel", "arbitrary"), vmem_limit_bytes=VMEM_LIMIT),
        name="moe",
    )(u2, cw, w1, w3, w2)


def _ple_kernel(h_ref, y_ref, p_ref, g_ref, wg_ref, wp_ref, gf_ref, o_ref, *, final, mode):
    h3 = h_ref[...] + y_ref[...]
    u = _rms(h3, g_ref[...])
    gate = jax.nn.sigmoid(_wdot(u, (wg_ref,), mode))
    h4 = h3 + gate * _wdot(p_ref[...], (wp_ref,), mode)
    if final:
        h4 = _rms(h4, gf_ref[...])
    o_ref[...] = h4


def _ple(h2, y, p, g, wg, wp, gf, tm, final):
    n = h2.shape[0]
    row = lambda i: (i, 0)
    fixed = lambda i: (0, 0)
    mode = 'f32' if wg.dtype == F32 else 'x1'
    return pl.pallas_call(
        functools.partial(_ple_kernel, final=final, mode=mode),
        grid=(n // tm,),
        in_specs=[
            pl.BlockSpec((tm, D_MODEL), row),
            pl.BlockSpec((tm, D_MODEL), row),
            pl.BlockSpec((tm, PLE_DIM), row),
            pl.BlockSpec((1, D_MODEL), fixed),
            pl.BlockSpec((D_MODEL, D_MODEL), fixed),
            pl.BlockSpec((PLE_DIM, D_MODEL), fixed),
            pl.BlockSpec((1, D_MODEL), fixed),
        ],
        out_specs=pl.BlockSpec((tm, D_MODEL), row),
        out_shape=jax.ShapeDtypeStruct((n, D_MODEL), F32),
        compiler_params=pltpu.CompilerParams(
            dimension_semantics=("parallel",), vmem_limit_bytes=VMEM_LIMIT),
        name="ple",
    )(h2, y, p, g, wg, wp, gf)


FOX_LF_LANE = 8
FOX_TB = 256
FOX_TQ = 256


def _split3_bf16(x):
    hi = x.astype(BF16)
    r = x - hi.astype(F32)
    mid = r.astype(BF16)
    lo = (r - mid.astype(F32)).astype(BF16)
    return hi, mid, lo


def _log_sigmoid(x):
    return jnp.minimum(x, 0.0) - jnp.log1p(jnp.exp(-jnp.abs(x)))


def _group_mean(x2, ones_blk, width):
    hi, mid, lo = _split3_bf16(x2)
    return (_dot(hi, ones_blk) + _dot(mid, ones_blk) + _dot(lo, ones_blk)) * (1.0 / width)


def _fox_layout(hp):
    ka = 256 if hp else 128
    a0 = 192 if hp else 64
    nx = 512 + 3 * LANES
    mq = np.zeros((nx, 4 * ka), np.float32)
    mk = np.zeros((nx, 4 * ka), np.float32)
    rq = np.zeros((1, 4 * ka), np.float32)
    rk = np.zeros((1, 4 * ka), np.float32)
    mv = np.zeros((512, 4 * LANES), np.float32)
    for h in range(N_HEADS):
        for d in range(HEAD_DIM):
            src_hi, src_lo = 64 * h + d, 256 + 64 * h + d
            mq[src_hi, h * ka + d] = 1.0
            mk[src_hi, h * ka + d] = 1.0
            if hp:
                mq[src_lo, h * ka + 64 + d] = 1.0
                mq[src_hi, h * ka + 128 + d] = 1.0
                mk[src_hi, h * ka + 64 + d] = 1.0
                mk[src_lo, h * ka + 128 + d] = 1.0
            mv[src_hi, h * LANES + d] = 1.0
            if hp:
                mv[src_lo, h * LANES + 64 + d] = 1.0
        for part in range(3):
            src = 512 + part * LANES + FOX_LF_LANE + h
            mq[src, h * ka + a0 + part] = 1.0
            mk[src, h * ka + a0 + 3 + part] = -1.0
            rq[0, h * ka + a0 + 3 + part] = 1.0
            rk[0, h * ka + a0 + part] = 1.0
    blk = (np.arange(256)[:, None] // HEAD_DIM == np.arange(256)[None, :] // HEAD_DIM).astype(np.float32)
    tri = np.tril(np.ones((FOX_TB, FOX_TB), np.float32))
    as_bf = lambda a: jnp.asarray(a, BF16)
    return dict(ka=ka, mq=as_bf(mq), mk=as_bf(mk), mv=as_bf(mv), rq=jnp.asarray(rq), rk=jnp.asarray(rk),
                blk=as_bf(blk), tri=as_bf(tri))


def _fox_prep_kernel(pc_ref, ps_ref, bf_ref, qn_ref, kn_ref, blk_ref, tri_ref, mq_ref, mk_ref, mv_ref,
                     rq_ref, rk_ref, krow_ref, lf_ref, qa_ref, ka_ref, vv_ref, carry_ref, *, ka):
    @pl.when(pl.program_id(1) == 0)
    def _():
        carry_ref[...] = jnp.zeros_like(carry_ref)

    blk = blk_ref[...]
    q = pc_ref[:, 0:256]
    k = pc_ref[:, 256:512]
    v = pc_ref[:, 512:768]
    qn = q * lax.rsqrt(_group_mean(q * q, blk, HEAD_DIM) + EPS) * qn_ref[...]
    kn = k * lax.rsqrt(_group_mean(k * k, blk, HEAD_DIM) + EPS) * kn_ref[...]
    krow_ref[...] = kn
    lf = _log_sigmoid(ps_ref[...] + bf_ref[...])
    lf_ref[...] = lf
    l_hi, l_mid, l_lo = _split3_bf16(lf)
    tri = tri_ref[...]
    c = _dot(tri, l_hi) + _dot(tri, l_mid) + _dot(tri, l_lo) + carry_ref[...]
    carry_ref[...] = c[FOX_TB - 1:FOX_TB, :]
    c_hi, c_mid, c_lo = _split3_bf16(c)
    q_hi, q_lo = _split_bf16(qn * (HEAD_DIM ** -0.5))
    k_hi, k_lo = _split_bf16(kn)
    xq = jnp.concatenate([q_hi, q_lo, c_hi, c_mid, c_lo], axis=-1)
    xk = jnp.concatenate([k_hi, k_lo, c_hi, c_mid, c_lo], axis=-1)
    qa = (_dot(xq, mq_ref[...]) + rq_ref[...]).astype(BF16)
    kk = (_dot(xk, mk_ref[...]) + rk_ref[...]).astype(BF16)
    v_hi, v_lo = _split_bf16(v)
    vv = _dot(jnp.concatenate([v_hi, v_lo], axis=-1), mv_ref[...]).astype(BF16)
    for h in range(N_HEADS):
        qa_ref[0, h] = qa[:, h * ka:(h + 1) * ka]
        ka_ref[0, h] = kk[:, h * ka:(h + 1) * ka]
        vv_ref[0, h] = vv[:, h * LANES:(h + 1) * LANES]


def _fox_prep(pm, ps, bf_row, qn_row, kn_row, lay, bsz, t):
    ka = lay['ka']
    nt = t // FOX_TB
    fixed = lambda b, i: (0, 0)
    rows = lambda b, i: (b * nt + i, 0)
    hm = lambda b, i: (b, 0, i, 0)
    return pl.pallas_call(
        functools.partial(_fox_prep_kernel, ka=ka),
        grid=(bsz, nt),
        in_specs=[
            pl.BlockSpec((FOX_TB, 1024), lambda b, i: (b * nt + i, 2)),
            pl.BlockSpec((FOX_TB, LANES), rows),
            pl.BlockSpec((1, LANES), fixed),
            pl.BlockSpec((1, 256), fixed),
            pl.BlockSpec((1, 256), fixed),
            pl.BlockSpec((256, 256), fixed),
            pl.BlockSpec((FOX_TB, FOX_TB), fixed),
            pl.BlockSpec(lay['mq'].shape, fixed),
            pl.BlockSpec(lay['mk'].shape, fixed),
            pl.BlockSpec(lay['mv'].shape, fixed),
            pl.BlockSpec((1, 4 * ka), fixed),
            pl.BlockSpec((1, 4 * ka), fixed),
        ],
        out_specs=[
            pl.BlockSpec((FOX_TB, 256), rows),
            pl.BlockSpec((FOX_TB, LANES), rows),
            pl.BlockSpec((1, N_HEADS, FOX_TB, ka), hm),
            pl.BlockSpec((1, N_HEADS, FOX_TB, ka), hm),
            pl.BlockSpec((1, N_HEADS, FOX_TB, LANES), hm),
        ],
        out_shape=[
            jax.ShapeDtypeStruct((bsz * t, 256), F32),
            jax.ShapeDtypeStruct((bsz * t, LANES), F32),
            jax.ShapeDtypeStruct((bsz, N_HEADS, t, ka), BF16),
            jax.ShapeDtypeStruct((bsz, N_HEADS, t, ka), BF16),
            jax.ShapeDtypeStruct((bsz, N_HEADS, t, LANES), BF16),
        ],
        scratch_shapes=[pltpu.VMEM((1, LANES), F32)],
        compiler_params=pltpu.CompilerParams(
            dimension_semantics=("parallel", "arbitrary"), vmem_limit_bytes=VMEM_LIMIT),
        name="fox_prep",
    )(pm, ps, bf_row, qn_row, kn_row, lay['blk'], lay['tri'], lay['mq'], lay['mk'], lay['mv'],
      lay['rq'], lay['rk'])


def _fox_flash_kernel(qa_ref, ka_ref, vv_ref, g_ref, on_ref, o_ref, *, hp):
    i = pl.program_id(1)
    tq = FOX_TQ
    row = lax.broadcasted_iota(jnp.int32, (tq, tq), 0)
    col = lax.broadcasted_iota(jnp.int32, (tq, tq), 1)
    outs = []
    for h in range(N_HEADS):
        qa = qa_ref[0, h]

        def tile(j, carry, masked, h=h, qa=qa):
            m, l, acc = carry
            start = pl.multiple_of(j * tq, tq)
            kt = ka_ref[0, h, pl.ds(start, tq), :]
            s = lax.dot_general(qa, kt, (((1,), (1,)), ((), ())), preferred_element_type=F32)
            if masked:
                s = jnp.where(row >= col, s, NEG_BIG)
            m_new = jnp.maximum(m, jnp.max(s, axis=-1, keepdims=True))
            alpha = jnp.exp(m - m_new)
            p = jnp.exp(s - m_new)
            l = alpha * l + jnp.sum(p, axis=-1, keepdims=True)
            vt = vv_ref[0, h, pl.ds(start, tq), :]
            if hp:
                p_hi, p_lo = _split_bf16(p)
                pv = _dot(p_hi, vt) + _dot(p_lo, vt)
            else:
                pv = _dot(p.astype(BF16), vt)
            return m_new, l, alpha * acc + pv

        init = (jnp.full((tq, 1), NEG_BIG, F32), jnp.zeros((tq, 1), F32), jnp.zeros((tq, LANES), F32))
        carry = lax.fori_loop(0, i, lambda j, c: tile(j, c, False), init)
        m, l, acc = tile(i, carry, True)
        o = (acc[:, :HEAD_DIM] + acc[:, HEAD_DIM:]) / l
        o = o * lax.rsqrt(jnp.mean(o * o, axis=-1, keepdims=True) + EPS)
        sl = slice(h * HEAD_DIM, (h + 1) * HEAD_DIM)
        outs.append(o * on_ref[:, sl] * jax.nn.sigmoid(g_ref[:, sl]))
    o_ref[...] = jnp.concatenate(outs, axis=-1)


def _fox_flash(qa, ka, vv, pm, on_row, bsz, t, hp):
    kad = qa.shape[-1]
    nq = t // FOX_TQ
    whole = lambda b, i: (b, 0, 0, 0)
    return pl.pallas_call(
        functools.partial(_fox_flash_kernel, hp=hp),
        grid=(bsz, nq),
        in_specs=[
            pl.BlockSpec((1, N_HEADS, FOX_TQ, kad), lambda b, i: (b, 0, i, 0)),
            pl.BlockSpec((1, N_HEADS, t, kad), whole, pipeline_mode=pl.Buffered(1)),
            pl.BlockSpec((1, N_HEADS, t, LANES), whole, pipeline_mode=pl.Buffered(1)),
            pl.BlockSpec((FOX_TQ, 256), lambda b, i: (b * nq + i, 11)),
            pl.BlockSpec((1, 256), lambda b, i: (0, 0)),
        ],
        out_specs=pl.BlockSpec((FOX_TQ, 256), lambda b, i: (b * nq + i, 0)),
        out_shape=jax.ShapeDtypeStruct((bsz * t, 256), F32),
        compiler_params=pltpu.CompilerParams(
            dimension_semantics=("parallel", "arbitrary"), vmem_limit_bytes=56 * 1024 * 1024),
        name="fox_flash",
    )(qa, ka, vv, pm, on_row)


def _fox_prompt(pm, ps, b_f, q_norm, k_norm, out_norm, bsz, t, hp):
    lay = _fox_layout(hp)
    bf_row = jnp.zeros((1, LANES), F32).at[0, FOX_LF_LANE:FOX_LF_LANE + N_HEADS].set(b_f)
    tile4 = lambda g: jnp.tile(g, N_HEADS).reshape(1, 256)
    krow, lf, qa, ka, vv = _fox_prep(pm, ps, bf_row, tile4(q_norm), tile4(k_norm), lay, bsz, t)
    o = _fox_flash(qa, ka, vv, pm, out_norm.reshape(1, 256), bsz, t, hp)
    return o, krow, lf[:, FOX_LF_LANE:FOX_LF_LANE + N_HEADS]


FOX_PAGES_PER_STEP = 8
_UST = np.triu(np.ones((PAGE_SIZE, PAGE_SIZE), np.float32), k=0).T - np.eye(PAGE_SIZE, dtype=np.float32)


def _fox_dec_kernel(pt_ref, pc_ref, ps_ref, bf_ref, qn_ref, kn_ref, on_ref, blk_ref, ust_ref, *refs):
    npp = FOX_PAGES_PER_STEP
    k_refs, v_refs, lf_refs = refs[0:npp], refs[npp:2 * npp], refs[2 * npp:3 * npp]
    o_ref, krow_ref, lfrow_ref, q3_s, m_s, l_s, acc_s, carry_s = refs[3 * npp:]
    j = pl.program_id(1)
    hrow = lax.broadcasted_iota(jnp.int32, (8, 256), 0)
    lane = lax.broadcasted_iota(jnp.int32, (8, 256), 1)
    hm = (lane >= hrow * HEAD_DIM) & (lane < (hrow + 1) * HEAD_DIM)
    blk = blk_ref[...]

    @pl.when(j == 0)
    def _():
        q = pc_ref[:, 0:256]
        k = pc_ref[:, 256:512]
        v = pc_ref[:, 512:768]
        qn = q * lax.rsqrt(_group_mean(q * q, blk, HEAD_DIM) + EPS) * qn_ref[...]
        kn = k * lax.rsqrt(_group_mean(k * k, blk, HEAD_DIM) + EPS) * kn_ref[...]
        krow_ref[...] = kn
        lf = _log_sigmoid(ps_ref[...] + bf_ref[...])
        lfrow_ref[...] = lf
        qb = jnp.where(hm, qn[0:1, :] * (HEAD_DIM ** -0.5), 0.0)
        q_hi, q_lo = _split_bf16(qb)
        q3_s[...] = jnp.concatenate([q_hi, q_lo, q_hi], axis=-1)
        m_s[...] = jnp.sum(qb * kn[0:1, :], axis=-1, keepdims=True)
        l_s[...] = jnp.ones_like(l_s)
        acc_s[...] = jnp.where(hm, v[0:1, :], 0.0)
        r8 = lax.broadcasted_iota(jnp.int32, (8, LANES), 0)
        l8 = lax.broadcasted_iota(jnp.int32, (8, LANES), 1)
        carry_s[...] = jnp.sum(jnp.where(l8 == r8 + FOX_LF_LANE, lf[0:1, :], 0.0), axis=-1, keepdims=True)

    q3 = q3_s[...]
    m, l, acc, carry = m_s[...], l_s[...], acc_s[...], carry_s[...]
    ust = ust_ref[...]
    for r in reversed(range(npp)):
        k_hi, k_lo = _split_bf16(k_refs[r][...])
        k3 = jnp.concatenate([k_hi, k_hi, k_lo], axis=-1)
        s = lax.dot_general(q3, k3, (((1,), (1,)), ((), ())), preferred_element_type=F32)
        lfp = lf_refs[r][...]
        s = s + _mm_exact_r(lfp, ust) + carry
        carry = carry + jnp.sum(lfp, axis=-1, keepdims=True)
        m_new = jnp.maximum(m, jnp.max(s, axis=-1, keepdims=True))
        alpha = jnp.exp(m - m_new)
        p = jnp.exp(s - m_new)
        l = alpha * l + jnp.sum(p, axis=-1, keepdims=True)
        p_hi, p_lo = _split_bf16(p)
        v_hi, v_lo = _split_bf16(v_refs[r][...])
        pv = _dot(jnp.concatenate([p_hi, p_lo, p_hi], axis=-1), jnp.concatenate([v_hi, v_hi, v_lo], axis=0))
        acc = alpha * acc + pv
        m = m_new
    m_s[...], l_s[...], acc_s[...], carry_s[...] = m, l, acc, carry

    @pl.when(j == pl.num_programs(1) - 1)
    def _():
        o = jnp.sum(jnp.where(hm, acc / l, 0.0), axis=0, keepdims=True)
        o = jnp.broadcast_to(o, (8, 256))
        o = o * lax.rsqrt(_group_mean(o * o, blk, HEAD_DIM) + EPS)
        o_ref[...] = o * on_ref[...] * jax.nn.sigmoid(pc_ref[:, 768:1024])


def _fox_sample(pmr, psr, cache_k, cache_v, cache_logf, page_table, l, b_f, q_norm, k_norm, out_norm):
    bsz, n_pages = page_table.shape
    npp = FOX_PAGES_PER_STEP
    nsteps = n_pages // npp
    depth, n_phys = cache_k.shape[0], cache_k.shape[1]
    ck = cache_k.reshape(depth, n_phys, PAGE_SIZE, 256)
    cv = cache_v.reshape(depth, n_phys, PAGE_SIZE, 256)
    clf = jnp.pad(jnp.swapaxes(cache_logf, 2, 3), ((0, 0), (0, 0), (0, 8 - N_HEADS), (0, 0)))
    bf_row = jnp.zeros((1, LANES), F32).at[0, FOX_LF_LANE:FOX_LF_LANE + N_HEADS].set(b_f)
    tile4 = lambda g: jnp.tile(g, N_HEADS).reshape(1, 256)
    fixed = lambda b, j, pt: (0, 0)

    def page_spec(r, width):
        return pl.BlockSpec((None, None, width[0], width[1]),
                            lambda b, j, pt, r=r: (l, pt[b, (nsteps - 1 - j) * npp + r], 0, 0))

    in_specs = [
        pl.BlockSpec((8, 1024), lambda b, j, pt: (b, 2)),
        pl.BlockSpec((8, LANES), lambda b, j, pt: (b, 0)),
        pl.BlockSpec((1, LANES), fixed),
        pl.BlockSpec((1, 256), fixed),
        pl.BlockSpec((1, 256), fixed),
        pl.BlockSpec((1, 256), fixed),
        pl.BlockSpec((256, 256), fixed),
        pl.BlockSpec((PAGE_SIZE, PAGE_SIZE), fixed),
    ]
    in_specs += [page_spec(r, (PAGE_SIZE, 256)) for r in range(npp)]
    in_specs += [page_spec(r, (PAGE_SIZE, 256)) for r in range(npp)]
    in_specs += [page_spec(r, (8, PAGE_SIZE)) for r in range(npp)]
    rows = lambda b, j, pt: (b, 0)
    o, krow, lfrow = pl.pallas_call(
        _fox_dec_kernel,
        grid_spec=pltpu.PrefetchScalarGridSpec(
            num_scalar_prefetch=1,
            grid=(bsz, nsteps),
            in_specs=in_specs,
            out_specs=[pl.BlockSpec((8, 256), rows), pl.BlockSpec((8, 256), rows),
                       pl.BlockSpec((8, LANES), rows)],
            scratch_shapes=[pltpu.VMEM((8, 768), BF16), pltpu.VMEM((8, 1), F32), pltpu.VMEM((8, 1), F32),
                            pltpu.VMEM((8, 256), F32), pltpu.VMEM((8, 1), F32)],
        ),
        out_shape=[jax.ShapeDtypeStruct((bsz * 8, 256), F32), jax.ShapeDtypeStruct((bsz * 8, 256), F32),
                   jax.ShapeDtypeStruct((bsz * 8, LANES), F32)],
        compiler_params=pltpu.CompilerParams(
            dimension_semantics=("parallel", "arbitrary"), vmem_limit_bytes=VMEM_LIMIT),
        name="fox_sample",
    )(page_table, pmr, psr, bf_row, tile4(q_norm), tile4(k_norm), out_norm.reshape(1, 256),
      jnp.asarray(_BLK256, BF16), jnp.asarray(_UST, BF16), *([ck] * npp), *([cv] * npp), *([clf] * npp))
    first = lambda a: a.reshape(bsz, 8, -1)[:, 0]
    return first(o), first(krow), first(lfrow)[:, FOX_LF_LANE:FOX_LF_LANE + N_HEADS]


CHUNK = 128
ROW0 = 8


def _mm(a, b, hp):
    if hp:
        a_hi, a_lo = _split_bf16(a)
        b_hi, b_lo = _split_bf16(b)
        return _dot3(a_hi, a_lo, b_hi, b_lo)
    return _dot(a.astype(BF16), b.astype(BF16))


def _mm_nt(a, b, hp):
    dn = (((1,), (1,)), ((), ()))
    if hp:
        a_hi, a_lo = _split_bf16(a)
        b_hi, b_lo = _split_bf16(b)
        a3 = jnp.concatenate([a_hi, a_lo, a_hi], axis=-1)
        b3 = jnp.concatenate([b_hi, b_hi, b_lo], axis=-1)
        return lax.dot_general(a3, b3, dn, preferred_element_type=F32)
    return lax.dot_general(a.astype(BF16), b.astype(BF16), dn, preferred_element_type=F32)


def _mm_exact(sel, x):
    hi, mid, lo = _split3_bf16(x)
    return _dot(sel, hi) + _dot(sel, mid) + _dot(sel, lo)


def _mm_exact_r(x, sel):
    hi, mid, lo = _split3_bf16(x)
    return _dot(hi, sel) + _dot(mid, sel) + _dot(lo, sel)


def _softplus(x):
    return jnp.maximum(x, 0.0) + jnp.log1p(jnp.exp(-jnp.abs(x)))


def _silu(x):
    return x * jax.nn.sigmoid(x)


def _stage_rows(buf, blk_ref, lo, hi, prev_ref, rows_in, first):
    @pl.when(first)
    def _():
        buf[ROW0 - 3:ROW0, :] = prev_ref[0]
        if rows_in < CHUNK:
            buf[ROW0 + rows_in:ROW0 + CHUNK, :] = jnp.zeros((CHUNK - rows_in, hi - lo), F32)

    @pl.when(jnp.logical_not(first))
    def _():
        buf[ROW0 - 3:ROW0, :] = buf[ROW0 + CHUNK - 3:ROW0 + CHUNK, :]

    buf[ROW0:ROW0 + rows_in, :] = blk_ref[:, lo:hi]


def _conv4(buf, w_ref):
    acc = w_ref[0:1, :] * buf[ROW0 - 3:ROW0 - 3 + CHUNK, :]
    for j in range(1, CONV_WIDTH):
        acc = acc + w_ref[j:j + 1, :] * buf[ROW0 - 3 + j:ROW0 - 3 + j + CHUNK, :]
    return acc


def _pad_rows(x, rows_in):
    if rows_in == CHUNK:
        return x
    return jnp.concatenate([x, jnp.zeros((CHUNK - rows_in, x.shape[1]), x.dtype)], axis=0)


def _head_expand(first_lane, width):
    e = np.zeros((LANES, N_HEADS * width), np.float32)
    for h in range(N_HEADS):
        e[first_lane + h, h * width:(h + 1) * width] = 1.0
    return jnp.asarray(e, BF16)


_TRI = np.tril(np.ones((CHUNK, CHUNK), np.float32))


SSD_DT_LANE = 12


def _ssd_kernel(pd_ref, ps_ref, cprev_ref, s0_ref, w_ref, cb_ref, dtb_ref, alog_ref, dsk_ref, gn_ref,
                e4_ref, tri_ref, y_ref, sfin_ref, buf, s_scr, *, rows_in, t_valid, hp):
    i = pl.program_id(1)
    first = i == 0

    @pl.when(first)
    def _():
        s_scr[...] = s0_ref[0]

    _stage_rows(buf, pd_ref, 256, 1024, cprev_ref, rows_in, first)
    xbc = _silu(_conv4(buf, w_ref) + cb_ref[...])
    xs = xbc[:, 0:256]
    bm = xbc[:, 256:512]
    cm = xbc[:, 512:768]
    z = _pad_rows(pd_ref[:, 0:256], rows_in)
    pre = _mm_exact_r(_pad_rows(ps_ref[...], rows_in), e4_ref[...])
    dt = _softplus(pre + dtb_ref[...])
    tpos = i * CHUNK + lax.broadcasted_iota(jnp.int32, (CHUNK, 1), 0)
    dt = jnp.where(tpos < t_valid, dt, 0.0)
    la = -jnp.exp(alog_ref[...]) * dt
    b = _mm_exact(tri_ref[...], la)
    row = lax.broadcasted_iota(jnp.int32, (CHUNK, CHUNK), 0)
    col = lax.broadcasted_iota(jnp.int32, (CHUNK, CHUNK), 1)
    causal = row >= col
    lane = lax.broadcasted_iota(jnp.int32, (1, 256), 1)
    s_prev = s_scr[...]
    cb = [_mm_nt(cm[:, g * 128:(g + 1) * 128], bm[:, g * 128:(g + 1) * 128], hp) for g in range(D_GROUPS)]
    y = jnp.zeros((CHUNK, 256), F32)
    s_new = jnp.zeros((D_STATE, 256), F32)
    for h in range(N_HEADS):
        g = h // (N_HEADS // D_GROUPS)
        bh = b[:, h * 128:(h + 1) * 128]
        dth = dt[:, h * 128:(h + 1) * 128]
        hmask = (lane >= h * HEAD_DIM) & (lane < (h + 1) * HEAD_DIM)
        dt2 = jnp.concatenate([dth, dth], axis=-1)
        xdt = jnp.where(hmask, xs * dt2, 0.0)
        rel = jnp.where(causal, jnp.exp(jnp.where(causal, bh - bh.T, 0.0)), 0.0)
        y = y + _mm(cb[g] * rel, xdt, hp)
        eb = jnp.exp(bh)
        y = y + _mm(cm[:, g * 128:(g + 1) * 128] * eb, jnp.where(hmask, s_prev, 0.0), hp)
        b_last = bh[CHUNK - 1:CHUNK, :]
        kdec = bm[:, g * 128:(g + 1) * 128] * jnp.exp(b_last - bh)
        s_new = s_new + _mm(kdec.T, xdt, hp)
        a2 = jnp.exp(jnp.concatenate([b_last, b_last], axis=-1))
        s_new = s_new + jnp.where(hmask, a2 * s_prev, 0.0)
    s_scr[...] = s_new
    sfin_ref[0] = s_new
    y = (y + xs * dsk_ref[...]) * _silu(z)
    outs = []
    for g in range(D_GROUPS):
        yg = y[:, g * 128:(g + 1) * 128]
        outs.append(yg * lax.rsqrt(jnp.mean(yg * yg, axis=-1, keepdims=True) + EPS))
    yn = jnp.concatenate(outs, axis=-1) * gn_ref[...]
    y_ref[...] = yn[0:rows_in, :]


def _ssd(pm, ps, conv_prev, s0, conv_w, conv_b, a_log, dt_bias, d_skip, norm_g, bsz, t_valid, rows_in, hp):
    nt = pm.shape[0] // (bsz * rows_in)
    fixed = lambda b, i: (0, 0)
    rows = lambda b, i: (b * nt + i, 0)
    rep128 = lambda v: jnp.repeat(v, 128).reshape(1, 512)
    s0l = jnp.transpose(s0, (0, 2, 1, 3)).reshape(bsz, D_STATE, 256)
    y, sfin = pl.pallas_call(
        functools.partial(_ssd_kernel, rows_in=rows_in, t_valid=t_valid, hp=hp),
        grid=(bsz, nt),
        in_specs=[
            pl.BlockSpec((rows_in, 1024), lambda b, i: (b * nt + i, 3)),
            pl.BlockSpec((rows_in, LANES), rows),
            pl.BlockSpec((1, 3, D_CONV_CH), lambda b, i: (b, 0, 0)),
            pl.BlockSpec((1, D_STATE, 256), lambda b, i: (b, 0, 0)),
            pl.BlockSpec((CONV_WIDTH, D_CONV_CH), fixed),
            pl.BlockSpec((1, D_CONV_CH), fixed),
            pl.BlockSpec((1, 512), fixed),
            pl.BlockSpec((1, 512), fixed),
            pl.BlockSpec((1, 256), fixed),
            pl.BlockSpec((1, 256), fixed),
            pl.BlockSpec((LANES, 512), fixed),
            pl.BlockSpec((CHUNK, CHUNK), fixed),
        ],
        out_specs=[
            pl.BlockSpec((rows_in, 256), rows),
            pl.BlockSpec((1, D_STATE, 256), lambda b, i: (b, 0, 0)),
        ],
        out_shape=[
            jax.ShapeDtypeStruct((pm.shape[0], 256), F32),
            jax.ShapeDtypeStruct((bsz, D_STATE, 256), F32),
        ],
        scratch_shapes=[pltpu.VMEM((ROW0 + CHUNK, D_CONV_CH), F32), pltpu.VMEM((D_STATE, 256), F32)],
        compiler_params=pltpu.CompilerParams(
            dimension_semantics=("parallel", "arbitrary"), vmem_limit_bytes=VMEM_LIMIT),
        name="ssd",
    )(pm, ps, conv_prev, s0l, conv_w, conv_b.reshape(1, -1), rep128(dt_bias), rep128(a_log),
      jnp.repeat(d_skip, HEAD_DIM).reshape(1, 256), norm_g.reshape(1, 256),
      _head_expand(SSD_DT_LANE, 128), jnp.asarray(_TRI, BF16))
    return y, jnp.transpose(sfin.reshape(bsz, D_STATE, N_HEADS, HEAD_DIM), (0, 2, 1, 3))


SUB = CHUNK_A
_SUB_ID = np.arange(CHUNK) // SUB
_SAME_SUB = (_SUB_ID[:, None] == _SUB_ID[None, :]).astype(np.float32)
_BLK256 = (np.arange(256)[:, None] // HEAD_DIM == np.arange(256)[None, :] // HEAD_DIM).astype(np.float32)


def _mm_tn(a, b, hp):
    dn = (((0,), (0,)), ((), ()))
    if hp:
        a_hi, a_lo = _split_bf16(a)
        b_hi, b_lo = _split_bf16(b)
        a3 = jnp.concatenate([a_hi, a_lo, a_hi], axis=0)
        b3 = jnp.concatenate([b_hi, b_hi, b_lo], axis=0)
        return lax.dot_general(a3, b3, dn, preferred_element_type=F32)
    return lax.dot_general(a.astype(BF16), b.astype(BF16), dn, preferred_element_type=F32)


def _group_sum(x, ones_blk, hp):
    if hp:
        return _mm_exact_r(x, ones_blk)
    return _dot(x.astype(BF16), ones_blk)


def _hgrn_kernel(pa_ref, s0_ref, c1_ref, c2_ref, oml_ref, gn_ref, blk_ref, t16_ref, l16_ref,
                 o_ref, sfin_ref, kbuf, vbuf, lbuf, st_scr, *, rows_in, t_valid, hp):
    i = pl.program_id(1)

    @pl.when(i == 0)
    def _():
        st_scr[...] = s0_ref[0]
        zeros = jnp.zeros((SUB, 256), F32)
        kbuf[0:SUB, :] = zeros
        vbuf[0:SUB, :] = zeros
        lbuf[0:SUB, :] = zeros

    q = _pad_rows(pa_ref[:, 0:256], rows_in) * (HEAD_DIM ** -0.5)
    zf = _pad_rows(pa_ref[:, 256:512], rows_in)
    v = _pad_rows(pa_ref[:, 512:768], rows_in)
    g = _pad_rows(pa_ref[:, 768:1024], rows_in)
    la = c1_ref[...]
    lb = c2_ref[...] + _log_sigmoid(zf)
    lf = jnp.maximum(la, lb) + jnp.log1p(jnp.exp(-jnp.abs(la - lb)))
    k = oml_ref[...] * jax.nn.sigmoid(-zf)
    tpos = i * CHUNK + lax.broadcasted_iota(jnp.int32, (CHUNK, 1), 0)
    valid = tpos < t_valid
    lf = jnp.where(valid, lf, 0.0)
    k = jnp.where(valid, k, 0.0)
    v = jnp.where(valid, v, 0.0)
    kbuf[SUB:SUB + CHUNK, :] = k
    vbuf[SUB:SUB + CHUNK, :] = v
    lbuf[SUB:SUB + CHUNK, :] = lf

    blk = blk_ref[...]
    sub = lax.broadcasted_iota(jnp.int32, (CHUNK, 1), 0) % SUB
    o = _group_sum(q * k, blk, hp) * v
    bd = jnp.zeros((CHUNK, 256), F32)
    for d in range(1, SUB):
        bd = bd + lbuf[SUB - d + 1:SUB - d + 1 + CHUNK, :]
        m = sub >= d
        ks = kbuf[SUB - d:SUB - d + CHUNK, :]
        tmp = jnp.where(m, q * ks * jnp.exp(jnp.where(m, bd, 0.0)), 0.0)
        o = o + _group_sum(tmp, blk, hp) * vbuf[SUB - d:SUB - d + CHUNK, :]

    b = _mm_exact(t16_ref[...], lf)
    bl = _mm_exact(l16_ref[...], lf)
    qe = q * jnp.exp(b)
    kd = k * jnp.exp(bl - b)
    st = st_scr[...]
    bdmask = blk > 0
    o_rows = []
    for n in range(CHUNK // SUB):
        r = slice(n * SUB, (n + 1) * SUB)
        o_rows.append(_mm_nt(qe[r], st, hp))
        ds = _mm_tn(v[r], kd[r], hp)
        st = st * jnp.exp(bl[n * SUB:n * SUB + 1, :]) + jnp.where(bdmask, ds, 0.0)
    st_scr[...] = st
    sfin_ref[0] = st
    o = o + jnp.concatenate(o_rows, axis=0)
    o = o * lax.rsqrt(_group_mean(o * o, blk, HEAD_DIM) + EPS) * gn_ref[...] * _silu(g)
    o_ref[...] = o[0:rows_in, :]


def _hgrn(pm, lb, norm_g, s0, bsz, t_valid, rows_in, hp):
    nt = pm.shape[0] // (bsz * rows_in)
    fixed = lambda b, i: (0, 0)
    rows = lambda b, i: (b * nt + i, 0)
    lb = jnp.clip(lb, 0.0, LB_CEIL)
    c1 = jnp.log(jnp.maximum(lb, LB_FLOOR)).reshape(1, 256)
    c2 = jnp.log1p(-lb).reshape(1, 256)
    oml = (1.0 - lb).reshape(1, 256)
    eye = jnp.eye(N_HEADS, dtype=bool)[None, :, None, :, None]
    st0 = jnp.where(eye, jnp.swapaxes(s0, 2, 3)[:, :, :, None, :], 0.0).reshape(bsz, 256, 256)
    o, sfin = pl.pallas_call(
        functools.partial(_hgrn_kernel, rows_in=rows_in, t_valid=t_valid, hp=hp),
        grid=(bsz, nt),
        in_specs=[
            pl.BlockSpec((rows_in, 1024), lambda b, i: (b * nt + i, 0)),
            pl.BlockSpec((1, 256, 256), lambda b, i: (b, 0, 0)),
            pl.BlockSpec((1, 256), fixed),
            pl.BlockSpec((1, 256), fixed),
            pl.BlockSpec((1, 256), fixed),
            pl.BlockSpec((1, 256), fixed),
            pl.BlockSpec((256, 256), fixed),
            pl.BlockSpec((CHUNK, CHUNK), fixed),
            pl.BlockSpec((CHUNK, CHUNK), fixed),
        ],
        out_specs=[
            pl.BlockSpec((rows_in, 256), rows),
            pl.BlockSpec((1, 256, 256), lambda b, i: (b, 0, 0)),
        ],
        out_shape=[
            jax.ShapeDtypeStruct((pm.shape[0], 256), F32),
            jax.ShapeDtypeStruct((bsz, 256, 256), F32),
        ],
        scratch_shapes=[pltpu.VMEM((SUB + CHUNK, 256), F32)] * 3 + [pltpu.VMEM((256, 256), F32)],
        compiler_params=pltpu.CompilerParams(
            dimension_semantics=("parallel", "arbitrary"), vmem_limit_bytes=VMEM_LIMIT),
        name="hgrn",
    )(pm, st0, c1, c2, oml, norm_g.reshape(1, 256), jnp.asarray(_BLK256, BF16),
      jnp.asarray(_TRI * _SAME_SUB, BF16), jnp.asarray(_SAME_SUB, BF16))
    sf = sfin.reshape(bsz, N_HEADS, HEAD_DIM, N_HEADS, HEAD_DIM)
    sf = jnp.stack([sf[:, h, :, h, :] for h in range(N_HEADS)], axis=1)
    return o, jnp.swapaxes(sf, 2, 3)


GDN_BETA_LANE = 0
GDN_DT_LANE = 4


def _block_diag_state(s0, bsz):
    eye = jnp.eye(N_HEADS, dtype=bool)[None, :, None, :, None]
    return jnp.where(eye, jnp.swapaxes(s0, 2, 3)[:, :, :, None, :], 0.0).reshape(bsz, 256, 256)


def _unblock_diag_state(st, bsz):
    sf = st.reshape(bsz, N_HEADS, HEAD_DIM, N_HEADS, HEAD_DIM)
    sf = jnp.stack([sf[:, h, :, h, :] for h in range(N_HEADS)], axis=1)
    return jnp.swapaxes(sf, 2, 3)


def _gdn_kernel(pb_ref, ps_ref, cprev_ref, s0_ref, w_ref, dtb_ref, alog_ref, dtb64_ref, alog64_ref, gn_ref,
                blk_ref, tri_ref, eb128_ref, ed128_ref, eb64_ref, ed64_ref,
                o_ref, sfin_ref, buf, st_scr, *, rows_in, t_valid, hp):
    i = pl.program_id(1)
    first = i == 0

    @pl.when(first)
    def _():
        st_scr[...] = s0_ref[0]

    _stage_rows(buf, pb_ref, 0, 768, cprev_ref, rows_in, first)
    qkv = _silu(_conv4(buf, w_ref))
    blk = blk_ref[...]
    q = qkv[:, 0:256]
    k = qkv[:, 256:512]
    v = qkv[:, 512:768]
    q = q * lax.rsqrt(_group_mean(q * q, blk, 1) + EPS) * (HEAD_DIM ** -0.5)
    k = k * lax.rsqrt(_group_mean(k * k, blk, 1) + EPS)
    gate = _pad_rows(pb_ref[:, 768:1024], rows_in)
    ps = _pad_rows(ps_ref[...], rows_in)
    tpos = i * CHUNK + lax.broadcasted_iota(jnp.int32, (CHUNK, 1), 0)
    valid = tpos < t_valid
    tri = tri_ref[...]
    beta128 = jnp.where(valid, jax.nn.sigmoid(_mm_exact_r(ps, eb128_ref[...])), 0.0)
    la128 = jnp.where(valid, -jnp.exp(alog_ref[...]) * _softplus(_mm_exact_r(ps, ed128_ref[...]) + dtb_ref[...]), 0.0)
    b128 = _mm_exact(tri, la128)
    beta64 = jnp.where(valid, jax.nn.sigmoid(_mm_exact_r(ps, eb64_ref[...])), 0.0)
    la64 = jnp.where(valid, -jnp.exp(alog64_ref[...]) * _softplus(_mm_exact_r(ps, ed64_ref[...]) + dtb64_ref[...]), 0.0)
    b64 = _mm_exact(tri, la64)
    eb64 = jnp.exp(b64)
    b_last64 = b64[CHUNK - 1:CHUNK, :]
    kb = k * beta64
    rv = v * beta64
    rk = kb * eb64
    q_dec = q * eb64
    k_dec = k * jnp.exp(b_last64 - b64)

    row = lax.broadcasted_iota(jnp.int32, (CHUNK, CHUNK), 0)
    col = lax.broadcasted_iota(jnp.int32, (CHUNK, CHUNK), 1)
    causal = row >= col
    strict = row > col
    lane = lax.broadcasted_iota(jnp.int32, (1, 256), 1)
    us, ws, atts = [], [], []
    for h in range(N_HEADS):
        hs = slice(h * HEAD_DIM, (h + 1) * HEAD_DIM)
        hmask = (lane >= h * HEAD_DIM) & (lane < (h + 1) * HEAD_DIM)
        bh = b128[:, h * 128:(h + 1) * 128]
        decay = jnp.where(causal, jnp.exp(jnp.where(causal, bh - bh.T, 0.0)), 0.0)
        k_h = jnp.where(hmask, k, 0.0)
        a = -jnp.where(strict, _mm_nt(jnp.where(hmask, kb, 0.0), k_h, True) * decay, 0.0)
        atts.append(_mm_nt(jnp.where(hmask, q, 0.0), k_h, hp) * decay)
        x = jnp.concatenate([rv[:, hs], rk[:, hs]], axis=-1)
        x = x + _mm(a, x, True)
        for _ in range(6):
            a = _mm(a, a, True)
            x = x + _mm(a, x, True)
        us.append(x[:, :HEAD_DIM])
        ws.append(x[:, HEAD_DIM:])
    u_all = jnp.concatenate(us, axis=-1)
    w_all = jnp.concatenate(ws, axis=-1)
    st = st_scr[...]
    v_new = u_all - _mm_nt(w_all, st, hp)
    o = _mm_nt(q_dec, st, hp)
    for h in range(N_HEADS):
        hmask = (lane >= h * HEAD_DIM) & (lane < (h + 1) * HEAD_DIM)
        o = o + _mm(atts[h], jnp.where(hmask, v_new, 0.0), hp)
    st = st * jnp.exp(b_last64) + jnp.where(blk > 0, _mm_tn(v_new, k_dec, hp), 0.0)
    st_scr[...] = st
    sfin_ref[0] = st
    o = o * lax.rsqrt(_group_mean(o * o, blk, HEAD_DIM) + EPS) * gn_ref[...] * _silu(gate)
    o_ref[...] = o[0:rows_in, :]


def _gdn(pm, ps, conv_prev, s0, conv_w, a_log, dt_bias, norm_g, bsz, t_valid, rows_in, hp):
    nt = pm.shape[0] // (bsz * rows_in)
    fixed = lambda b, i: (0, 0)
    rows = lambda b, i: (b * nt + i, 0)
    rep = lambda v, w: jnp.repeat(v, w).reshape(1, N_HEADS * w)
    o, sfin = pl.pallas_call(
        functools.partial(_gdn_kernel, rows_in=rows_in, t_valid=t_valid, hp=hp),
        grid=(bsz, nt),
        in_specs=[
            pl.BlockSpec((rows_in, 1024), lambda b, i: (b * nt + i, 1)),
            pl.BlockSpec((rows_in, LANES), rows),
            pl.BlockSpec((1, 3, B_CONV_CH), lambda b, i: (b, 0, 0)),
            pl.BlockSpec((1, 256, 256), lambda b, i: (b, 0, 0)),
            pl.BlockSpec((CONV_WIDTH, B_CONV_CH), fixed),
            pl.BlockSpec((1, 512), fixed),
            pl.BlockSpec((1, 512), fixed),
            pl.BlockSpec((1, 256), fixed),
            pl.BlockSpec((1, 256), fixed),
            pl.BlockSpec((1, 256), fixed),
            pl.BlockSpec((256, 256), fixed),
            pl.BlockSpec((CHUNK, CHUNK), fixed),
            pl.BlockSpec((LANES, 512), fixed),
            pl.BlockSpec((LANES, 512), fixed),
            pl.BlockSpec((LANES, 256), fixed),
            pl.BlockSpec((LANES, 256), fixed),
        ],
        out_specs=[
            pl.BlockSpec((rows_in, 256), rows),
            pl.BlockSpec((1, 256, 256), lambda b, i: (b, 0, 0)),
        ],
        out_shape=[
            jax.ShapeDtypeStruct((pm.shape[0], 256), F32),
            jax.ShapeDtypeStruct((bsz, 256, 256), F32),
        ],
        scratch_shapes=[pltpu.VMEM((ROW0 + CHUNK, B_CONV_CH), F32), pltpu.VMEM((256, 256), F32)],
        compiler_params=pltpu.CompilerParams(
            dimension_semantics=("parallel", "arbitrary"), vmem_limit_bytes=VMEM_LIMIT),
        name="gdn",
    )(pm, ps, conv_prev, _block_diag_state(s0, bsz), conv_w, rep(dt_bias, 128), rep(a_log, 128),
      rep(dt_bias, 64), rep(a_log, 64), norm_g.reshape(1, 256), jnp.asarray(_BLK256, BF16),
      jnp.asarray(_TRI, BF16), _head_expand(GDN_BETA_LANE, 128), _head_expand(GDN_DT_LANE, 128),
      _head_expand(GDN_BETA_LANE, 64), _head_expand(GDN_DT_LANE, 64))
    return o, _unblock_diag_state(sfin, bsz)


def _head_rmsnorm(x, g):
    y = x * lax.rsqrt(jnp.mean(x * x, axis=-1, keepdims=True) + EPS)
    return y * g.reshape((-1, x.shape[-1]))


def _l2norm(x):
    return x * lax.rsqrt(jnp.sum(x * x, axis=-1, keepdims=True) + EPS)


def _masked_exp(logit, mask):
    return jnp.where(mask, jnp.exp(jnp.where(mask, logit, 0.0)), 0.0)


def _causal_conv(x, w, prev):
    t = x.shape[1]
    xp = jnp.concatenate([prev.astype(x.dtype), x], axis=1)
    y = xp[:, 0:t] * w[0]
    for j in range(1, w.shape[0]):
        y = y + xp[:, j:j + t] * w[j]
    return y, xp[:, t:]


def _pad_time(t, pad):
    return jnp.pad(t, [(0, 0), (0, pad)] + [(0, 0)] * (t.ndim - 2))


def _to_chunks(t, chunk):
    b, tt, h = t.shape[:3]
    t = t.reshape((b, tt // chunk, chunk, h) + t.shape[3:])
    return jnp.moveaxis(t, 3, 1)


def _from_chunks(t):
    b, h, n, c = t.shape[:4]
    t = jnp.moveaxis(t, 1, 3)
    return t.reshape((b, n * c, h) + t.shape[4:])


def _chunk_inputs(arrs, chunk):
    t = arrs[0].shape[1]
    pad = (-t) % chunk
    return [_to_chunks(_pad_time(a.astype(F32), pad), chunk) for a in arrs]


def _chunked_gla(q, k, v, log_f, s0, chunk):
    t = q.shape[1]
    q, k, v, log_f = _chunk_inputs([q, k, v, log_f], chunk)
    b = jnp.cumsum(log_f, axis=3)
    causal = jnp.tril(jnp.ones((chunk, chunk), bool))[:, :, None]
    rel = _masked_exp(b[..., :, None, :] - b[..., None, :, :], causal)
    att = jnp.einsum('bhntk,bhnsk,bhntsk->bhnts', q, k, rel)
    o_intra = jnp.einsum('bhnts,bhnsv->bhntv', att, v)
    b_last = b[..., -1:, :]
    ds = jnp.einsum('bhnsk,bhnsv->nbhkv', k * jnp.exp(b_last - b), v)
    a_chunk = jnp.moveaxis(jnp.exp(b_last[..., 0, :]), 2, 0)

    def step(s, inp):
        a, d = inp
        return a[..., None] * s + d, s

    s_final, s_prev = lax.scan(step, s0.astype(F32), (a_chunk, ds))
    o_inter = jnp.einsum('bhntk,nbhkv->bhntv', q * jnp.exp(b), s_prev)
    return _from_chunks(o_intra + o_inter)[:, :t], s_final


def _chunked_ssd(q, k, v, log_a, s0, chunk):
    t = q.shape[1]
    q, k, v, la = _chunk_inputs([q, k, v, log_a], chunk)
    b = jnp.cumsum(la, axis=-1)
    causal = jnp.tril(jnp.ones((chunk, chunk), bool))
    rel = _masked_exp(b[..., :, None] - b[..., None, :], causal)
    att = jnp.einsum('bhntk,bhnsk->bhnts', q, k) * rel
    o_intra = jnp.einsum('bhnts,bhnsv->bhntv', att, v)
    b_last = b[..., -1:]
    ds = jnp.einsum('bhnsk,bhnsv->nbhkv', k * jnp.exp(b_last - b)[..., None], v)
    a_chunk = jnp.moveaxis(jnp.exp(b_last[..., 0]), 2, 0)

    def step(s, inp):
        a, d = inp
        return a[..., None, None] * s + d, s

    s_final, s_prev = lax.scan(step, s0.astype(F32), (a_chunk, ds))
    o_inter = jnp.einsum('bhntk,nbhkv->bhntv', q * jnp.exp(b)[..., None], s_prev)
    return _from_chunks(o_intra + o_inter)[:, :t], s_final


def _chunked_gated_delta(q, k, v, beta, log_a, s0, chunk):
    t = q.shape[1]
    vd = v.shape[-1]
    q, k, v, beta, la = _chunk_inputs([q, k, v, beta, log_a], chunk)
    b = jnp.cumsum(la, axis=-1)
    causal = jnp.tril(jnp.ones((chunk, chunk), bool))
    strict = jnp.tril(jnp.ones((chunk, chunk), bool), k=-1)
    decay = _masked_exp(b[..., :, None] - b[..., None, :], causal)
    kb = k * beta[..., None]
    m = jnp.where(strict, jnp.einsum('bhntk,bhnsk->bhnts', kb, k) * decay, 0.0)
    rhs = jnp.concatenate([v * beta[..., None], kb * jnp.exp(b)[..., None]], axis=-1)
    sol = lax.linalg.triangular_solve(m + jnp.eye(chunk, dtype=F32), rhs, left_side=True,
                                      lower=True, unit_diagonal=True)
    u, w = sol[..., :vd], sol[..., vd:]
    att = jnp.einsum('bhntk,bhnsk->bhnts', q, k) * decay
    q_dec = q * jnp.exp(b)[..., None]
    k_dec = k * jnp.exp(b[..., -1:] - b)[..., None]
    a_chunk = jnp.exp(b[..., -1])
    xs = tuple(jnp.moveaxis(a, 2, 0) for a in (u, w, att, q_dec, k_dec, a_chunk))

    def step(s, inp):
        u_c, w_c, att_c, qd_c, kd_c, a_c = inp
        v_new = u_c - jnp.einsum('bhtk,bhkv->bhtv', w_c, s)
        o = jnp.einsum('bhtk,bhkv->bhtv', qd_c, s) + jnp.einsum('bhts,bhsv->bhtv', att_c, v_new)
        s = a_c[..., None, None] * s + jnp.einsum('bhsk,bhsv->bhkv', kd_c, v_new)
        return s, o

    s_final, o = lax.scan(step, s0.astype(F32), xs)
    return _from_chunks(jnp.moveaxis(o, 0, 2))[:, :t], s_final


def _hgrn2_mixer(pa, lb, norm_g, s0):
    bsz, t, _ = pa.shape
    w = GROUP_WIDTH
    q, zf, inp, g = (pa[..., j * w:(j + 1) * w] for j in range(4))
    lb = jnp.clip(lb, 0.0, LB_CEIL)
    log_f = jnp.logaddexp(jnp.log(jnp.maximum(lb, LB_FLOOR)), jnp.log1p(-lb) + jax.nn.log_sigmoid(zf))
    k = (1.0 - lb) * jax.nn.sigmoid(-zf)
    heads = lambda a: a.reshape(bsz, t, N_HEADS, -1)
    o, s = _chunked_gla(heads(q) * HEAD_DIM ** -0.5, heads(k), heads(inp), heads(log_f), s0, CHUNK_A)
    o = _head_rmsnorm(o, norm_g) * jax.nn.silu(heads(g))
    return o.reshape(bsz, t, w), s


def _gdn_mixer(pb, beta_pre, dt_pre, conv_w, conv_prev, a_log, dt_bias, norm_g, s0):
    bsz, t, _ = pb.shape
    w = GROUP_WIDTH
    qkv, conv_new = _causal_conv(pb[..., :3 * w], conv_w, conv_prev)
    qkv = jax.nn.silu(qkv)
    heads = lambda a: a.reshape(bsz, t, N_HEADS, -1)
    q = _l2norm(heads(qkv[..., :w])) * HEAD_DIM ** -0.5
    k = _l2norm(heads(qkv[..., w:2 * w]))
    v = heads(qkv[..., 2 * w:3 * w])
    gate = heads(pb[..., 3 * w:4 * w])
    beta = jax.nn.sigmoid(beta_pre)
    log_a = -jnp.exp(a_log) * jax.nn.softplus(dt_pre + dt_bias)
    o, s = _chunked_gated_delta(q, k, v, beta, log_a, s0, CHUNK_B)
    o = _head_rmsnorm(o, norm_g) * jax.nn.silu(gate)
    return o.reshape(bsz, t, w), s, conv_new


def _fox_attention_prompt(q, k, v, log_f):
    b, t, h, d = q.shape
    n_blk = -(-t // Q_BLOCK)
    c = jnp.moveaxis(jnp.cumsum(log_f, axis=1), 2, 1)
    key_pos = jnp.arange(t)

    def block(i):
        start = i * Q_BLOCK
        q_i = lax.dynamic_slice_in_dim(q, start, Q_BLOCK, axis=1)
        c_i = lax.dynamic_slice_in_dim(c, start, Q_BLOCK, axis=2)
        s = jnp.einsum('bqhd,bkhd->bhqk', q_i, k) * (d ** -0.5) + c_i[..., :, None] - c[:, :, None, :]
        allowed = (start + jnp.arange(Q_BLOCK))[:, None] >= key_pos[None, :]
        p = jax.nn.softmax(jnp.where(allowed, s, NEG_BIG), axis=-1)
        return jnp.einsum('bhqk,bkhd->bqhd', p, v)

    o = lax.map(block, jnp.arange(n_blk))
    return jnp.moveaxis(o, 0, 1).reshape(b, n_blk * Q_BLOCK, h, d)[:, :t]


def _fox_attention_sample(q, k, v, log_f, k_past, v_past, logf_past):
    n_past = k_past.shape[1]
    s_new = q.shape[1]
    d = q.shape[-1]
    k_all = jnp.concatenate([k_past, k], axis=1)
    v_all = jnp.concatenate([v_past, v], axis=1)
    c = jnp.cumsum(jnp.concatenate([logf_past, log_f], axis=1), axis=1)
    c = jnp.moveaxis(c, 2, 1)
    s = jnp.einsum('bqhd,bkhd->bhqk', q, k_all) * (d ** -0.5) + c[:, :, n_past:, None] - c[:, :, None, :]
    allowed = (n_past + jnp.arange(s_new))[:, None] >= jnp.arange(n_past + s_new)[None, :]
    p = jax.nn.softmax(jnp.where(allowed, s, NEG_BIG), axis=-1)
    return jnp.einsum('bhqk,bkhd->bqhd', p, v_all)


def _fox_mixer(pc, lf_pre, b_f, q_norm, k_norm, out_norm, past):
    bsz, t, _ = pc.shape
    w = GROUP_WIDTH
    heads = lambda a: a.reshape(bsz, t, N_HEADS, HEAD_DIM)
    q = _head_rmsnorm(heads(pc[..., :w]), q_norm)
    k = _head_rmsnorm(heads(pc[..., w:2 * w]), k_norm)
    v = heads(pc[..., 2 * w:3 * w])
    g = heads(pc[..., 3 * w:4 * w])
    log_f = jax.nn.log_sigmoid(lf_pre + b_f)
    if past is None:
        o = _fox_attention_prompt(q, k, v, log_f)
    else:
        o = _fox_attention_sample(q, k, v, log_f, past[0], past[1], past[2])
    o = _head_rmsnorm(o, out_norm) * jax.nn.sigmoid(g)
    return o.reshape(bsz, t, w), k, v, log_f


def _ssd_mixer(pd, dt_pre, conv_w, conv_b, conv_prev, a_log, dt_bias, d_skip, norm_g, s0):
    bsz, t, _ = pd.shape
    w = GROUP_WIDTH
    gn = D_GROUPS * D_STATE
    rep = N_HEADS // D_GROUPS
    z = pd[..., :w].reshape(bsz, t, N_HEADS, HEAD_DIM)
    xbc, conv_new = _causal_conv(pd[..., w:w + D_CONV_CH], conv_w, conv_prev)
    xbc = jax.nn.silu(xbc + conv_b)
    xs = xbc[..., :w].reshape(bsz, t, N_HEADS, HEAD_DIM)
    bm = jnp.repeat(xbc[..., w:w + gn].reshape(bsz, t, D_GROUPS, D_STATE), rep, axis=2)
    cm = jnp.repeat(xbc[..., w + gn:].reshape(bsz, t, D_GROUPS, D_STATE), rep, axis=2)
    dt = jax.nn.softplus(dt_pre + dt_bias)
    log_a = -jnp.exp(a_log) * dt
    y, s = _chunked_ssd(cm, bm, xs * dt[..., None], log_a, s0, CHUNK_D)
    y = (y + xs * d_skip[:, None]) * jax.nn.silu(z)
    y = _head_rmsnorm(y.reshape(bsz, t, D_GROUPS, -1), norm_g)
    return y.reshape(bsz, t, w), s, conv_new


def _gather_pages(pool, page_table):
    g = pool[page_table]
    return g.reshape((g.shape[0], g.shape[1] * g.shape[2]) + g.shape[3:])


def _prep_weights(prm):
    w_in = prm['w_in']
    wm = w_in[:, :, _MAIN_COLS]
    ws = jnp.pad(w_in[:, :, _SMALL_COLS], ((0, 0), (0, 0), (0, LANES - len(_SMALL_COLS))))
    wr = jnp.concatenate([prm['moe_w_expert'], prm['moe_w_group']], axis=-1)
    n_r = N_EXPERTS + N_EXPERT_GROUPS
    wr = jnp.pad(wr, ((0, 0), (0, 0), (0, LANES - n_r)))
    both = lambda pair: [(pair[0][:D_MODEL], pair[1][:D_MODEL]), (pair[0][D_MODEL:], pair[1][D_MODEL:])]
    ws_split = both(_split_weight(ws.reshape(DEPTH * D_MODEL, LANES)))
    wr_split = both(_split_weight(wr.reshape(DEPTH * D_MODEL, LANES)))
    br = jnp.pad(jnp.concatenate([prm['moe_b_expert'], prm['moe_b_group']], axis=-1),
                 ((0, 0), (0, LANES - n_r)))[:, None, :]
    bf = lambda a: a.astype(BF16)
    prompt, sample = [], []
    for l in range(DEPTH):
        common = dict(ws=ws_split[l], wr=wr_split[l], br=br[l])
        if l == 0:
            wm_p, wo_p = tuple(_split_weight(wm[l])), tuple(_split_weight(prm['w_out'][l]))
        else:
            wm_p, wo_p = (bf(wm[l]),), (bf(prm['w_out'][l]),)
        prompt.append(dict(common, wm=wm_p, wo=wo_p, w1=bf(prm['moe_w1'][l]), w3=bf(prm['moe_w3'][l]),
                           w2=bf(prm['moe_w2'][l]), wg=bf(prm['ple_w_gate'][l]), wp=bf(prm['ple_w_proj'][l])))
        sample.append(dict(common, wm=(wm[l],), wo=(prm['w_out'][l],), w1=prm['moe_w1'][l], w3=prm['moe_w3'][l],
                           w2=prm['moe_w2'][l], wg=prm['ple_w_gate'][l], wp=prm['ple_w_proj'][l]))
    return prompt, sample


def _trunk(x, p, init_state, fox_cache, lb_all, prm, wts, tm, hp_layers):
    s_hgrn0, s_gdn0, c_gdn0, s_ssd0, c_ssd0 = init_state
    bsz, t, _ = x.shape
    n = bsz * t
    h = x.reshape(n, D_MODEL)
    outs = [[] for _ in range(8)]
    row = lambda a: a.reshape(1, -1)
    for l in range(DEPTH):
        hp_mix = hp_layers[l]
        w = wts[l]
        pm, ps = _inproj(h, row(prm['g_mix'][l]), w['wm'], w['ws'][0], w['ws'][1], tm)
        if fox_cache is None:
            o_c, k_c, lf_c = _fox_prompt(pm, ps, prm['fox_b_f'][l], prm['fox_q_norm'][l], prm['fox_k_norm'][l],
                                         prm['fox_out_norm'][l], bsz, t, hp_mix)
            o_c = o_c.reshape(bsz, t, GROUP_WIDTH)
            k_c = k_c.reshape(bsz, t, N_HEADS, HEAD_DIM)
            lf_c = lf_c.reshape(bsz, t, N_HEADS)
            v_c = pm[:, 2560:2816].reshape(bsz, t, N_HEADS, HEAD_DIM)
        rows_in = CHUNK if t % CHUNK == 0 else 8
        if rows_in == CHUNK:
            pmr, psr = pm, ps
        else:
            padr = lambda a: jnp.pad(a.reshape(bsz, t, -1), ((0, 0), (0, rows_in - t), (0, 0))).reshape(
                bsz * rows_in, -1)
            pmr, psr = padr(pm), padr(ps)
        unpad = lambda a: a.reshape(bsz, -1, GROUP_WIDTH)[:, :t]
        o_a, s_a = _hgrn(pmr, lb_all[l], prm['hgrn_norm'][l], s_hgrn0[l], bsz, t, rows_in, hp_mix)
        o_d, s_d = _ssd(pmr, psr, c_ssd0[l], s_ssd0[l], prm['ssd_conv_w'][l], prm['ssd_conv_b'][l],
                        prm['ssd_a_log'][l], prm['ssd_dt_bias'][l], prm['ssd_d'][l], prm['ssd_norm'][l],
                        bsz, t, rows_in, hp_mix)
        o_b, s_b = _gdn(pmr, psr, c_gdn0[l], s_gdn0[l], prm['gdn_conv_w'][l], prm['gdn_a_log'][l],
                        prm['gdn_dt_bias'][l], prm['gdn_norm'][l], bsz, t, rows_in, hp_mix)
        o_a, o_b, o_d = unpad(o_a), unpad(o_b), unpad(o_d)
        pm = pm.reshape(bsz, t, N_MAIN)
        ps = ps.reshape(bsz, t, LANES)
        c_d = jnp.concatenate([c_ssd0[l], pm[:, max(0, t - 3):, 3328:4096]], axis=1)[:, -(CONV_WIDTH - 1):]
        c_b = jnp.concatenate([c_gdn0[l], pm[:, max(0, t - 3):, 1024:1792]], axis=1)[:, -(CONV_WIDTH - 1):]
        if fox_cache is not None:
            cache_k, cache_v, cache_logf, page_table = fox_cache
            o_c, k_c, lf_c = _fox_sample(pmr, psr, cache_k, cache_v, cache_logf, page_table, l,
                                         prm['fox_b_f'][l], prm['fox_q_norm'][l], prm['fox_k_norm'][l],
                                         prm['fox_out_norm'][l])
            o_c = o_c.reshape(bsz, t, GROUP_WIDTH)
            k_c = k_c.reshape(bsz, t, N_HEADS, HEAD_DIM)
            lf_c = lf_c.reshape(bsz, t, N_HEADS)
            v_c = pm[..., 2560:2816].reshape(bsz, t, N_HEADS, HEAD_DIM)
        mix = jnp.concatenate([o_a, o_b, o_c, o_d], axis=-1).reshape(n, D_MODEL)
        h2, u2, cw = _outproj(h, mix, w['wo'], row(prm['g_ffn'][l]), w['wr'][0], w['wr'][1], w['br'], tm)
        y = _moe(u2, cw, w['w1'], w['w3'], w['w2'], tm)
        h = _ple(h2, y, p[l].reshape(n, PLE_DIM), row(prm['g_ple'][l]), w['wg'], w['wp'],
                 row(prm['g_final']), tm, final=(l == DEPTH - 1))
        for acc, val in zip(outs, (k_c, v_c, lf_c, s_a, s_b, c_b, s_d, c_d)):
            acc.append(val)
    return (h.reshape(bsz, t, D_MODEL),) + tuple(jnp.stack(acc) for acc in outs)


def _hgrn_lower_bounds(lb_param):
    sm = jax.nn.softmax(lb_param, axis=0)
    return jnp.concatenate([jnp.zeros_like(sm[:1]), jnp.cumsum(sm[1:], axis=0)], axis=0)


def kernel(x_prompt, x_sample, cache_fox_k, cache_fox_v, cache_fox_logf, state_hgrn, state_gdn,
           state_gdn_conv, state_ssd, state_ssd_conv, page_table, p_prompt, p_sample, w_in, w_out,
           g_mix, g_ffn, g_ple, g_final, hgrn_lb, hgrn_norm, gdn_conv_w, gdn_a_log, gdn_dt_bias,
           gdn_norm, fox_b_f, fox_q_norm, fox_k_norm, fox_out_norm, ssd_conv_w, ssd_conv_b, ssd_a_log,
           ssd_dt_bias, ssd_d, ssd_norm, moe_w_group, moe_b_group, moe_w_expert, moe_b_expert, moe_w1,
           moe_w3, moe_w2, ple_w_gate, ple_w_proj):
    prm = dict(w_in=w_in, w_out=w_out, g_mix=g_mix, g_ffn=g_ffn, g_ple=g_ple, g_final=g_final,
               hgrn_norm=hgrn_norm, gdn_conv_w=gdn_conv_w, gdn_a_log=gdn_a_log, gdn_dt_bias=gdn_dt_bias,
               gdn_norm=gdn_norm, fox_b_f=fox_b_f, fox_q_norm=fox_q_norm, fox_k_norm=fox_k_norm,
               fox_out_norm=fox_out_norm, ssd_conv_w=ssd_conv_w, ssd_conv_b=ssd_conv_b, ssd_a_log=ssd_a_log,
               ssd_dt_bias=ssd_dt_bias, ssd_d=ssd_d, ssd_norm=ssd_norm, moe_w_group=moe_w_group,
               moe_b_group=moe_b_group, moe_w_expert=moe_w_expert, moe_b_expert=moe_b_expert,
               moe_w1=moe_w1, moe_w3=moe_w3, moe_w2=moe_w2, ple_w_gate=ple_w_gate, ple_w_proj=ple_w_proj)
    wts_prompt, wts_sample = _prep_weights(prm)
    lb_all = _hgrn_lower_bounds(hgrn_lb)
    bp = x_prompt.shape[0]
    zero_state = (jnp.zeros((DEPTH, bp, N_HEADS, HEAD_DIM, HEAD_DIM), F32),
                  jnp.zeros((DEPTH, bp, N_HEADS, HEAD_DIM, HEAD_DIM), F32),
                  jnp.zeros((DEPTH, bp, CONV_WIDTH - 1, B_CONV_CH), F32),
                  jnp.zeros((DEPTH, bp, N_HEADS, D_STATE, HEAD_DIM), F32),
                  jnp.zeros((DEPTH, bp, CONV_WIDTH - 1, D_CONV_CH), F32))
    pr = _trunk(x_prompt, p_prompt, zero_state, None, lb_all, prm, wts_prompt, tm=512, hp_layers=(True, False))
    sm = _trunk(x_sample, p_sample, (state_hgrn, state_gdn, state_gdn_conv, state_ssd, state_ssd_conv),
                (cache_fox_k, cache_fox_v, cache_fox_logf, page_table), lb_all, prm, wts_sample, tm=32,
                hp_layers=(True, True))
    return (pr[0], sm[0]) + tuple(pr[1:]) + tuple(sm[1:])
```

```python
import functools
import math

import jax
import jax.numpy as jnp
import numpy as np
from jax import lax
from jax.experimental import pallas as pl
from jax.experimental.pallas import tpu as pltpu

F32 = jnp.float32
BF16 = jnp.bfloat16

D_MODEL = 1024
DEPTH = 2
PAGE_SIZE = 128
EPS = 1e-6
NEG_BIG = -1e30
LB_FLOOR = 1e-30
LB_CEIL = 1.0 - 1e-6
PLE_DIM = 256
GROUP_WIDTH = 256
HEAD_DIM = 64
N_HEADS = 4
D_GROUPS = 2
D_STATE = 128
CONV_WIDTH = 4
B_CONV_CH = 3 * GROUP_WIDTH
D_CONV_CH = GROUP_WIDTH + 2 * D_GROUPS * D_STATE
SIZE_A = 4 * GROUP_WIDTH
SIZE_B = 4 * GROUP_WIDTH + 2 * N_HEADS
SIZE_C = 4 * GROUP_WIDTH + N_HEADS
SIZE_D = GROUP_WIDTH + D_CONV_CH + N_HEADS
OFF_B = SIZE_A
OFF_C = OFF_B + SIZE_B
OFF_D = OFF_C + SIZE_C
N_IN = OFF_D + SIZE_D
CHUNK_A = 16
CHUNK_B = 64
CHUNK_D = 128
Q_BLOCK = 128
N_EXPERT_GROUPS = 4
EXPERTS_PER_GROUP = 4
N_EXPERTS = 16
D_EXPERT = 512

LANES = 128
N_MAIN = 4096
VMEM_LIMIT = 48 * 1024 * 1024

_MAIN_COLS = np.concatenate([
    np.arange(0, SIZE_A),
    np.arange(OFF_B, OFF_B + 4 * GROUP_WIDTH),
    np.arange(OFF_C, OFF_C + 4 * GROUP_WIDTH),
    np.arange(OFF_D, OFF_D + 4 * GROUP_WIDTH),
])
_SMALL_COLS = np.concatenate([
    np.arange(OFF_B + 4 * GROUP_WIDTH, OFF_B + SIZE_B),
    np.arange(OFF_C + 4 * GROUP_WIDTH, OFF_C + SIZE_C),
    np.arange(OFF_D + 4 * GROUP_WIDTH, OFF_D + SIZE_D),
])


def _split_bf16(x):
    hi = x.astype(BF16)
    lo = (x - hi.astype(F32)).astype(BF16)
    return hi, lo


def _dot(a, b):
    return jnp.dot(a, b, preferred_element_type=F32)


def _dot3(a_hi, a_lo, b_hi, b_lo):
    return _dot(a_hi, b_hi) + _dot(a_lo, b_hi) + _dot(a_hi, b_lo)


def _rms(x, g):
    return x * lax.rsqrt(jnp.mean(x * x, axis=-1, keepdims=True) + EPS) * g


def _wmode(w):
    if len(w) == 2:
        return 'x3'
    return 'f32' if w[0].dtype == F32 else 'x1'


def _wcount(mode):
    return 2 if mode == 'x3' else 1


def _wload(refs, mode, idx=None):
    get = (lambda r: r[...]) if idx is None else (lambda r: r[idx])
    if mode == 'x1':
        return get(refs[0]), None
    if mode == 'x3':
        return get(refs[0]), get(refs[1])
    return _split_bf16(get(refs[0]))


def _wdot(a, refs, mode, idx=None):
    w_hi, w_lo = _wload(refs, mode, idx)
    if w_lo is None:
        return _dot(a.astype(BF16), w_hi)
    a_hi, a_lo = _split_bf16(a)
    return _dot3(a_hi, a_lo, w_hi, w_lo)


def _split_kernel(w_ref, hi_ref, lo_ref):
    hi, lo = _split_bf16(w_ref[...])
    hi_ref[...] = hi
    lo_ref[...] = lo


def _split_weight(w):
    rows, cols = w.shape
    spec = pl.BlockSpec((256, cols), lambda i: (i, 0))
    return pl.pallas_call(
        _split_kernel,
        grid=(rows // 256,),
        in_specs=[spec],
        out_specs=[spec, spec],
        out_shape=[jax.ShapeDtypeStruct((rows, cols), BF16)] * 2,
        compiler_params=pltpu.CompilerParams(dimension_semantics=("parallel",)),
        name="split_weight",
    )(w)


def _inproj_kernel(x_ref, g_ref, *refs, mode):
    nw = _wcount(mode)
    wm_refs = refs[:nw]
    wsh_ref, wsl_ref, om_ref, os_ref, uh_ref, ul_ref = refs[nw:]

    @pl.when(pl.program_id(1) == 0)
    def _():
        u = _rms(x_ref[...], g_ref[...])
        uh, ul = _split_bf16(u)
        uh_ref[...] = uh
        ul_ref[...] = ul
        os_ref[...] = _dot3(uh, ul, wsh_ref[...], wsl_ref[...])

    w_hi, w_lo = _wload(wm_refs, mode)
    if w_lo is None:
        om_ref[...] = _dot(uh_ref[...], w_hi)
    else:
        om_ref[...] = _dot3(uh_ref[...], ul_ref[...], w_hi, w_lo)


def _inproj(x, g, wm, wsh, wsl, tm):
    n = x.shape[0]
    tn = 1024
    return pl.pallas_call(
        functools.partial(_inproj_kernel, mode=_wmode(wm)),
        grid=(n // tm, N_MAIN // tn),
        in_specs=[
            pl.BlockSpec((tm, D_MODEL), lambda i, j: (i, 0)),
            pl.BlockSpec((1, D_MODEL), lambda i, j: (0, 0)),
        ] + [pl.BlockSpec((D_MODEL, tn), lambda i, j: (0, j))] * len(wm) + [
            pl.BlockSpec((D_MODEL, LANES), lambda i, j: (0, 0)),
            pl.BlockSpec((D_MODEL, LANES), lambda i, j: (0, 0)),
        ],
        out_specs=[
            pl.BlockSpec((tm, tn), lambda i, j: (i, j)),
            pl.BlockSpec((tm, LANES), lambda i, j: (i, 0)),
        ],
        out_shape=[
            jax.ShapeDtypeStruct((n, N_MAIN), F32),
            jax.ShapeDtypeStruct((n, LANES), F32),
        ],
        scratch_shapes=[pltpu.VMEM((tm, D_MODEL), BF16), pltpu.VMEM((tm, D_MODEL), BF16)],
        compiler_params=pltpu.CompilerParams(
            dimension_semantics=("parallel", "arbitrary"), vmem_limit_bytes=VMEM_LIMIT),
        name="inproj",
    )(x, g, *wm, wsh, wsl)


def _route(logits):
    lane = lax.broadcasted_iota(jnp.int32, logits.shape, 1)
    gmask = (lane >= N_EXPERTS) & (lane < N_EXPERTS + N_EXPERT_GROUPS)
    gl = jnp.where(gmask, logits, -jnp.inf)
    gmax = jnp.max(gl, axis=-1, keepdims=True)
    gidx = jnp.min(jnp.where(gl == gmax, lane, 4 * LANES), axis=-1, keepdims=True) - N_EXPERTS
    gw = 1.0 / jnp.sum(jnp.where(gmask, jnp.exp(gl - gmax), 0.0), axis=-1, keepdims=True)
    lo = gidx * EXPERTS_PER_GROUP
    emask = (lane >= lo) & (lane < lo + EXPERTS_PER_GROUP)
    el = jnp.where(emask, logits, -jnp.inf)
    m1 = jnp.max(el, axis=-1, keepdims=True)
    i1 = jnp.min(jnp.where(el == m1, lane, 4 * LANES), axis=-1, keepdims=True)
    el2 = jnp.where(lane == i1, -jnp.inf, el)
    m2 = jnp.max(el2, axis=-1, keepdims=True)
    i2 = jnp.min(jnp.where(el2 == m2, lane, 4 * LANES), axis=-1, keepdims=True)
    e2 = jnp.exp(m2 - m1)
    den = 1.0 + e2
    g1 = gw / den
    g2 = gw * e2 / den
    return jnp.where(lane == i1, g1, jnp.where(lane == i2, g2, 0.0))


def _outproj_kernel(h_ref, mix_ref, *refs, mode):
    nw = _wcount(mode)
    g_ref, wrh_ref, wrl_ref, br_ref, h2_ref, u2_ref, cw_ref = refs[nw:]
    h2 = h_ref[...] + _wdot(mix_ref[...], refs[:nw], mode)
    h2_ref[...] = h2
    u = _rms(h2, g_ref[...])
    u2_ref[...] = u
    uh, ul = _split_bf16(u)
    logits = _dot3(uh, ul, wrh_ref[...], wrl_ref[...]) + br_ref[...]
    cw_ref[...] = _route(logits)


def _outproj(h, mix, wo, g, wrh, wrl, br, tm):
    n = h.shape[0]
    row = lambda i: (i, 0)
    fixed = lambda i: (0, 0)
    return pl.pallas_call(
        functools.partial(_outproj_kernel, mode=_wmode(wo)),
        grid=(n // tm,),
        in_specs=[
            pl.BlockSpec((tm, D_MODEL), row),
            pl.BlockSpec((tm, D_MODEL), row),
        ] + [pl.BlockSpec((D_MODEL, D_MODEL), fixed)] * len(wo) + [
            pl.BlockSpec((1, D_MODEL), fixed),
            pl.BlockSpec((D_MODEL, LANES), fixed),
            pl.BlockSpec((D_MODEL, LANES), fixed),
            pl.BlockSpec((1, LANES), fixed),
        ],
        out_specs=[
            pl.BlockSpec((tm, D_MODEL), row),
            pl.BlockSpec((tm, D_MODEL), row),
            pl.BlockSpec((tm, LANES), row),
        ],
        out_shape=[
            jax.ShapeDtypeStruct((n, D_MODEL), F32),
            jax.ShapeDtypeStruct((n, D_MODEL), F32),
            jax.ShapeDtypeStruct((n, LANES), F32),
        ],
        compiler_params=pltpu.CompilerParams(
            dimension_semantics=("parallel",), vmem_limit_bytes=VMEM_LIMIT),
        name="outproj",
    )(h, mix, *wo, g, wrh, wrl, br)


def _moe_kernel(x_ref, cw_ref, w1_ref, w3_ref, w2_ref, y_ref, xh_ref, xl_ref, *, mode):
    e = pl.program_id(1)

    @pl.when(e == 0)
    def _():
        y_ref[...] = jnp.zeros_like(y_ref)
        xh, xl = _split_bf16(x_ref[...])
        xh_ref[...] = xh
        xl_ref[...] = xl

    cw = cw_ref[...]
    lane = lax.broadcasted_iota(jnp.int32, cw.shape, 1)
    col = jnp.sum(jnp.where(lane == e, cw, 0.0), axis=-1, keepdims=True)
    xh = xh_ref[...]
    w1_hi, w1_lo = _wload((w1_ref,), mode, 0)
    w3_hi, w3_lo = _wload((w3_ref,), mode, 0)
    if w1_lo is None:
        a = _dot(xh, w1_hi)
        b = _dot(xh, w3_hi)
    else:
        xl = xl_ref[...]
        a = _dot3(xh, xl, w1_hi, w1_lo)
        b = _dot3(xh, xl, w3_hi, w3_lo)
    hid = (a * jax.nn.sigmoid(a)) * b
    y_ref[...] += col * _wdot(hid, (w2_ref,), mode, 0)


def _moe(u2, cw, w1, w3, w2, tm):
    n = u2.shape[0]
    mode = 'f32' if w1.dtype == F32 else 'x1'
    return pl.pallas_call(
        functools.partial(_moe_kernel, mode=mode),
        grid=(n // tm, N_EXPERTS),
        in_specs=[
            pl.BlockSpec((tm, D_MODEL), lambda i, e: (i, 0)),
            pl.BlockSpec((tm, LANES), lambda i, e: (i, 0)),
            pl.BlockSpec((1, D_MODEL, D_EXPERT), lambda i, e: (e, 0, 0)),
            pl.BlockSpec((1, D_MODEL, D_EXPERT), lambda i, e: (e, 0, 0)),
            pl.BlockSpec((1, D_EXPERT, D_MODEL), lambda i, e: (e, 0, 0)),
        ],
        out_specs=pl.BlockSpec((tm, D_MODEL), lambda i, e: (i, 0)),
        out_shape=jax.ShapeDtypeStruct((n, D_MODEL), F32),
        scratch_shapes=[pltpu.VMEM((tm, D_MODEL), BF16), pltpu.VMEM((tm, D_MODEL), BF16)],
        compiler_params=pltpu.CompilerParams(
            dimension_semantics=("parallel", "arbitrary"), vmem_limit_bytes=VMEM_LIMIT),
        name="moe",
    )(u2, cw, w1, w3, w2)


def _ple_kernel(h_ref, y_ref, p_ref, g_ref, wg_ref, wp_ref, gf_ref, o_ref, *, final, mode):
    h3 = h_ref[...] + y_ref[...]
    u = _rms(h3, g_ref[...])
    gate = jax.nn.sigmoid(_wdot(u, (wg_ref,), mode))
    h4 = h3 + gate * _wdot(p_ref[...], (wp_ref,), mode)
    if final:
        h4 = _rms(h4, gf_ref[...])
    o_ref[...] = h4


def _ple(h2, y, p, g, wg, wp, gf, tm, final):
    n = h2.shape[0]
    row = lambda i: (i, 0)
    fixed = lambda i: (0, 0)
    mode = 'f32' if wg.dtype == F32 else 'x1'
    return pl.pallas_call(
        functools.partial(_ple_kernel, final=final, mode=mode),
        grid=(n // tm,),
        in_specs=[
            pl.BlockSpec((tm, D_MODEL), row),
            pl.BlockSpec((tm, D_MODEL), row),
            pl.BlockSpec((tm, PLE_DIM), row),
            pl.BlockSpec((1, D_MODEL), fixed),
            pl.BlockSpec((D_MODEL, D_MODEL), fixed),
            pl.BlockSpec((PLE_DIM, D_MODEL), fixed),
            pl.BlockSpec((1, D_MODEL), fixed),
        ],
        out_specs=pl.BlockSpec((tm, D_MODEL), row),
        out_shape=jax.ShapeDtypeStruct((n, D_MODEL), F32),
        compiler_params=pltpu.CompilerParams(
            dimension_semantics=("parallel",), vmem_limit_bytes=VMEM_LIMIT),
        name="ple",
    )(h2, y, p, g, wg, wp, gf)


FOX_LF_LANE = 8
FOX_TB = 256
FOX_TQ = 256


def _split3_bf16(x):
    hi = x.astype(BF16)
    r = x - hi.astype(F32)
    mid = r.astype(BF16)
    lo = (r - mid.astype(F32)).astype(BF16)
    return hi, mid, lo


def _log_sigmoid(x):
    return jnp.minimum(x, 0.0) - jnp.log1p(jnp.exp(-jnp.abs(x)))


def _group_mean(x2, ones_blk, width):
    hi, mid, lo = _split3_bf16(x2)
    return (_dot(hi, ones_blk) + _dot(mid, ones_blk) + _dot(lo, ones_blk)) * (1.0 / width)


def _fox_layout(hp):
    ka = 256 if hp else 128
    a0 = 192 if hp else 64
    nx = 512 + 3 * LANES
    mq = np.zeros((nx, 4 * ka), np.float32)
    mk = np.zeros((nx, 4 * ka), np.float32)
    rq = np.zeros((1, 4 * ka), np.float32)
    rk = np.zeros((1, 4 * ka), np.float32)
    mv = np.zeros((512, 4 * LANES), np.float32)
    for h in range(N_HEADS):
        for d in range(HEAD_DIM):
            src_hi, src_lo = 64 * h + d, 256 + 64 * h + d
            mq[src_hi, h * ka + d] = 1.0
            mk[src_hi, h * ka + d] = 1.0
            if hp:
                mq[src_lo, h * ka + 64 + d] = 1.0
                mq[src_hi, h * ka + 128 + d] = 1.0
                mk[src_hi, h * ka + 64 + d] = 1.0
                mk[src_lo, h * ka + 128 + d] = 1.0
            mv[src_hi, h * LANES + d] = 1.0
            if hp:
                mv[src_lo, h * LANES + 64 + d] = 1.0
        for part in range(3):
            src = 512 + part * LANES + FOX_LF_LANE + h
            mq[src, h * ka + a0 + part] = 1.0
            mk[src, h * ka + a0 + 3 + part] = -1.0
            rq[0, h * ka + a0 + 3 + part] = 1.0
            rk[0, h * ka + a0 + part] = 1.0
    blk = (np.arange(256)[:, None] // HEAD_DIM == np.arange(256)[None, :] // HEAD_DIM).astype(np.float32)
    tri = np.tril(np.ones((FOX_TB, FOX_TB), np.float32))
    as_bf = lambda a: jnp.asarray(a, BF16)
    return dict(ka=ka, mq=as_bf(mq), mk=as_bf(mk), mv=as_bf(mv), rq=jnp.asarray(rq), rk=jnp.asarray(rk),
                blk=as_bf(blk), tri=as_bf(tri))


def _fox_prep_kernel(pc_ref, ps_ref, bf_ref, qn_ref, kn_ref, blk_ref, tri_ref, mq_ref, mk_ref, mv_ref,
                     rq_ref, rk_ref, krow_ref, lf_ref, qa_ref, ka_ref, vv_ref, carry_ref, *, ka):
    @pl.when(pl.program_id(1) == 0)
    def _():
        carry_ref[...] = jnp.zeros_like(carry_ref)

    blk = blk_ref[...]
    q = pc_ref[:, 0:256]
    k = pc_ref[:, 256:512]
    v = pc_ref[:, 512:768]
    qn = q * lax.rsqrt(_group_mean(q * q, blk, HEAD_DIM) + EPS) * qn_ref[...]
    kn = k * lax.rsqrt(_group_mean(k * k, blk, HEAD_DIM) + EPS) * kn_ref[...]
    krow_ref[...] = kn
    lf = _log_sigmoid(ps_ref[...] + bf_ref[...])
    lf_ref[...] = lf
    l_hi, l_mid, l_lo = _split3_bf16(lf)
    tri = tri_ref[...]
    c = _dot(tri, l_hi) + _dot(tri, l_mid) + _dot(tri, l_lo) + carry_ref[...]
    carry_ref[...] = c[FOX_TB - 1:FOX_TB, :]
    c_hi, c_mid, c_lo = _split3_bf16(c)
    q_hi, q_lo = _split_bf16(qn * (HEAD_DIM ** -0.5))
    k_hi, k_lo = _split_bf16(kn)
    xq = jnp.concatenate([q_hi, q_lo, c_hi, c_mid, c_lo], axis=-1)
    xk = jnp.concatenate([k_hi, k_lo, c_hi, c_mid, c_lo], axis=-1)
    qa = (_dot(xq, mq_ref[...]) + rq_ref[...]).astype(BF16)
    kk = (_dot(xk, mk_ref[...]) + rk_ref[...]).astype(BF16)
    v_hi, v_lo = _split_bf16(v)
    vv = _dot(jnp.concatenate([v_hi, v_lo], axis=-1), mv_ref[...]).astype(BF16)
    for h in range(N_HEADS):
        qa_ref[0, h] = qa[:, h * ka:(h + 1) * ka]
        ka_ref[0, h] = kk[:, h * ka:(h + 1) * ka]
        vv_ref[0, h] = vv[:, h * LANES:(h + 1) * LANES]


def _fox_prep(pm, ps, bf_row, qn_row, kn_row, lay, bsz, t):
    ka = lay['ka']
    nt = t // FOX_TB
    fixed = lambda b, i: (0, 0)
    rows = lambda b, i: (b * nt + i, 0)
    hm = lambda b, i: (b, 0, i, 0)
    return pl.pallas_call(
        functools.partial(_fox_prep_kernel, ka=ka),
        grid=(bsz, nt),
        in_specs=[
            pl.BlockSpec((FOX_TB, 1024), lambda b, i: (b * nt + i, 2)),
            pl.BlockSpec((FOX_TB, LANES), rows),
            pl.BlockSpec((1, LANES), fixed),
            pl.BlockSpec((1, 256), fixed),
            pl.BlockSpec((1, 256), fixed),
            pl.BlockSpec((256, 256), fixed),
            pl.BlockSpec((FOX_TB, FOX_TB), fixed),
            pl.BlockSpec(lay['mq'].shape, fixed),
            pl.BlockSpec(lay['mk'].shape, fixed),
            pl.BlockSpec(lay['mv'].shape, fixed),
            pl.BlockSpec((1, 4 * ka), fixed),
            pl.BlockSpec((1, 4 * ka), fixed),
        ],
        out_specs=[
            pl.BlockSpec((FOX_TB, 256), rows),
            pl.BlockSpec((FOX_TB, LANES), rows),
            pl.BlockSpec((1, N_HEADS, FOX_TB, ka), hm),
            pl.BlockSpec((1, N_HEADS, FOX_TB, ka), hm),
            pl.BlockSpec((1, N_HEADS, FOX_TB, LANES), hm),
        ],
        out_shape=[
            jax.ShapeDtypeStruct((bsz * t, 256), F32),
            jax.ShapeDtypeStruct((bsz * t, LANES), F32),
            jax.ShapeDtypeStruct((bsz, N_HEADS, t, ka), BF16),
            jax.ShapeDtypeStruct((bsz, N_HEADS, t, ka), BF16),
            jax.ShapeDtypeStruct((bsz, N_HEADS, t, LANES), BF16),
        ],
        scratch_shapes=[pltpu.VMEM((1, LANES), F32)],
        compiler_params=pltpu.CompilerParams(
            dimension_semantics=("parallel", "arbitrary"), vmem_limit_bytes=VMEM_LIMIT),
        name="fox_prep",
    )(pm, ps, bf_row, qn_row, kn_row, lay['blk'], lay['tri'], lay['mq'], lay['mk'], lay['mv'],
      lay['rq'], lay['rk'])


def _fox_flash_kernel(qa_ref, ka_ref, vv_ref, g_ref, on_ref, o_ref, m_s, l_s, acc_s, *, hp):
    i = pl.program_id(1)
    tq = FOX_TQ
    row = lax.broadcasted_iota(jnp.int32, (tq, tq), 0)
    col = lax.broadcasted_iota(jnp.int32, (tq, tq), 1)
    m_s[...] = jnp.full(m_s.shape, NEG_BIG, F32)
    l_s[...] = jnp.zeros(l_s.shape, F32)
    acc_s[...] = jnp.zeros(acc_s.shape, F32)

    def tile(j, masked):
        start = pl.multiple_of(j * tq, tq)
        for h in range(N_HEADS):
            kt = ka_ref[0, h, pl.ds(start, tq), :]
            s = lax.dot_general(qa_ref[0, h], kt, (((1,), (1,)), ((), ())), preferred_element_type=F32)
            if masked:
                s = jnp.where(row >= col, s, NEG_BIG)
            m = m_s[h]
            m_new = jnp.maximum(m, jnp.max(s, axis=-1, keepdims=True))
            alpha = jnp.exp(m - m_new)
            p = jnp.exp(s - m_new)
            l_s[h] = alpha * l_s[h] + jnp.sum(p, axis=-1, keepdims=True)
            m_s[h] = m_new
            vt = vv_ref[0, h, pl.ds(start, tq), :]
            if hp:
                p_hi, p_lo = _split_bf16(p)
                pv = _dot(p_hi, vt) + _dot(p_lo, vt)
            else:
                pv = _dot(p.astype(BF16), vt)
            acc_s[h] = alpha * acc_s[h] + pv

    def body(j, carry):
        tile(j, False)
        return carry

    lax.fori_loop(0, i, body, 0)
    tile(i, True)
    outs = []
    for h in range(N_HEADS):
        acc = acc_s[h]
        o = (acc[:, :HEAD_DIM] + acc[:, HEAD_DIM:]) / l_s[h]
        o = o * lax.rsqrt(jnp.mean(o * o, axis=-1, keepdims=True) + EPS)
        sl = slice(h * HEAD_DIM, (h + 1) * HEAD_DIM)
        outs.append(o * on_ref[:, sl] * jax.nn.sigmoid(g_ref[:, sl]))
    o_ref[...] = jnp.concatenate(outs, axis=-1)


def _fox_flash(qa, ka, vv, pm, on_row, bsz, t, hp):
    kad = qa.shape[-1]
    nq = t // FOX_TQ
    whole = lambda b, i: (b, 0, 0, 0)
    return pl.pallas_call(
        functools.partial(_fox_flash_kernel, hp=hp),
        grid=(bsz, nq),
        in_specs=[
            pl.BlockSpec((1, N_HEADS, FOX_TQ, kad), lambda b, i: (b, 0, i, 0)),
            pl.BlockSpec((1, N_HEADS, t, kad), whole, pipeline_mode=pl.Buffered(1)),
            pl.BlockSpec((1, N_HEADS, t, LANES), whole, pipeline_mode=pl.Buffered(1)),
            pl.BlockSpec((FOX_TQ, 256), lambda b, i: (b * nq + i, 11)),
            pl.BlockSpec((1, 256), lambda b, i: (0, 0)),
        ],
        out_specs=pl.BlockSpec((FOX_TQ, 256), lambda b, i: (b * nq + i, 0)),
        out_shape=jax.ShapeDtypeStruct((bsz * t, 256), F32),
        scratch_shapes=[pltpu.VMEM((N_HEADS, FOX_TQ, 1), F32), pltpu.VMEM((N_HEADS, FOX_TQ, 1), F32),
                        pltpu.VMEM((N_HEADS, FOX_TQ, LANES), F32)],
        compiler_params=pltpu.CompilerParams(
            dimension_semantics=("parallel", "arbitrary"), vmem_limit_bytes=56 * 1024 * 1024),
        name="fox_flash",
    )(qa, ka, vv, pm, on_row)


def _fox_prompt(pm, ps, b_f, q_norm, k_norm, out_norm, bsz, t, hp):
    lay = _fox_layout(hp)
    bf_row = jnp.zeros((1, LANES), F32).at[0, FOX_LF_LANE:FOX_LF_LANE + N_HEADS].set(b_f)
    tile4 = lambda g: jnp.tile(g, N_HEADS).reshape(1, 256)
    krow, lf, qa, ka, vv = _fox_prep(pm, ps, bf_row, tile4(q_norm), tile4(k_norm), lay, bsz, t)
    o = _fox_flash(qa, ka, vv, pm, out_norm.reshape(1, 256), bsz, t, hp)
    return o, krow, lf[:, FOX_LF_LANE:FOX_LF_LANE + N_HEADS]


FOX_PAGES_PER_STEP = 8
_UST = np.tril(np.ones((PAGE_SIZE, PAGE_SIZE), np.float32), k=-1)


def _row_to_col(row, eye):
    return jnp.sum(jnp.where(eye, row, 0.0), axis=-1, keepdims=True)


def _col_to_row(col, eye):
    return jnp.sum(jnp.where(eye, col, 0.0), axis=0, keepdims=True)


def _per_head_rows(x8, rows_per_head, width):
    return jnp.concatenate([jnp.broadcast_to(x8[h:h + 1, :], (rows_per_head, width)) for h in range(N_HEADS)],
                           axis=0)


def _fox_dec_kernel(pt_ref, pc_ref, ps_ref, bf_ref, qn_ref, kn_ref, on_ref, blk_ref, ust_ref, *refs):
    npp = FOX_PAGES_PER_STEP
    k_refs, v_refs, lf_refs = refs[0:npp], refs[npp:2 * npp], refs[2 * npp:3 * npp]
    o_ref, krow_ref, lfrow_ref, qb_s, m_s, l_s, acc_s, carry_s = refs[3 * npp:]
    j = pl.program_id(1)
    hrow = lax.broadcasted_iota(jnp.int32, (8, 256), 0)
    lane = lax.broadcasted_iota(jnp.int32, (8, 256), 1)
    hm = (lane >= hrow * HEAD_DIM) & (lane < (hrow + 1) * HEAD_DIM)
    eye = lax.broadcasted_iota(jnp.int32, (256, 256), 0) == lax.broadcasted_iota(jnp.int32, (256, 256), 1)
    r8 = lax.broadcasted_iota(jnp.int32, (8, LANES), 0)
    l8 = lax.broadcasted_iota(jnp.int32, (8, LANES), 1)
    blk = blk_ref[...]

    @pl.when(j == 0)
    def _():
        q = pc_ref[:, 0:256]
        k = pc_ref[:, 256:512]
        v = pc_ref[:, 512:768]
        qn = q * lax.rsqrt(_group_mean(q * q, blk, HEAD_DIM) + EPS) * qn_ref[...]
        kn = k * lax.rsqrt(_group_mean(k * k, blk, HEAD_DIM) + EPS) * kn_ref[...]
        krow_ref[...] = kn
        lf = _log_sigmoid(ps_ref[...] + bf_ref[...])
        lfrow_ref[...] = lf
        q_row = qn[0:1, :] * (HEAD_DIM ** -0.5)
        qb_s[...] = jnp.broadcast_to(_row_to_col(q_row, eye), (256, LANES))
        m_s[...] = jnp.sum(jnp.where(hm, q_row * kn[0:1, :], 0.0), axis=-1, keepdims=True)
        l_s[...] = jnp.ones_like(l_s)
        lane_full = lax.broadcasted_iota(jnp.int32, (256, LANES), 1)
        acc_s[...] = jnp.where(lane_full == 0, _row_to_col(v[0:1, :], eye), 0.0)
        carry_s[...] = jnp.sum(jnp.where(l8 == r8 + FOX_LF_LANE, lf[0:1, :], 0.0), axis=-1, keepdims=True)

    qb = qb_s[...]
    m, l, carry = m_s[...], l_s[...], carry_s[...]
    ust = ust_ref[...]
    scores = []
    for r in reversed(range(npp)):
        prod = k_refs[r][...] * qb
        s = jnp.zeros((8, LANES), F32)
        for h in range(N_HEADS):
            s_h = jnp.sum(prod[h * HEAD_DIM:(h + 1) * HEAD_DIM, :], axis=0, keepdims=True)
            s = s + jnp.where(r8 == h, s_h, 0.0)
        lfp = lf_refs[r][...]
        scores.append(s + _mm_exact_r(lfp, ust) + carry)
        carry = carry + jnp.sum(lfp, axis=-1, keepdims=True)
    m_new = m
    for s in scores:
        m_new = jnp.maximum(m_new, jnp.max(s, axis=-1, keepdims=True))
    alpha = jnp.exp(m - m_new)
    l = alpha * l
    acc = acc_s[...] * _per_head_rows(alpha, HEAD_DIM, 1)
    for idx, r in enumerate(reversed(range(npp))):
        p = jnp.exp(scores[idx] - m_new)
        l = l + jnp.sum(p, axis=-1, keepdims=True)
        acc = acc + _per_head_rows(p, HEAD_DIM, LANES) * v_refs[r][...]
    m_s[...], l_s[...], acc_s[...], carry_s[...] = m_new, l, acc, carry

    @pl.when(j == pl.num_programs(1) - 1)
    def _():
        o = _col_to_row(jnp.sum(acc, axis=-1, keepdims=True), eye)
        l_row = jnp.sum(jnp.where(hm, l, 0.0), axis=0, keepdims=True)
        o = jnp.broadcast_to(o / l_row, (8, 256))
        o = o * lax.rsqrt(_group_mean(o * o, blk, HEAD_DIM) + EPS)
        o_ref[...] = o * on_ref[...] * jax.nn.sigmoid(pc_ref[:, 768:1024])


def _fox_sample(pmr, psr, cache_k, cache_v, cache_logf, page_table, l, b_f, q_norm, k_norm, out_norm):
    bsz, n_pages = page_table.shape
    npp = FOX_PAGES_PER_STEP
    nsteps = n_pages // npp
    depth, n_phys = cache_k.shape[0], cache_k.shape[1]
    ck = jnp.transpose(cache_k, (0, 1, 3, 4, 2)).reshape(depth, n_phys, 256, PAGE_SIZE)
    cv = jnp.transpose(cache_v, (0, 1, 3, 4, 2)).reshape(depth, n_phys, 256, PAGE_SIZE)
    clf = jnp.pad(jnp.swapaxes(cache_logf, 2, 3), ((0, 0), (0, 0), (0, 8 - N_HEADS), (0, 0)))
    bf_row = jnp.zeros((1, LANES), F32).at[0, FOX_LF_LANE:FOX_LF_LANE + N_HEADS].set(b_f)
    tile4 = lambda g: jnp.tile(g, N_HEADS).reshape(1, 256)
    fixed = lambda b, j, pt: (0, 0)

    def page_spec(r, width):
        return pl.BlockSpec((None, None, width[0], width[1]),
                            lambda b, j, pt, r=r: (l, pt[b, (nsteps - 1 - j) * npp + r], 0, 0))

    in_specs = [
        pl.BlockSpec((8, 1024), lambda b, j, pt: (b, 2)),
        pl.BlockSpec((8, LANES), lambda b, j, pt: (b, 0)),
        pl.BlockSpec((1, LANES), fixed),
        pl.BlockSpec((1, 256), fixed),
        pl.BlockSpec((1, 256), fixed),
        pl.BlockSpec((1, 256), fixed),
        pl.BlockSpec((256, 256), fixed),
        pl.BlockSpec((PAGE_SIZE, PAGE_SIZE), fixed),
    ]
    in_specs += [page_spec(r, (256, PAGE_SIZE)) for r in range(npp)]
    in_specs += [page_spec(r, (256, PAGE_SIZE)) for r in range(npp)]
    in_specs += [page_spec(r, (8, PAGE_SIZE)) for r in range(npp)]
    rows = lambda b, j, pt: (b, 0)
    o, krow, lfrow = pl.pallas_call(
        _fox_dec_kernel,
        grid_spec=pltpu.PrefetchScalarGridSpec(
            num_scalar_prefetch=1,
            grid=(bsz, nsteps),
            in_specs=in_specs,
            out_specs=[pl.BlockSpec((8, 256), rows), pl.BlockSpec((8, 256), rows),
                       pl.BlockSpec((8, LANES), rows)],
            scratch_shapes=[pltpu.VMEM((256, LANES), F32), pltpu.VMEM((8, 1), F32), pltpu.VMEM((8, 1), F32),
                            pltpu.VMEM((256, LANES), F32), pltpu.VMEM((8, 1), F32)],
        ),
        out_shape=[jax.ShapeDtypeStruct((bsz * 8, 256), F32), jax.ShapeDtypeStruct((bsz * 8, 256), F32),
                   jax.ShapeDtypeStruct((bsz * 8, LANES), F32)],
        compiler_params=pltpu.CompilerParams(
            dimension_semantics=("parallel", "arbitrary"), vmem_limit_bytes=VMEM_LIMIT),
        name="fox_sample",
    )(page_table, pmr, psr, bf_row, tile4(q_norm), tile4(k_norm), out_norm.reshape(1, 256),
      jnp.asarray(_BLK256, BF16), jnp.asarray(_UST, BF16), *([ck] * npp), *([cv] * npp), *([clf] * npp))
    first = lambda a: a.reshape(bsz, 8, -1)[:, 0]
    return first(o), first(krow), first(lfrow)[:, FOX_LF_LANE:FOX_LF_LANE + N_HEADS]


CHUNK = 128
ROW0 = 8


def _mm(a, b, hp):
    if hp:
        a_hi, a_lo = _split_bf16(a)
        b_hi, b_lo = _split_bf16(b)
        return _dot3(a_hi, a_lo, b_hi, b_lo)
    return _dot(a.astype(BF16), b.astype(BF16))


def _mm_nt(a, b, hp):
    dn = (((1,), (1,)), ((), ()))
    if hp:
        a_hi, a_lo = _split_bf16(a)
        b_hi, b_lo = _split_bf16(b)
        a3 = jnp.concatenate([a_hi, a_lo, a_hi], axis=-1)
        b3 = jnp.concatenate([b_hi, b_hi, b_lo], axis=-1)
        return lax.dot_general(a3, b3, dn, preferred_element_type=F32)
    return lax.dot_general(a.astype(BF16), b.astype(BF16), dn, preferred_element_type=F32)


def _mm_exact(sel, x):
    hi, mid, lo = _split3_bf16(x)
    return _dot(sel, hi) + _dot(sel, mid) + _dot(sel, lo)


def _mm_exact_r(x, sel):
    hi, mid, lo = _split3_bf16(x)
    return _dot(hi, sel) + _dot(mid, sel) + _dot(lo, sel)


def _softplus(x):
    return jnp.maximum(x, 0.0) + jnp.log1p(jnp.exp(-jnp.abs(x)))


def _silu(x):
    return x * jax.nn.sigmoid(x)


def _stage_rows(buf, blk_ref, lo, hi, prev_ref, rows_in, first):
    @pl.when(first)
    def _():
        buf[ROW0 - 3:ROW0, :] = prev_ref[0]
        if rows_in < CHUNK:
            buf[ROW0 + rows_in:ROW0 + CHUNK, :] = jnp.zeros((CHUNK - rows_in, hi - lo), F32)

    @pl.when(jnp.logical_not(first))
    def _():
        buf[ROW0 - 3:ROW0, :] = buf[ROW0 + CHUNK - 3:ROW0 + CHUNK, :]

    buf[ROW0:ROW0 + rows_in, :] = blk_ref[:, lo:hi]


def _conv4(buf, w_ref):
    acc = w_ref[0:1, :] * buf[ROW0 - 3:ROW0 - 3 + CHUNK, :]
    for j in range(1, CONV_WIDTH):
        acc = acc + w_ref[j:j + 1, :] * buf[ROW0 - 3 + j:ROW0 - 3 + j + CHUNK, :]
    return acc


def _pad_rows(x, rows_in):
    if rows_in == CHUNK:
        return x
    return jnp.concatenate([x, jnp.zeros((CHUNK - rows_in, x.shape[1]), x.dtype)], axis=0)


def _head_expand(first_lane, width):
    e = np.zeros((LANES, N_HEADS * width), np.float32)
    for h in range(N_HEADS):
        e[first_lane + h, h * width:(h + 1) * width] = 1.0
    return jnp.asarray(e, BF16)


_TRI = np.tril(np.ones((CHUNK, CHUNK), np.float32))


SSD_DT_LANE = 12


def _ssd_kernel(pd_ref, ps_ref, cprev_ref, s0_ref, w_ref, cb_ref, dtb_ref, alog_ref, dsk_ref, gn_ref,
                e4_ref, tri_ref, y_ref, sfin_ref, buf, s_scr, *, rows_in, t_valid, hp):
    i = pl.program_id(1)
    first = i == 0

    @pl.when(first)
    def _():
        s_scr[...] = s0_ref[0]

    _stage_rows(buf, pd_ref, 256, 1024, cprev_ref, rows_in, first)
    xbc = _silu(_conv4(buf, w_ref) + cb_ref[...])
    xs = xbc[:, 0:256]
    bm = xbc[:, 256:512]
    cm = xbc[:, 512:768]
    z = _pad_rows(pd_ref[:, 0:256], rows_in)
    pre = _mm_exact_r(_pad_rows(ps_ref[...], rows_in), e4_ref[...])
    dt = _softplus(pre + dtb_ref[...])
    tpos = i * CHUNK + lax.broadcasted_iota(jnp.int32, (CHUNK, 1), 0)
    dt = jnp.where(tpos < t_valid, dt, 0.0)
    la = -jnp.exp(alog_ref[...]) * dt
    b = _mm_exact(tri_ref[...], la)
    row = lax.broadcasted_iota(jnp.int32, (CHUNK, CHUNK), 0)
    col = lax.broadcasted_iota(jnp.int32, (CHUNK, CHUNK), 1)
    causal = row >= col
    lane = lax.broadcasted_iota(jnp.int32, (1, 256), 1)
    s_prev = s_scr[...]
    cb = [_mm_nt(cm[:, g * 128:(g + 1) * 128], bm[:, g * 128:(g + 1) * 128], hp) for g in range(D_GROUPS)]
    y = jnp.zeros((CHUNK, 256), F32)
    s_new = jnp.zeros((D_STATE, 256), F32)
    for h in range(N_HEADS):
        g = h // (N_HEADS // D_GROUPS)
        bh = b[:, h * 128:(h + 1) * 128]
        dth = dt[:, h * 128:(h + 1) * 128]
        hmask = (lane >= h * HEAD_DIM) & (lane < (h + 1) * HEAD_DIM)
        dt2 = jnp.concatenate([dth, dth], axis=-1)
        xdt = jnp.where(hmask, xs * dt2, 0.0)
        rel = jnp.where(causal, jnp.exp(jnp.where(causal, bh - bh.T, 0.0)), 0.0)
        y = y + _mm(cb[g] * rel, xdt, hp)
        eb = jnp.exp(bh)
        y = y + _mm(cm[:, g * 128:(g + 1) * 128] * eb, jnp.where(hmask, s_prev, 0.0), hp)
        b_last = bh[CHUNK - 1:CHUNK, :]
        kdec = bm[:, g * 128:(g + 1) * 128] * jnp.exp(b_last - bh)
        s_new = s_new + _mm(kdec.T, xdt, hp)
        a2 = jnp.exp(jnp.concatenate([b_last, b_last], axis=-1))
        s_new = s_new + jnp.where(hmask, a2 * s_prev, 0.0)
    s_scr[...] = s_new
    sfin_ref[0] = s_new
    y = (y + xs * dsk_ref[...]) * _silu(z)
    outs = []
    for g in range(D_GROUPS):
        yg = y[:, g * 128:(g + 1) * 128]
        outs.append(yg * lax.rsqrt(jnp.mean(yg * yg, axis=-1, keepdims=True) + EPS))
    yn = jnp.concatenate(outs, axis=-1) * gn_ref[...]
    y_ref[...] = yn[0:rows_in, :]


def _ssd(pm, ps, conv_prev, s0, conv_w, conv_b, a_log, dt_bias, d_skip, norm_g, bsz, t_valid, rows_in, hp):
    nt = pm.shape[0] // (bsz * rows_in)
    fixed = lambda b, i: (0, 0)
    rows = lambda b, i: (b * nt + i, 0)
    rep128 = lambda v: jnp.repeat(v, 128).reshape(1, 512)
    s0l = jnp.transpose(s0, (0, 2, 1, 3)).reshape(bsz, D_STATE, 256)
    y, sfin = pl.pallas_call(
        functools.partial(_ssd_kernel, rows_in=rows_in, t_valid=t_valid, hp=hp),
        grid=(bsz, nt),
        in_specs=[
            pl.BlockSpec((rows_in, 1024), lambda b, i: (b * nt + i, 3)),
            pl.BlockSpec((rows_in, LANES), rows),
            pl.BlockSpec((1, 3, D_CONV_CH), lambda b, i: (b, 0, 0)),
            pl.BlockSpec((1, D_STATE, 256), lambda b, i: (b, 0, 0)),
            pl.BlockSpec((CONV_WIDTH, D_CONV_CH), fixed),
            pl.BlockSpec((1, D_CONV_CH), fixed),
            pl.BlockSpec((1, 512), fixed),
            pl.BlockSpec((1, 512), fixed),
            pl.BlockSpec((1, 256), fixed),
            pl.BlockSpec((1, 256), fixed),
            pl.BlockSpec((LANES, 512), fixed),
            pl.BlockSpec((CHUNK, CHUNK), fixed),
        ],
        out_specs=[
            pl.BlockSpec((rows_in, 256), rows),
            pl.BlockSpec((1, D_STATE, 256), lambda b, i: (b, 0, 0)),
        ],
        out_shape=[
            jax.ShapeDtypeStruct((pm.shape[0], 256), F32),
            jax.ShapeDtypeStruct((bsz, D_STATE, 256), F32),
        ],
        scratch_shapes=[pltpu.VMEM((ROW0 + CHUNK, D_CONV_CH), F32), pltpu.VMEM((D_STATE, 256), F32)],
        compiler_params=pltpu.CompilerParams(
            dimension_semantics=("parallel", "arbitrary"), vmem_limit_bytes=VMEM_LIMIT),
        name="ssd",
    )(pm, ps, conv_prev, s0l, conv_w, conv_b.reshape(1, -1), rep128(dt_bias), rep128(a_log),
      jnp.repeat(d_skip, HEAD_DIM).reshape(1, 256), norm_g.reshape(1, 256),
      _head_expand(SSD_DT_LANE, 128), jnp.asarray(_TRI, BF16))
    return y, jnp.transpose(sfin.reshape(bsz, D_STATE, N_HEADS, HEAD_DIM), (0, 2, 1, 3))


SUB = CHUNK_A
_SUB_ID = np.arange(CHUNK) // SUB
_SAME_SUB = (_SUB_ID[:, None] == _SUB_ID[None, :]).astype(np.float32)
_BLK256 = (np.arange(256)[:, None] // HEAD_DIM == np.arange(256)[None, :] // HEAD_DIM).astype(np.float32)


def _mm_tn(a, b, hp):
    dn = (((0,), (0,)), ((), ()))
    if hp:
        a_hi, a_lo = _split_bf16(a)
        b_hi, b_lo = _split_bf16(b)
        a3 = jnp.concatenate([a_hi, a_lo, a_hi], axis=0)
        b3 = jnp.concatenate([b_hi, b_hi, b_lo], axis=0)
        return lax.dot_general(a3, b3, dn, preferred_element_type=F32)
    return lax.dot_general(a.astype(BF16), b.astype(BF16), dn, preferred_element_type=F32)


def _group_sum(x, ones_blk, hp):
    if hp:
        return _mm_exact_r(x, ones_blk)
    return _dot(x.astype(BF16), ones_blk)


def _hgrn_kernel(pa_ref, s0_ref, c1_ref, c2_ref, oml_ref, gn_ref, blk_ref, t16_ref, l16_ref,
                 o_ref, sfin_ref, kbuf, vbuf, lbuf, st_scr, *, rows_in, t_valid, hp):
    i = pl.program_id(1)

    @pl.when(i == 0)
    def _():
        st_scr[...] = s0_ref[0]
        zeros = jnp.zeros((SUB, 256), F32)
        kbuf[0:SUB, :] = zeros
        vbuf[0:SUB, :] = zeros
        lbuf[0:SUB, :] = zeros

    q = _pad_rows(pa_ref[:, 0:256], rows_in) * (HEAD_DIM ** -0.5)
    zf = _pad_rows(pa_ref[:, 256:512], rows_in)
    v = _pad_rows(pa_ref[:, 512:768], rows_in)
    g = _pad_rows(pa_ref[:, 768:1024], rows_in)
    la = c1_ref[...]
    lb = c2_ref[...] + _log_sigmoid(zf)
    lf = jnp.maximum(la, lb) + jnp.log1p(jnp.exp(-jnp.abs(la - lb)))
    k = oml_ref[...] * jax.nn.sigmoid(-zf)
    tpos = i * CHUNK + lax.broadcasted_iota(jnp.int32, (CHUNK, 1), 0)
    valid = tpos < t_valid
    lf = jnp.where(valid, lf, 0.0)
    k = jnp.where(valid, k, 0.0)
    v = jnp.where(valid, v, 0.0)
    kbuf[SUB:SUB + CHUNK, :] = k
    vbuf[SUB:SUB + CHUNK, :] = v
    lbuf[SUB:SUB + CHUNK, :] = lf

    blk = blk_ref[...]
    sub = lax.broadcasted_iota(jnp.int32, (CHUNK, 1), 0) % SUB
    o = _group_sum(q * k, blk, hp) * v
    bd = jnp.zeros((CHUNK, 256), F32)
    for d in range(1, SUB):
        bd = bd + lbuf[SUB - d + 1:SUB - d + 1 + CHUNK, :]
        m = sub >= d
        ks = kbuf[SUB - d:SUB - d + CHUNK, :]
        tmp = jnp.where(m, q * ks * jnp.exp(jnp.where(m, bd, 0.0)), 0.0)
        o = o + _group_sum(tmp, blk, hp) * vbuf[SUB - d:SUB - d + CHUNK, :]

    b = _mm_exact(t16_ref[...], lf)
    bl = _mm_exact(l16_ref[...], lf)
    qe = q * jnp.exp(b)
    kd = k * jnp.exp(bl - b)
    st = st_scr[...]
    bdmask = blk > 0
    o_rows = []
    for n in range(CHUNK // SUB):
        r = slice(n * SUB, (n + 1) * SUB)
        o_rows.append(_mm_nt(qe[r], st, hp))
        ds = _mm_tn(v[r], kd[r], hp)
        st = st * jnp.exp(bl[n * SUB:n * SUB + 1, :]) + jnp.where(bdmask, ds, 0.0)
    st_scr[...] = st
    sfin_ref[0] = st
    o = o + jnp.concatenate(o_rows, axis=0)
    o = o * lax.rsqrt(_group_mean(o * o, blk, HEAD_DIM) + EPS) * gn_ref[...] * _silu(g)
    o_ref[...] = o[0:rows_in, :]


def _hgrn(pm, lb, norm_g, s0, bsz, t_valid, rows_in, hp):
    nt = pm.shape[0] // (bsz * rows_in)
    fixed = lambda b, i: (0, 0)
    rows = lambda b, i: (b * nt + i, 0)
    lb = jnp.clip(lb, 0.0, LB_CEIL)
    c1 = jnp.log(jnp.maximum(lb, LB_FLOOR)).reshape(1, 256)
    c2 = jnp.log1p(-lb).reshape(1, 256)
    oml = (1.0 - lb).reshape(1, 256)
    eye = jnp.eye(N_HEADS, dtype=bool)[None, :, None, :, None]
    st0 = jnp.where(eye, jnp.swapaxes(s0, 2, 3)[:, :, :, None, :], 0.0).reshape(bsz, 256, 256)
    o, sfin = pl.pallas_call(
        functools.partial(_hgrn_kernel, rows_in=rows_in, t_valid=t_valid, hp=hp),
        grid=(bsz, nt),
        in_specs=[
            pl.BlockSpec((rows_in, 1024), lambda b, i: (b * nt + i, 0)),
            pl.BlockSpec((1, 256, 256), lambda b, i: (b, 0, 0)),
            pl.BlockSpec((1, 256), fixed),
            pl.BlockSpec((1, 256), fixed),
            pl.BlockSpec((1, 256), fixed),
            pl.BlockSpec((1, 256), fixed),
            pl.BlockSpec((256, 256), fixed),
            pl.BlockSpec((CHUNK, CHUNK), fixed),
            pl.BlockSpec((CHUNK, CHUNK), fixed),
        ],
        out_specs=[
            pl.BlockSpec((rows_in, 256), rows),
            pl.BlockSpec((1, 256, 256), lambda b, i: (b, 0, 0)),
        ],
        out_shape=[
            jax.ShapeDtypeStruct((pm.shape[0], 256), F32),
            jax.ShapeDtypeStruct((bsz, 256, 256), F32),
        ],
        scratch_shapes=[pltpu.VMEM((SUB + CHUNK, 256), F32)] * 3 + [pltpu.VMEM((256, 256), F32)],
        compiler_params=pltpu.CompilerParams(
            dimension_semantics=("parallel", "arbitrary"), vmem_limit_bytes=VMEM_LIMIT),
        name="hgrn",
    )(pm, st0, c1, c2, oml, norm_g.reshape(1, 256), jnp.asarray(_BLK256, BF16),
      jnp.asarray(_TRI * _SAME_SUB, BF16), jnp.asarray(_SAME_SUB, BF16))
    sf = sfin.reshape(bsz, N_HEADS, HEAD_DIM, N_HEADS, HEAD_DIM)
    sf = jnp.stack([sf[:, h, :, h, :] for h in range(N_HEADS)], axis=1)
    return o, jnp.swapaxes(sf, 2, 3)


GDN_BETA_LANE = 0
GDN_DT_LANE = 4


def _block_diag_state(s0, bsz):
    eye = jnp.eye(N_HEADS, dtype=bool)[None, :, None, :, None]
    return jnp.where(eye, jnp.swapaxes(s0, 2, 3)[:, :, :, None, :], 0.0).reshape(bsz, 256, 256)


def _unblock_diag_state(st, bsz):
    sf = st.reshape(bsz, N_HEADS, HEAD_DIM, N_HEADS, HEAD_DIM)
    sf = jnp.stack([sf[:, h, :, h, :] for h in range(N_HEADS)], axis=1)
    return jnp.swapaxes(sf, 2, 3)


def _gdn_kernel(pb_ref, ps_ref, cprev_ref, s0_ref, w_ref, dtb_ref, alog_ref, dtb64_ref, alog64_ref, gn_ref,
                blk_ref, tri_ref, eb128_ref, ed128_ref, eb64_ref, ed64_ref,
                o_ref, sfin_ref, buf, st_scr, *, rows_in, t_valid, hp):
    i = pl.program_id(1)
    first = i == 0

    @pl.when(first)
    def _():
        st_scr[...] = s0_ref[0]

    _stage_rows(buf, pb_ref, 0, 768, cprev_ref, rows_in, first)
    qkv = _silu(_conv4(buf, w_ref))
    blk = blk_ref[...]
    q = qkv[:, 0:256]
    k = qkv[:, 256:512]
    v = qkv[:, 512:768]
    q = q * lax.rsqrt(_group_mean(q * q, blk, 1) + EPS) * (HEAD_DIM ** -0.5)
    k = k * lax.rsqrt(_group_mean(k * k, blk, 1) + EPS)
    gate = _pad_rows(pb_ref[:, 768:1024], rows_in)
    ps = _pad_rows(ps_ref[...], rows_in)
    tpos = i * CHUNK + lax.broadcasted_iota(jnp.int32, (CHUNK, 1), 0)
    valid = tpos < t_valid
    tri = tri_ref[...]
    beta128 = jnp.where(valid, jax.nn.sigmoid(_mm_exact_r(ps, eb128_ref[...])), 0.0)
    la128 = jnp.where(valid, -jnp.exp(alog_ref[...]) * _softplus(_mm_exact_r(ps, ed128_ref[...]) + dtb_ref[...]), 0.0)
    b128 = _mm_exact(tri, la128)
    beta64 = jnp.where(valid, jax.nn.sigmoid(_mm_exact_r(ps, eb64_ref[...])), 0.0)
    la64 = jnp.where(valid, -jnp.exp(alog64_ref[...]) * _softplus(_mm_exact_r(ps, ed64_ref[...]) + dtb64_ref[...]), 0.0)
    b64 = _mm_exact(tri, la64)
    eb64 = jnp.exp(b64)
    b_last64 = b64[CHUNK - 1:CHUNK, :]
    kb = k * beta64
    rv = v * beta64
    rk = kb * eb64
    q_dec = q * eb64
    k_dec = k * jnp.exp(b_last64 - b64)

    row = lax.broadcasted_iota(jnp.int32, (CHUNK, CHUNK), 0)
    col = lax.broadcasted_iota(jnp.int32, (CHUNK, CHUNK), 1)
    causal = row >= col
    strict = row > col
    lane = lax.broadcasted_iota(jnp.int32, (1, 256), 1)
    us, ws, atts = [], [], []
    for h in range(N_HEADS):
        hs = slice(h * HEAD_DIM, (h + 1) * HEAD_DIM)
        hmask = (lane >= h * HEAD_DIM) & (lane < (h + 1) * HEAD_DIM)
        bh = b128[:, h * 128:(h + 1) * 128]
        decay = jnp.where(causal, jnp.exp(jnp.where(causal, bh - bh.T, 0.0)), 0.0)
        k_h = jnp.where(hmask, k, 0.0)
        a = -jnp.where(strict, _mm_nt(jnp.where(hmask, kb, 0.0), k_h, True) * decay, 0.0)
        atts.append(_mm_nt(jnp.where(hmask, q, 0.0), k_h, hp) * decay)
        x = jnp.concatenate([rv[:, hs], rk[:, hs]], axis=-1)
        x = x + _mm(a, x, True)
        for _ in range(6):
            a = _mm(a, a, True)
            x = x + _mm(a, x, True)
        us.append(x[:, :HEAD_DIM])
        ws.append(x[:, HEAD_DIM:])
    u_all = jnp.concatenate(us, axis=-1)
    w_all = jnp.concatenate(ws, axis=-1)
    st = st_scr[...]
    v_new = u_all - _mm_nt(w_all, st, hp)
    o = _mm_nt(q_dec, st, hp)
    for h in range(N_HEADS):
        hmask = (lane >= h * HEAD_DIM) & (lane < (h + 1) * HEAD_DIM)
        o = o + _mm(atts[h], jnp.where(hmask, v_new, 0.0), hp)
    st = st * jnp.exp(b_last64) + jnp.where(blk > 0, _mm_tn(v_new, k_dec, hp), 0.0)
    st_scr[...] = st
    sfin_ref[0] = st
    o = o * lax.rsqrt(_group_mean(o * o, blk, HEAD_DIM) + EPS) * gn_ref[...] * _silu(gate)
    o_ref[...] = o[0:rows_in, :]


def _gdn(pm, ps, conv_prev, s0, conv_w, a_log, dt_bias, norm_g, bsz, t_valid, rows_in, hp):
    nt = pm.shape[0] // (bsz * rows_in)
    fixed = lambda b, i: (0, 0)
    rows = lambda b, i: (b * nt + i, 0)
    rep = lambda v, w: jnp.repeat(v, w).reshape(1, N_HEADS * w)
    o, sfin = pl.pallas_call(
        functools.partial(_gdn_kernel, rows_in=rows_in, t_valid=t_valid, hp=hp),
        grid=(bsz, nt),
        in_specs=[
            pl.BlockSpec((rows_in, 1024), lambda b, i: (b * nt + i, 1)),
            pl.BlockSpec((rows_in, LANES), rows),
            pl.BlockSpec((1, 3, B_CONV_CH), lambda b, i: (b, 0, 0)),
            pl.BlockSpec((1, 256, 256), lambda b, i: (b, 0, 0)),
            pl.BlockSpec((CONV_WIDTH, B_CONV_CH), fixed),
            pl.BlockSpec((1, 512), fixed),
            pl.BlockSpec((1, 512), fixed),
            pl.BlockSpec((1, 256), fixed),
            pl.BlockSpec((1, 256), fixed),
            pl.BlockSpec((1, 256), fixed),
            pl.BlockSpec((256, 256), fixed),
            pl.BlockSpec((CHUNK, CHUNK), fixed),
            pl.BlockSpec((LANES, 512), fixed),
            pl.BlockSpec((LANES, 512), fixed),
            pl.BlockSpec((LANES, 256), fixed),
            pl.BlockSpec((LANES, 256), fixed),
        ],
        out_specs=[
            pl.BlockSpec((rows_in, 256), rows),
            pl.BlockSpec((1, 256, 256), lambda b, i: (b, 0, 0)),
        ],
        out_shape=[
            jax.ShapeDtypeStruct((pm.shape[0], 256), F32),
            jax.ShapeDtypeStruct((bsz, 256, 256), F32),
        ],
        scratch_shapes=[pltpu.VMEM((ROW0 + CHUNK, B_CONV_CH), F32), pltpu.VMEM((256, 256), F32)],
        compiler_params=pltpu.CompilerParams(
            dimension_semantics=("parallel", "arbitrary"), vmem_limit_bytes=VMEM_LIMIT),
        name="gdn",
    )(pm, ps, conv_prev, _block_diag_state(s0, bsz), conv_w, rep(dt_bias, 128), rep(a_log, 128),
      rep(dt_bias, 64), rep(a_log, 64), norm_g.reshape(1, 256), jnp.asarray(_BLK256, BF16),
      jnp.asarray(_TRI, BF16), _head_expand(GDN_BETA_LANE, 128), _head_expand(GDN_DT_LANE, 128),
      _head_expand(GDN_BETA_LANE, 64), _head_expand(GDN_DT_LANE, 64))
    return o, _unblock_diag_state(sfin, bsz)


def _head_rmsnorm(x, g):
    y = x * lax.rsqrt(jnp.mean(x * x, axis=-1, keepdims=True) + EPS)
    return y * g.reshape((-1, x.shape[-1]))


def _l2norm(x):
    return x * lax.rsqrt(jnp.sum(x * x, axis=-1, keepdims=True) + EPS)


def _masked_exp(logit, mask):
    return jnp.where(mask, jnp.exp(jnp.where(mask, logit, 0.0)), 0.0)


def _causal_conv(x, w, prev):
    t = x.shape[1]
    xp = jnp.concatenate([prev.astype(x.dtype), x], axis=1)
    y = xp[:, 0:t] * w[0]
    for j in range(1, w.shape[0]):
        y = y + xp[:, j:j + t] * w[j]
    return y, xp[:, t:]


def _pad_time(t, pad):
    return jnp.pad(t, [(0, 0), (0, pad)] + [(0, 0)] * (t.ndim - 2))


def _to_chunks(t, chunk):
    b, tt, h = t.shape[:3]
    t = t.reshape((b, tt // chunk, chunk, h) + t.shape[3:])
    return jnp.moveaxis(t, 3, 1)


def _from_chunks(t):
    b, h, n, c = t.shape[:4]
    t = jnp.moveaxis(t, 1, 3)
    return t.reshape((b, n * c, h) + t.shape[4:])


def _chunk_inputs(arrs, chunk):
    t = arrs[0].shape[1]
    pad = (-t) % chunk
    return [_to_chunks(_pad_time(a.astype(F32), pad), chunk) for a in arrs]


def _chunked_gla(q, k, v, log_f, s0, chunk):
    t = q.shape[1]
    q, k, v, log_f = _chunk_inputs([q, k, v, log_f], chunk)
    b = jnp.cumsum(log_f, axis=3)
    causal = jnp.tril(jnp.ones((chunk, chunk), bool))[:, :, None]
    rel = _masked_exp(b[..., :, None, :] - b[..., None, :, :], causal)
    att = jnp.einsum('bhntk,bhnsk,bhntsk->bhnts', q, k, rel)
    o_intra = jnp.einsum('bhnts,bhnsv->bhntv', att, v)
    b_last = b[..., -1:, :]
    ds = jnp.einsum('bhnsk,bhnsv->nbhkv', k * jnp.exp(b_last - b), v)
    a_chunk = jnp.moveaxis(jnp.exp(b_last[..., 0, :]), 2, 0)

    def step(s, inp):
        a, d = inp
        return a[..., None] * s + d, s

    s_final, s_prev = lax.scan(step, s0.astype(F32), (a_chunk, ds))
    o_inter = jnp.einsum('bhntk,nbhkv->bhntv', q * jnp.exp(b), s_prev)
    return _from_chunks(o_intra + o_inter)[:, :t], s_final


def _chunked_ssd(q, k, v, log_a, s0, chunk):
    t = q.shape[1]
    q, k, v, la = _chunk_inputs([q, k, v, log_a], chunk)
    b = jnp.cumsum(la, axis=-1)
    causal = jnp.tril(jnp.ones((chunk, chunk), bool))
    rel = _masked_exp(b[..., :, None] - b[..., None, :], causal)
    att = jnp.einsum('bhntk,bhnsk->bhnts', q, k) * rel
    o_intra = jnp.einsum('bhnts,bhnsv->bhntv', att, v)
    b_last = b[..., -1:]
    ds = jnp.einsum('bhnsk,bhnsv->nbhkv', k * jnp.exp(b_last - b)[..., None], v)
    a_chunk = jnp.moveaxis(jnp.exp(b_last[..., 0]), 2, 0)

    def step(s, inp):
        a, d = inp
        return a[..., None, None] * s + d, s

    s_final, s_prev = lax.scan(step, s0.astype(F32), (a_chunk, ds))
    o_inter = jnp.einsum('bhntk,nbhkv->bhntv', q * jnp.exp(b)[..., None], s_prev)
    return _from_chunks(o_intra + o_inter)[:, :t], s_final


def _chunked_gated_delta(q, k, v, beta, log_a, s0, chunk):
    t = q.shape[1]
    vd = v.shape[-1]
    q, k, v, beta, la = _chunk_inputs([q, k, v, beta, log_a], chunk)
    b = jnp.cumsum(la, axis=-1)
    causal = jnp.tril(jnp.ones((chunk, chunk), bool))
    strict = jnp.tril(jnp.ones((chunk, chunk), bool), k=-1)
    decay = _masked_exp(b[..., :, None] - b[..., None, :], causal)
    kb = k * beta[..., None]
    m = jnp.where(strict, jnp.einsum('bhntk,bhnsk->bhnts', kb, k) * decay, 0.0)
    rhs = jnp.concatenate([v * beta[..., None], kb * jnp.exp(b)[..., None]], axis=-1)
    sol = lax.linalg.triangular_solve(m + jnp.eye(chunk, dtype=F32), rhs, left_side=True,
                                      lower=True, unit_diagonal=True)
    u, w = sol[..., :vd], sol[..., vd:]
    att = jnp.einsum('bhntk,bhnsk->bhnts', q, k) * decay
    q_dec = q * jnp.exp(b)[..., None]
    k_dec = k * jnp.exp(b[..., -1:] - b)[..., None]
    a_chunk = jnp.exp(b[..., -1])
    xs = tuple(jnp.moveaxis(a, 2, 0) for a in (u, w, att, q_dec, k_dec, a_chunk))

    def step(s, inp):
        u_c, w_c, att_c, qd_c, kd_c, a_c = inp
        v_new = u_c - jnp.einsum('bhtk,bhkv->bhtv', w_c, s)
        o = jnp.einsum('bhtk,bhkv->bhtv', qd_c, s) + jnp.einsum('bhts,bhsv->bhtv', att_c, v_new)
        s = a_c[..., None, None] * s + jnp.einsum('bhsk,bhsv->bhkv', kd_c, v_new)
        return s, o

    s_final, o = lax.scan(step, s0.astype(F32), xs)
    return _from_chunks(jnp.moveaxis(o, 0, 2))[:, :t], s_final


def _hgrn2_mixer(pa, lb, norm_g, s0):
    bsz, t, _ = pa.shape
    w = GROUP_WIDTH
    q, zf, inp, g = (pa[..., j * w:(j + 1) * w] for j in range(4))
    lb = jnp.clip(lb, 0.0, LB_CEIL)
    log_f = jnp.logaddexp(jnp.log(jnp.maximum(lb, LB_FLOOR)), jnp.log1p(-lb) + jax.nn.log_sigmoid(zf))
    k = (1.0 - lb) * jax.nn.sigmoid(-zf)
    heads = lambda a: a.reshape(bsz, t, N_HEADS, -1)
    o, s = _chunked_gla(heads(q) * HEAD_DIM ** -0.5, heads(k), heads(inp), heads(log_f), s0, CHUNK_A)
    o = _head_rmsnorm(o, norm_g) * jax.nn.silu(heads(g))
    return o.reshape(bsz, t, w), s


def _gdn_mixer(pb, beta_pre, dt_pre, conv_w, conv_prev, a_log, dt_bias, norm_g, s0):
    bsz, t, _ = pb.shape
    w = GROUP_WIDTH
    qkv, conv_new = _causal_conv(pb[..., :3 * w], conv_w, conv_prev)
    qkv = jax.nn.silu(qkv)
    heads = lambda a: a.reshape(bsz, t, N_HEADS, -1)
    q = _l2norm(heads(qkv[..., :w])) * HEAD_DIM ** -0.5
    k = _l2norm(heads(qkv[..., w:2 * w]))
    v = heads(qkv[..., 2 * w:3 * w])
    gate = heads(pb[..., 3 * w:4 * w])
    beta = jax.nn.sigmoid(beta_pre)
    log_a = -jnp.exp(a_log) * jax.nn.softplus(dt_pre + dt_bias)
    o, s = _chunked_gated_delta(q, k, v, beta, log_a, s0, CHUNK_B)
    o = _head_rmsnorm(o, norm_g) * jax.nn.silu(gate)
    return o.reshape(bsz, t, w), s, conv_new


def _fox_attention_prompt(q, k, v, log_f):
    b, t, h, d = q.shape
    n_blk = -(-t // Q_BLOCK)
    c = jnp.moveaxis(jnp.cumsum(log_f, axis=1), 2, 1)
    key_pos = jnp.arange(t)

    def block(i):
        start = i * Q_BLOCK
        q_i = lax.dynamic_slice_in_dim(q, start, Q_BLOCK, axis=1)
        c_i = lax.dynamic_slice_in_dim(c, start, Q_BLOCK, axis=2)
        s = jnp.einsum('bqhd,bkhd->bhqk', q_i, k) * (d ** -0.5) + c_i[..., :, None] - c[:, :, None, :]
        allowed = (start + jnp.arange(Q_BLOCK))[:, None] >= key_pos[None, :]
        p = jax.nn.softmax(jnp.where(allowed, s, NEG_BIG), axis=-1)
        return jnp.einsum('bhqk,bkhd->bqhd', p, v)

    o = lax.map(block, jnp.arange(n_blk))
    return jnp.moveaxis(o, 0, 1).reshape(b, n_blk * Q_BLOCK, h, d)[:, :t]


def _fox_attention_sample(q, k, v, log_f, k_past, v_past, logf_past):
    n_past = k_past.shape[1]
    s_new = q.shape[1]
    d = q.shape[-1]
    k_all = jnp.concatenate([k_past, k], axis=1)
    v_all = jnp.concatenate([v_past, v], axis=1)
    c = jnp.cumsum(jnp.concatenate([logf_past, log_f], axis=1), axis=1)
    c = jnp.moveaxis(c, 2, 1)
    s = jnp.einsum('bqhd,bkhd->bhqk', q, k_all) * (d ** -0.5) + c[:, :, n_past:, None] - c[:, :, None, :]
    allowed = (n_past + jnp.arange(s_new))[:, None] >= jnp.arange(n_past + s_new)[None, :]
    p = jax.nn.softmax(jnp.where(allowed, s, NEG_BIG), axis=-1)
    return jnp.einsum('bhqk,bkhd->bqhd', p, v_all)


def _fox_mixer(pc, lf_pre, b_f, q_norm, k_norm, out_norm, past):
    bsz, t, _ = pc.shape
    w = GROUP_WIDTH
    heads = lambda a: a.reshape(bsz, t, N_HEADS, HEAD_DIM)
    q = _head_rmsnorm(heads(pc[..., :w]), q_norm)
    k = _head_rmsnorm(heads(pc[..., w:2 * w]), k_norm)
    v = heads(pc[..., 2 * w:3 * w])
    g = heads(pc[..., 3 * w:4 * w])
    log_f = jax.nn.log_sigmoid(lf_pre + b_f)
    if past is None:
        o = _fox_attention_prompt(q, k, v, log_f)
    else:
        o = _fox_attention_sample(q, k, v, log_f, past[0], past[1], past[2])
    o = _head_rmsnorm(o, out_norm) * jax.nn.sigmoid(g)
    return o.reshape(bsz, t, w), k, v, log_f


def _ssd_mixer(pd, dt_pre, conv_w, conv_b, conv_prev, a_log, dt_bias, d_skip, norm_g, s0):
    bsz, t, _ = pd.shape
    w = GROUP_WIDTH
    gn = D_GROUPS * D_STATE
    rep = N_HEADS // D_GROUPS
    z = pd[..., :w].reshape(bsz, t, N_HEADS, HEAD_DIM)
    xbc, conv_new = _causal_conv(pd[..., w:w + D_CONV_CH], conv_w, conv_prev)
    xbc = jax.nn.silu(xbc + conv_b)
    xs = xbc[..., :w].reshape(bsz, t, N_HEADS, HEAD_DIM)
    bm = jnp.repeat(xbc[..., w:w + gn].reshape(bsz, t, D_GROUPS, D_STATE), rep, axis=2)
    cm = jnp.repeat(xbc[..., w + gn:].reshape(bsz, t, D_GROUPS, D_STATE), rep, axis=2)
    dt = jax.nn.softplus(dt_pre + dt_bias)
    log_a = -jnp.exp(a_log) * dt
    y, s = _chunked_ssd(cm, bm, xs * dt[..., None], log_a, s0, CHUNK_D)
    y = (y + xs * d_skip[:, None]) * jax.nn.silu(z)
    y = _head_rmsnorm(y.reshape(bsz, t, D_GROUPS, -1), norm_g)
    return y.reshape(bsz, t, w), s, conv_new


def _gather_pages(pool, page_table):
    g = pool[page_table]
    return g.reshape((g.shape[0], g.shape[1] * g.shape[2]) + g.shape[3:])


def _prep_weights(prm):
    w_in = prm['w_in']
    wm = w_in[:, :, _MAIN_COLS]
    ws = jnp.pad(w_in[:, :, _SMALL_COLS], ((0, 0), (0, 0), (0, LANES - len(_SMALL_COLS))))
    wr = jnp.concatenate([prm['moe_w_expert'], prm['moe_w_group']], axis=-1)
    n_r = N_EXPERTS + N_EXPERT_GROUPS
    wr = jnp.pad(wr, ((0, 0), (0, 0), (0, LANES - n_r)))
    both = lambda pair: [(pair[0][:D_MODEL], pair[1][:D_MODEL]), (pair[0][D_MODEL:], pair[1][D_MODEL:])]
    ws_split = both(_split_weight(ws.reshape(DEPTH * D_MODEL, LANES)))
    wr_split = both(_split_weight(wr.reshape(DEPTH * D_MODEL, LANES)))
    br = jnp.pad(jnp.concatenate([prm['moe_b_expert'], prm['moe_b_group']], axis=-1),
                 ((0, 0), (0, LANES - n_r)))[:, None, :]
    bf = lambda a: a.astype(BF16)
    prompt, sample = [], []
    for l in range(DEPTH):
        common = dict(ws=ws_split[l], wr=wr_split[l], br=br[l])
        if l == 0:
            wm_p, wo_p = tuple(_split_weight(wm[l])), tuple(_split_weight(prm['w_out'][l]))
        else:
            wm_p, wo_p = (bf(wm[l]),), (bf(prm['w_out'][l]),)
        prompt.append(dict(common, wm=wm_p, wo=wo_p, w1=bf(prm['moe_w1'][l]), w3=bf(prm['moe_w3'][l]),
                           w2=bf(prm['moe_w2'][l]), wg=bf(prm['ple_w_gate'][l]), wp=bf(prm['ple_w_proj'][l])))
        sample.append(dict(common, wm=(wm[l],), wo=(prm['w_out'][l],), w1=prm['moe_w1'][l], w3=prm['moe_w3'][l],
                           w2=prm['moe_w2'][l], wg=prm['ple_w_gate'][l], wp=prm['ple_w_proj'][l]))
    return prompt, sample


def _trunk(x, p, init_state, fox_cache, lb_all, prm, wts, tm, hp_layers):
    s_hgrn0, s_gdn0, c_gdn0, s_ssd0, c_ssd0 = init_state
    bsz, t, _ = x.shape
    n = bsz * t
    h = x.reshape(n, D_MODEL)
    outs = [[] for _ in range(8)]
    row = lambda a: a.reshape(1, -1)
    for l in range(DEPTH):
        hp_mix = hp_layers[l]
        w = wts[l]
        pm, ps = _inproj(h, row(prm['g_mix'][l]), w['wm'], w['ws'][0], w['ws'][1], tm)
        if fox_cache is None:
            o_c, k_c, lf_c = _fox_prompt(pm, ps, prm['fox_b_f'][l], prm['fox_q_norm'][l], prm['fox_k_norm'][l],
                                         prm['fox_out_norm'][l], bsz, t, hp_mix)
            o_c = o_c.reshape(bsz, t, GROUP_WIDTH)
            k_c = k_c.reshape(bsz, t, N_HEADS, HEAD_DIM)
            lf_c = lf_c.reshape(bsz, t, N_HEADS)
            v_c = pm[:, 2560:2816].reshape(bsz, t, N_HEADS, HEAD_DIM)
        rows_in = CHUNK if t % CHUNK == 0 else 8
        if rows_in == CHUNK:
            pmr, psr = pm, ps
        else:
            padr = lambda a: jnp.pad(a.reshape(bsz, t, -1), ((0, 0), (0, rows_in - t), (0, 0))).reshape(
                bsz * rows_in, -1)
            pmr, psr = padr(pm), padr(ps)
        unpad = lambda a: a.reshape(bsz, -1, GROUP_WIDTH)[:, :t]
        o_a, s_a = _hgrn(pmr, lb_all[l], prm['hgrn_norm'][l], s_hgrn0[l], bsz, t, rows_in, hp_mix)
        o_d, s_d = _ssd(pmr, psr, c_ssd0[l], s_ssd0[l], prm['ssd_conv_w'][l], prm['ssd_conv_b'][l],
                        prm['ssd_a_log'][l], prm['ssd_dt_bias'][l], prm['ssd_d'][l], prm['ssd_norm'][l],
                        bsz, t, rows_in, hp_mix)
        o_b, s_b = _gdn(pmr, psr, c_gdn0[l], s_gdn0[l], prm['gdn_conv_w'][l], prm['gdn_a_log'][l],
                        prm['gdn_dt_bias'][l], prm['gdn_norm'][l], bsz, t, rows_in, hp_mix)
        o_a, o_b, o_d = unpad(o_a), unpad(o_b), unpad(o_d)
        pm = pm.reshape(bsz, t, N_MAIN)
        ps = ps.reshape(bsz, t, LANES)
        c_d = jnp.concatenate([c_ssd0[l], pm[:, max(0, t - 3):, 3328:4096]], axis=1)[:, -(CONV_WIDTH - 1):]
        c_b = jnp.concatenate([c_gdn0[l], pm[:, max(0, t - 3):, 1024:1792]], axis=1)[:, -(CONV_WIDTH - 1):]
        if fox_cache is not None:
            cache_k, cache_v, cache_logf, page_table = fox_cache
            o_c, k_c, lf_c = _fox_sample(pmr, psr, cache_k, cache_v, cache_logf, page_table, l,
                                         prm['fox_b_f'][l], prm['fox_q_norm'][l], prm['fox_k_norm'][l],
                                         prm['fox_out_norm'][l])
            o_c = o_c.reshape(bsz, t, GROUP_WIDTH)
            k_c = k_c.reshape(bsz, t, N_HEADS, HEAD_DIM)
            lf_c = lf_c.reshape(bsz, t, N_HEADS)
            v_c = pm[..., 2560:2816].reshape(bsz, t, N_HEADS, HEAD_DIM)
        mix = jnp.concatenate([o_a, o_b, o_c, o_d], axis=-1).reshape(n, D_MODEL)
        h2, u2, cw = _outproj(h, mix, w['wo'], row(prm['g_ffn'][l]), w['wr'][0], w['wr'][1], w['br'], tm)
        y = _moe(u2, cw, w['w1'], w['w3'], w['w2'], tm)
        h = _ple(h2, y, p[l].reshape(n, PLE_DIM), row(prm['g_ple'][l]), w['wg'], w['wp'],
                 row(prm['g_final']), tm, final=(l == DEPTH - 1))
        for acc, val in zip(outs, (k_c, v_c, lf_c, s_a, s_b, c_b, s_d, c_d)):
            acc.append(val)
    return (h.reshape(bsz, t, D_MODEL),) + tuple(jnp.stack(acc) for acc in outs)


def _hgrn_lower_bounds(lb_param):
    sm = jax.nn.softmax(lb_param, axis=0)
    return jnp.concatenate([jnp.zeros_like(sm[:1]), jnp.cumsum(sm[1:], axis=0)], axis=0)


def kernel(x_prompt, x_sample, cache_fox_k, cache_fox_v, cache_fox_logf, state_hgrn, state_gdn,
           state_gdn_conv, state_ssd, state_ssd_conv, page_table, p_prompt, p_sample, w_in, w_out,
           g_mix, g_ffn, g_ple, g_final, hgrn_lb, hgrn_norm, gdn_conv_w, gdn_a_log, gdn_dt_bias,
           gdn_norm, fox_b_f, fox_q_norm, fox_k_norm, fox_out_norm, ssd_conv_w, ssd_conv_b, ssd_a_log,
           ssd_dt_bias, ssd_d, ssd_norm, moe_w_group, moe_b_group, moe_w_expert, moe_b_expert, moe_w1,
           moe_w3, moe_w2, ple_w_gate, ple_w_proj):
    prm = dict(w_in=w_in, w_out=w_out, g_mix=g_mix, g_ffn=g_ffn, g_ple=g_ple, g_final=g_final,
               hgrn_norm=hgrn_norm, gdn_conv_w=gdn_conv_w, gdn_a_log=gdn_a_log, gdn_dt_bias=gdn_dt_bias,
               gdn_norm=gdn_norm, fox_b_f=fox_b_f, fox_q_norm=fox_q_norm, fox_k_norm=fox_k_norm,
               fox_out_norm=fox_out_norm, ssd_conv_w=ssd_conv_w, ssd_conv_b=ssd_conv_b, ssd_a_log=ssd_a_log,
               ssd_dt_bias=ssd_dt_bias, ssd_d=ssd_d, ssd_norm=ssd_norm, moe_w_group=moe_w_group,
               moe_b_group=moe_b_group, moe_w_expert=moe_w_expert, moe_b_expert=moe_b_expert,
               moe_w1=moe_w1, moe_w3=moe_w3, moe_w2=moe_w2, ple_w_gate=ple_w_gate, ple_w_proj=ple_w_proj)
    wts_prompt, wts_sample = _prep_weights(prm)
    lb_all = _hgrn_lower_bounds(hgrn_lb)
    bp = x_prompt.shape[0]
    zero_state = (jnp.zeros((DEPTH, bp, N_HEADS, HEAD_DIM, HEAD_DIM), F32),
                  jnp.zeros((DEPTH, bp, N_HEADS, HEAD_DIM, HEAD_DIM), F32),
                  jnp.zeros((DEPTH, bp, CONV_WIDTH - 1, B_CONV_CH), F32),
                  jnp.zeros((DEPTH, bp, N_HEADS, D_STATE, HEAD_DIM), F32),
                  jnp.zeros((DEPTH, bp, CONV_WIDTH - 1, D_CONV_CH), F32))
    pr = _trunk(x_prompt, p_prompt, zero_state, None, lb_all, prm, wts_prompt, tm=512, hp_layers=(True, False))
    sm = _trunk(x_sample, p_sample, (state_hgrn, state_gdn, state_gdn_conv, state_ssd, state_ssd_conv),
                (cache_fox_k, cache_fox_v, cache_fox_logf, page_table), lb_all, prm, wts_sample, tm=32,
                hp_layers=(True, True))
    return (pr[0], sm[0]) + tuple(pr[1:]) + tuple(sm[1:])
```

```python
import functools
import math

import jax
import jax.numpy as jnp
import numpy as np
from jax import lax
from jax.experimental import pallas as pl
from jax.experimental.pallas import tpu as pltpu

F32 = jnp.float32
BF16 = jnp.bfloat16

D_MODEL = 1024
DEPTH = 2
PAGE_SIZE = 128
EPS = 1e-6
NEG_BIG = -1e30
LB_FLOOR = 1e-30
LB_CEIL = 1.0 - 1e-6
PLE_DIM = 256
GROUP_WIDTH = 256
HEAD_DIM = 64
N_HEADS = 4
D_GROUPS = 2
D_STATE = 128
CONV_WIDTH = 4
B_CONV_CH = 3 * GROUP_WIDTH
D_CONV_CH = GROUP_WIDTH + 2 * D_GROUPS * D_STATE
SIZE_A = 4 * GROUP_WIDTH
SIZE_B = 4 * GROUP_WIDTH + 2 * N_HEADS
SIZE_C = 4 * GROUP_WIDTH + N_HEADS
SIZE_D = GROUP_WIDTH + D_CONV_CH + N_HEADS
OFF_B = SIZE_A
OFF_C = OFF_B + SIZE_B
OFF_D = OFF_C + SIZE_C
N_IN = OFF_D + SIZE_D
CHUNK_A = 16
CHUNK_B = 64
CHUNK_D = 128
Q_BLOCK = 128
N_EXPERT_GROUPS = 4
EXPERTS_PER_GROUP = 4
N_EXPERTS = 16
D_EXPERT = 512

LANES = 128
N_MAIN = 4096
VMEM_LIMIT = 48 * 1024 * 1024

_MAIN_COLS = np.concatenate([
    np.arange(0, SIZE_A),
    np.arange(OFF_B, OFF_B + 4 * GROUP_WIDTH),
    np.arange(OFF_C, OFF_C + 4 * GROUP_WIDTH),
    np.arange(OFF_D, OFF_D + 4 * GROUP_WIDTH),
])
_SMALL_COLS = np.concatenate([
    np.arange(OFF_B + 4 * GROUP_WIDTH, OFF_B + SIZE_B),
    np.arange(OFF_C + 4 * GROUP_WIDTH, OFF_C + SIZE_C),
    np.arange(OFF_D + 4 * GROUP_WIDTH, OFF_D + SIZE_D),
])


def _split_bf16(x):
    hi = x.astype(BF16)
    lo = (x - hi.astype(F32)).astype(BF16)
    return hi, lo


def _dot(a, b):
    return jnp.dot(a, b, preferred_element_type=F32)


def _dot3(a_hi, a_lo, b_hi, b_lo):
    return _dot(a_hi, b_hi) + _dot(a_lo, b_hi) + _dot(a_hi, b_lo)


def _rms(x, g):
    return x * lax.rsqrt(jnp.mean(x * x, axis=-1, keepdims=True) + EPS) * g


def _wmode(w):
    if len(w) == 2:
        return 'x3'
    return 'f32' if w[0].dtype == F32 else 'x1'


def _wcount(mode):
    return 2 if mode == 'x3' else 1


def _wload(refs, mode, idx=None):
    get = (lambda r: r[...]) if idx is None else (lambda r: r[idx])
    if mode == 'x1':
        return get(refs[0]), None
    if mode == 'x3':
        return get(refs[0]), get(refs[1])
    return _split_bf16(get(refs[0]))


def _wdot(a, refs, mode, idx=None):
    w_hi, w_lo = _wload(refs, mode, idx)
    if w_lo is None:
        return _dot(a.astype(BF16), w_hi)
    a_hi, a_lo = _split_bf16(a)
    return _dot3(a_hi, a_lo, w_hi, w_lo)


def _split_kernel(w_ref, hi_ref, lo_ref):
    hi, lo = _split_bf16(w_ref[...])
    hi_ref[...] = hi
    lo_ref[...] = lo


def _split_weight(w):
    rows, cols = w.shape
    spec = pl.BlockSpec((256, cols), lambda i: (i, 0))
    return pl.pallas_call(
        _split_kernel,
        grid=(rows // 256,),
        in_specs=[spec],
        out_specs=[spec, spec],
        out_shape=[jax.ShapeDtypeStruct((rows, cols), BF16)] * 2,
        compiler_params=pltpu.CompilerParams(dimension_semantics=("parallel",)),
        name="split_weight",
    )(w)


def _inproj_kernel(x_ref, g_ref, *refs, mode):
    nw = _wcount(mode)
    wm_refs = refs[:nw]
    wsh_ref, wsl_ref, om_ref, os_ref, uh_ref, ul_ref = refs[nw:]

    @pl.when(pl.program_id(1) == 0)
    def _():
        u = _rms(x_ref[...], g_ref[...])
        uh, ul = _split_bf16(u)
        uh_ref[...] = uh
        ul_ref[...] = ul
        os_ref[...] = _dot3(uh, ul, wsh_ref[...], wsl_ref[...])

    w_hi, w_lo = _wload(wm_refs, mode)
    if w_lo is None:
        om_ref[...] = _dot(uh_ref[...], w_hi)
    else:
        om_ref[...] = _dot3(uh_ref[...], ul_ref[...], w_hi, w_lo)


def _inproj(x, g, wm, wsh, wsl, tm):
    n = x.shape[0]
    tn = 1024
    return pl.pallas_call(
        functools.partial(_inproj_kernel, mode=_wmode(wm)),
        grid=(n // tm, N_MAIN // tn),
        in_specs=[
            pl.BlockSpec((tm, D_MODEL), lambda i, j: (i, 0)),
            pl.BlockSpec((1, D_MODEL), lambda i, j: (0, 0)),
        ] + [pl.BlockSpec((D_MODEL, tn), lambda i, j: (0, j))] * len(wm) + [
            pl.BlockSpec((D_MODEL, LANES), lambda i, j: (0, 0)),
            pl.BlockSpec((D_MODEL, LANES), lambda i, j: (0, 0)),
        ],
        out_specs=[
            pl.BlockSpec((tm, tn), lambda i, j: (i, j)),
            pl.BlockSpec((tm, LANES), lambda i, j: (i, 0)),
        ],
        out_shape=[
            jax.ShapeDtypeStruct((n, N_MAIN), F32),
            jax.ShapeDtypeStruct((n, LANES), F32),
        ],
        scratch_shapes=[pltpu.VMEM((tm, D_MODEL), BF16), pltpu.VMEM((tm, D_MODEL), BF16)],
        compiler_params=pltpu.CompilerParams(
            dimension_semantics=("parallel", "arbitrary"), vmem_limit_bytes=VMEM_LIMIT),
        name="inproj",
    )(x, g, *wm, wsh, wsl)


def _route(logits):
    lane = lax.broadcasted_iota(jnp.int32, logits.shape, 1)
    gmask = (lane >= N_EXPERTS) & (lane < N_EXPERTS + N_EXPERT_GROUPS)
    gl = jnp.where(gmask, logits, -jnp.inf)
    gmax = jnp.max(gl, axis=-1, keepdims=True)
    gidx = jnp.min(jnp.where(gl == gmax, lane, 4 * LANES), axis=-1, keepdims=True) - N_EXPERTS
    gw = 1.0 / jnp.sum(jnp.where(gmask, jnp.exp(gl - gmax), 0.0), axis=-1, keepdims=True)
    lo = gidx * EXPERTS_PER_GROUP
    emask = (lane >= lo) & (lane < lo + EXPERTS_PER_GROUP)
    el = jnp.where(emask, logits, -jnp.inf)
    m1 = jnp.max(el, axis=-1, keepdims=True)
    i1 = jnp.min(jnp.where(el == m1, lane, 4 * LANES), axis=-1, keepdims=True)
    el2 = jnp.where(lane == i1, -jnp.inf, el)
    m2 = jnp.max(el2, axis=-1, keepdims=True)
    i2 = jnp.min(jnp.where(el2 == m2, lane, 4 * LANES), axis=-1, keepdims=True)
    e2 = jnp.exp(m2 - m1)
    den = 1.0 + e2
    g1 = gw / den
    g2 = gw * e2 / den
    return jnp.where(lane == i1, g1, jnp.where(lane == i2, g2, 0.0))


def _outproj_kernel(h_ref, mix_ref, *refs, mode):
    nw = _wcount(mode)
    g_ref, wrh_ref, wrl_ref, br_ref, h2_ref, u2_ref, cw_ref = refs[nw:]
    h2 = h_ref[...] + _wdot(mix_ref[...], refs[:nw], mode)
    h2_ref[...] = h2
    u = _rms(h2, g_ref[...])
    u2_ref[...] = u
    uh, ul = _split_bf16(u)
    logits = _dot3(uh, ul, wrh_ref[...], wrl_ref[...]) + br_ref[...]
    cw_ref[...] = _route(logits)


def _outproj(h, mix, wo, g, wrh, wrl, br, tm):
    n = h.shape[0]
    row = lambda i: (i, 0)
    fixed = lambda i: (0, 0)
    return pl.pallas_call(
        functools.partial(_outproj_kernel, mode=_wmode(wo)),
        grid=(n // tm,),
        in_specs=[
            pl.BlockSpec((tm, D_MODEL), row),
            pl.BlockSpec((tm, D_MODEL), row),
        ] + [pl.BlockSpec((D_MODEL, D_MODEL), fixed)] * len(wo) + [
            pl.BlockSpec((1, D_MODEL), fixed),
            pl.BlockSpec((D_MODEL, LANES), fixed),
            pl.BlockSpec((D_MODEL, LANES), fixed),
            pl.BlockSpec((1, LANES), fixed),
        ],
        out_specs=[
            pl.BlockSpec((tm, D_MODEL), row),
            pl.BlockSpec((tm, D_MODEL), row),
            pl.BlockSpec((tm, LANES), row),
        ],
        out_shape=[
            jax.ShapeDtypeStruct((n, D_MODEL), F32),
            jax.ShapeDtypeStruct((n, D_MODEL), F32),
            jax.ShapeDtypeStruct((n, LANES), F32),
        ],
        compiler_params=pltpu.CompilerParams(
            dimension_semantics=("parallel",), vmem_limit_bytes=VMEM_LIMIT),
        name="outproj",
    )(h, mix, *wo, g, wrh, wrl, br)


def _moe_kernel(x_ref, cw_ref, w1_ref, w3_ref, w2_ref, y_ref, xh_ref, xl_ref, *, mode):
    e = pl.program_id(1)

    @pl.when(e == 0)
    def _():
        y_ref[...] = jnp.zeros_like(y_ref)
        xh, xl = _split_bf16(x_ref[...])
        xh_ref[...] = xh
        xl_ref[...] = xl

    cw = cw_ref[...]
    lane = lax.broadcasted_iota(jnp.int32, cw.shape, 1)
    col = jnp.sum(jnp.where(lane == e, cw, 0.0), axis=-1, keepdims=True)
    xh = xh_ref[...]
    w1_hi, w1_lo = _wload((w1_ref,), mode, 0)
    w3_hi, w3_lo = _wload((w3_ref,), mode, 0)
    if w1_lo is None:
        a = _dot(xh, w1_hi)
        b = _dot(xh, w3_hi)
    else:
        xl = xl_ref[...]
        a = _dot3(xh, xl, w1_hi, w1_lo)
        b = _dot3(xh, xl, w3_hi, w3_lo)
    hid = (a * jax.nn.sigmoid(a)) * b
    y_ref[...] += col * _wdot(hid, (w2_ref,), mode, 0)


def _moe(u2, cw, w1, w3, w2, tm):
    n = u2.shape[0]
    mode = 'f32' if w1.dtype == F32 else 'x1'
    return pl.pallas_call(
        functools.partial(_moe_kernel, mode=mode),
        grid=(n // tm, N_EXPERTS),
        in_specs=[
            pl.BlockSpec((tm, D_MODEL), lambda i, e: (i, 0)),
            pl.BlockSpec((tm, LANES), lambda i, e: (i, 0)),
            pl.BlockSpec((1, D_MODEL, D_EXPERT), lambda i, e: (e, 0, 0)),
            pl.BlockSpec((1, D_MODEL, D_EXPERT), lambda i, e: (e, 0, 0)),
            pl.BlockSpec((1, D_EXPERT, D_MODEL), lambda i, e: (e, 0, 0)),
        ],
        out_specs=pl.BlockSpec((tm, D_MODEL), lambda i, e: (i, 0)),
        out_shape=jax.ShapeDtypeStruct((n, D_MODEL), F32),
        scratch_shapes=[pltpu.VMEM((tm, D_MODEL), BF16), pltpu.VMEM((tm, D_MODEL), BF16)],
        compiler_params=pltpu.CompilerParams(
            dimension_semantics=("parallel", "arbitrary"), vmem_limit_bytes=VMEM_LIMIT),
        name="moe",
    )(u2, cw, w1, w3, w2)


def _ple_kernel(h_ref, y_ref, p_ref, g_ref, wg_ref, wp_ref, gf_ref, o_ref, *, final, mode):
    h3 = h_ref[...] + y_ref[...]
    u = _rms(h3, g_ref[...])
    gate = jax.nn.sigmoid(_wdot(u, (wg_ref,), mode))
    h4 = h3 + gate * _wdot(p_ref[...], (wp_ref,), mode)
    if final:
        h4 = _rms(h4, gf_ref[...])
    o_ref[...] = h4


def _ple(h2, y, p, g, wg, wp, gf, tm, final):
    n = h2.shape[0]
    row = lambda i: (i, 0)
    fixed = lambda i: (0, 0)
    mode = 'f32' if wg.dtype == F32 else 'x1'
    return pl.pallas_call(
        functools.partial(_ple_kernel, final=final, mode=mode),
        grid=(n // tm,),
        in_specs=[
            pl.BlockSpec((tm, D_MODEL), row),
            pl.BlockSpec((tm, D_MODEL), row),
            pl.BlockSpec((tm, PLE_DIM), row),
            pl.BlockSpec((1, D_MODEL), fixed),
            pl.BlockSpec((D_MODEL, D_MODEL), fixed),
            pl.BlockSpec((PLE_DIM, D_MODEL), fixed),
            pl.BlockSpec((1, D_MODEL), fixed),
        ],
        out_specs=pl.BlockSpec((tm, D_MODEL), row),
        out_shape=jax.ShapeDtypeStruct((n, D_MODEL), F32),
        compiler_params=pltpu.CompilerParams(
            dimension_semantics=("parallel",), vmem_limit_bytes=VMEM_LIMIT),
        name="ple",
    )(h2, y, p, g, wg, wp, gf)


FOX_LF_LANE = 8
FOX_TB = 256
FOX_TQ = 256


def _split3_bf16(x):
    hi = x.astype(BF16)
    r = x - hi.astype(F32)
    mid = r.astype(BF16)
    lo = (r - mid.astype(F32)).astype(BF16)
    return hi, mid, lo


def _log_sigmoid(x):
    return jnp.minimum(x, 0.0) - jnp.log1p(jnp.exp(-jnp.abs(x)))


def _group_mean(x2, ones_blk, width):
    hi, mid, lo = _split3_bf16(x2)
    return (_dot(hi, ones_blk) + _dot(mid, ones_blk) + _dot(lo, ones_blk)) * (1.0 / width)


def _fox_layout(hp):
    ka = 256 if hp else 128
    a0 = 192 if hp else 64
    nx = 512 + 3 * LANES
    mq = np.zeros((nx, 4 * ka), np.float32)
    mk = np.zeros((nx, 4 * ka), np.float32)
    rq = np.zeros((1, 4 * ka), np.float32)
    rk = np.zeros((1, 4 * ka), np.float32)
    mv = np.zeros((512, 4 * LANES), np.float32)
    for h in range(N_HEADS):
        for d in range(HEAD_DIM):
            src_hi, src_lo = 64 * h + d, 256 + 64 * h + d
            mq[src_hi, h * ka + d] = 1.0
            mk[src_hi, h * ka + d] = 1.0
            if hp:
                mq[src_lo, h * ka + 64 + d] = 1.0
                mq[src_hi, h * ka + 128 + d] = 1.0
                mk[src_hi, h * ka + 64 + d] = 1.0
                mk[src_lo, h * ka + 128 + d] = 1.0
            mv[src_hi, h * LANES + d] = 1.0
            if hp:
                mv[src_lo, h * LANES + 64 + d] = 1.0
        for part in range(3):
            src = 512 + part * LANES + FOX_LF_LANE + h
            mq[src, h * ka + a0 + part] = 1.0
            mk[src, h * ka + a0 + 3 + part] = -1.0
            rq[0, h * ka + a0 + 3 + part] = 1.0
            rk[0, h * ka + a0 + part] = 1.0
    blk = (np.arange(256)[:, None] // HEAD_DIM == np.arange(256)[None, :] // HEAD_DIM).astype(np.float32)
    tri = np.tril(np.ones((FOX_TB, FOX_TB), np.float32))
    as_bf = lambda a: jnp.asarray(a, BF16)
    return dict(ka=ka, mq=as_bf(mq), mk=as_bf(mk), mv=as_bf(mv), rq=jnp.asarray(rq), rk=jnp.asarray(rk),
                blk=as_bf(blk), tri=as_bf(tri))


def _fox_prep_kernel(pc_ref, ps_ref, bf_ref, qn_ref, kn_ref, blk_ref, tri_ref, mq_ref, mk_ref, mv_ref,
                     rq_ref, rk_ref, krow_ref, lf_ref, qa_ref, ka_ref, vv_ref, cb_ref, carry_ref, *, ka):
    @pl.when(pl.program_id(1) == 0)
    def _():
        carry_ref[...] = jnp.zeros_like(carry_ref)

    blk = blk_ref[...]
    q = pc_ref[:, 0:256]
    k = pc_ref[:, 256:512]
    v = pc_ref[:, 512:768]
    qn = q * lax.rsqrt(_group_mean(q * q, blk, HEAD_DIM) + EPS) * qn_ref[...]
    kn = k * lax.rsqrt(_group_mean(k * k, blk, HEAD_DIM) + EPS) * kn_ref[...]
    krow_ref[...] = kn
    lf = _log_sigmoid(ps_ref[...] + bf_ref[...])
    lf_ref[...] = lf
    l_hi, l_mid, l_lo = _split3_bf16(lf)
    tri = tri_ref[...]
    c = _dot(tri, l_hi) + _dot(tri, l_mid) + _dot(tri, l_lo) + carry_ref[...]
    carry_ref[...] = c[FOX_TB - 1:FOX_TB, :]
    r8 = lax.broadcasted_iota(jnp.int32, (8, LANES), 0)
    cb_ref[...] = jnp.where(r8 == 0, c[0:1, :], jnp.where(r8 == 1, c[FOX_TB - 1:FOX_TB, :], 0.0))
    c_hi, c_mid, c_lo = _split3_bf16(c)
    q_hi, q_lo = _split_bf16(qn * (HEAD_DIM ** -0.5))
    k_hi, k_lo = _split_bf16(kn)
    xq = jnp.concatenate([q_hi, q_lo, c_hi, c_mid, c_lo], axis=-1)
    xk = jnp.concatenate([k_hi, k_lo, c_hi, c_mid, c_lo], axis=-1)
    qa = (_dot(xq, mq_ref[...]) + rq_ref[...]).astype(BF16)
    kk = (_dot(xk, mk_ref[...]) + rk_ref[...]).astype(BF16)
    v_hi, v_lo = _split_bf16(v)
    vv = _dot(jnp.concatenate([v_hi, v_lo], axis=-1), mv_ref[...]).astype(BF16)
    for h in range(N_HEADS):
        qa_ref[0, h] = qa[:, h * ka:(h + 1) * ka]
        ka_ref[0, h] = kk[:, h * ka:(h + 1) * ka]
        vv_ref[0, h] = vv[:, h * LANES:(h + 1) * LANES]


def _fox_prep(pm, ps, bf_row, qn_row, kn_row, lay, bsz, t):
    ka = lay['ka']
    nt = t // FOX_TB
    fixed = lambda b, i: (0, 0)
    rows = lambda b, i: (b * nt + i, 0)
    hm = lambda b, i: (b, 0, i, 0)
    return pl.pallas_call(
        functools.partial(_fox_prep_kernel, ka=ka),
        grid=(bsz, nt),
        in_specs=[
            pl.BlockSpec((FOX_TB, 1024), lambda b, i: (b * nt + i, 2)),
            pl.BlockSpec((FOX_TB, LANES), rows),
            pl.BlockSpec((1, LANES), fixed),
            pl.BlockSpec((1, 256), fixed),
            pl.BlockSpec((1, 256), fixed),
            pl.BlockSpec((256, 256), fixed),
            pl.BlockSpec((FOX_TB, FOX_TB), fixed),
            pl.BlockSpec(lay['mq'].shape, fixed),
            pl.BlockSpec(lay['mk'].shape, fixed),
            pl.BlockSpec(lay['mv'].shape, fixed),
            pl.BlockSpec((1, 4 * ka), fixed),
            pl.BlockSpec((1, 4 * ka), fixed),
        ],
        out_specs=[
            pl.BlockSpec((FOX_TB, 256), rows),
            pl.BlockSpec((FOX_TB, LANES), rows),
            pl.BlockSpec((1, N_HEADS, FOX_TB, ka), hm),
            pl.BlockSpec((1, N_HEADS, FOX_TB, ka), hm),
            pl.BlockSpec((1, N_HEADS, FOX_TB, LANES), hm),
            pl.BlockSpec((8, LANES), rows),
        ],
        out_shape=[
            jax.ShapeDtypeStruct((bsz * t, 256), F32),
            jax.ShapeDtypeStruct((bsz * t, LANES), F32),
            jax.ShapeDtypeStruct((bsz, N_HEADS, t, ka), BF16),
            jax.ShapeDtypeStruct((bsz, N_HEADS, t, ka), BF16),
            jax.ShapeDtypeStruct((bsz, N_HEADS, t, LANES), BF16),
            jax.ShapeDtypeStruct((bsz * nt * 8, LANES), F32),
        ],
        scratch_shapes=[pltpu.VMEM((1, LANES), F32)],
        compiler_params=pltpu.CompilerParams(
            dimension_semantics=("parallel", "arbitrary"), vmem_limit_bytes=VMEM_LIMIT),
        name="fox_prep",
    )(pm, ps, bf_row, qn_row, kn_row, lay['blk'], lay['tri'], lay['mq'], lay['mk'], lay['mv'],
      lay['rq'], lay['rk'])


def _fox_flash_kernel(j0_ref, qa_ref, ka_ref, vv_ref, g_ref, on_ref, o_ref, m_s, l_s, acc_s, *, hp):
    i = pl.program_id(1)
    j_first = j0_ref[pl.program_id(0) * pl.num_programs(1) + i]
    tq = FOX_TQ
    row = lax.broadcasted_iota(jnp.int32, (tq, tq), 0)
    col = lax.broadcasted_iota(jnp.int32, (tq, tq), 1)
    m_s[...] = jnp.full(m_s.shape, NEG_BIG, F32)
    l_s[...] = jnp.zeros(l_s.shape, F32)
    acc_s[...] = jnp.zeros(acc_s.shape, F32)

    def tile(j, masked):
        start = pl.multiple_of(j * tq, tq)
        for h in range(N_HEADS):
            kt = ka_ref[0, h, pl.ds(start, tq), :]
            s = lax.dot_general(qa_ref[0, h], kt, (((1,), (1,)), ((), ())), preferred_element_type=F32)
            if masked:
                s = jnp.where(row >= col, s, NEG_BIG)
            m = m_s[h]
            m_new = jnp.maximum(m, jnp.max(s, axis=-1, keepdims=True))
            alpha = jnp.exp(m - m_new)
            p = jnp.exp(s - m_new)
            l_s[h] = alpha * l_s[h] + jnp.sum(p, axis=-1, keepdims=True)
            m_s[h] = m_new
            vt = vv_ref[0, h, pl.ds(start, tq), :]
            if hp:
                p_hi, p_lo = _split_bf16(p)
                pv = _dot(p_hi, vt) + _dot(p_lo, vt)
            else:
                pv = _dot(p.astype(BF16), vt)
            acc_s[h] = alpha * acc_s[h] + pv

    def body(jj, carry):
        tile(j_first + jj, False)
        return carry

    lax.fori_loop(0, i - j_first, body, 0)
    tile(i, True)
    outs = []
    for h in range(N_HEADS):
        acc = acc_s[h]
        o = (acc[:, :HEAD_DIM] + acc[:, HEAD_DIM:]) / l_s[h]
        o = o * lax.rsqrt(jnp.mean(o * o, axis=-1, keepdims=True) + EPS)
        sl = slice(h * HEAD_DIM, (h + 1) * HEAD_DIM)
        outs.append(o * on_ref[:, sl] * jax.nn.sigmoid(g_ref[:, sl]))
    o_ref[...] = jnp.concatenate(outs, axis=-1)


def _fox_flash(j0, qa, ka, vv, pm, on_row, bsz, t, hp):
    kad = qa.shape[-1]
    nq = t // FOX_TQ
    whole = lambda b, i, j0: (b, 0, 0, 0)
    return pl.pallas_call(
        functools.partial(_fox_flash_kernel, hp=hp),
        grid_spec=pltpu.PrefetchScalarGridSpec(
            num_scalar_prefetch=1,
            grid=(bsz, nq),
            in_specs=[
                pl.BlockSpec((1, N_HEADS, FOX_TQ, kad), lambda b, i, j0: (b, 0, i, 0)),
                pl.BlockSpec((1, N_HEADS, t, kad), whole, pipeline_mode=pl.Buffered(1)),
                pl.BlockSpec((1, N_HEADS, t, LANES), whole, pipeline_mode=pl.Buffered(1)),
                pl.BlockSpec((FOX_TQ, 256), lambda b, i, j0: (b * nq + i, 11)),
                pl.BlockSpec((1, 256), lambda b, i, j0: (0, 0)),
            ],
            out_specs=pl.BlockSpec((FOX_TQ, 256), lambda b, i, j0: (b * nq + i, 0)),
            scratch_shapes=[pltpu.VMEM((N_HEADS, FOX_TQ, 1), F32), pltpu.VMEM((N_HEADS, FOX_TQ, 1), F32),
                            pltpu.VMEM((N_HEADS, FOX_TQ, LANES), F32)],
        ),
        out_shape=jax.ShapeDtypeStruct((bsz * t, 256), F32),
        compiler_params=pltpu.CompilerParams(
            dimension_semantics=("parallel", "arbitrary"), vmem_limit_bytes=56 * 1024 * 1024),
        name="fox_flash",
    )(j0.reshape(-1), qa, ka, vv, pm, on_row)


FOX_UNDERFLOW = -120.0


def _fox_first_tile(cb, q_norm, k_norm, bsz, nq):
    cb = cb.reshape(bsz, nq, 8, LANES)[:, :, :, FOX_LF_LANE:FOX_LF_LANE + N_HEADS]
    c_first, c_last = cb[:, :, 0], cb[:, :, 1]
    qk_bound = math.sqrt(HEAD_DIM) * jnp.max(jnp.abs(q_norm)) * jnp.max(jnp.abs(k_norm)) * 1.01
    gap = c_first[:, :, None, :] - c_last[:, None, :, :]
    needed = jnp.any(gap >= FOX_UNDERFLOW - 2.0 * qk_bound, axis=-1)
    tiles = jnp.arange(nq)
    needed = needed | (tiles[None, None, :] >= tiles[None, :, None])
    return jnp.sum(jnp.cumsum(needed.astype(jnp.int32), axis=-1) == 0, axis=-1).astype(jnp.int32)


def _fox_prompt(pm, ps, b_f, q_norm, k_norm, out_norm, bsz, t, hp):
    lay = _fox_layout(hp)
    bf_row = jnp.zeros((1, LANES), F32).at[0, FOX_LF_LANE:FOX_LF_LANE + N_HEADS].set(b_f)
    tile4 = lambda g: jnp.tile(g, N_HEADS).reshape(1, 256)
    krow, lf, qa, ka, vv, cb = _fox_prep(pm, ps, bf_row, tile4(q_norm), tile4(k_norm), lay, bsz, t)
    j0 = _fox_first_tile(cb, q_norm, k_norm, bsz, t // FOX_TQ)
    o = _fox_flash(j0, qa, ka, vv, pm, out_norm.reshape(1, 256), bsz, t, hp)
    return o, krow, lf[:, FOX_LF_LANE:FOX_LF_LANE + N_HEADS]


FOX_PAGES_PER_STEP = 8
_UST = np.tril(np.ones((PAGE_SIZE, PAGE_SIZE), np.float32), k=-1)


def _row_to_col(row, eye):
    return jnp.sum(jnp.where(eye, row, 0.0), axis=-1, keepdims=True)


def _col_to_row(col, eye):
    return jnp.sum(jnp.where(eye, col, 0.0), axis=0, keepdims=True)


def _per_head_rows(x8, rows_per_head, width):
    return jnp.concatenate([jnp.broadcast_to(x8[h:h + 1, :], (rows_per_head, width)) for h in range(N_HEADS)],
                           axis=0)


def _fox_dec_kernel(pt_ref, pc_ref, ps_ref, bf_ref, qn_ref, kn_ref, on_ref, blk_ref, ust_ref, *refs):
    npp = FOX_PAGES_PER_STEP
    k_refs, v_refs, lf_refs = refs[0:npp], refs[npp:2 * npp], refs[2 * npp:3 * npp]
    o_ref, krow_ref, lfrow_ref, qb_s, m_s, l_s, acc_s, carry_s = refs[3 * npp:]
    j = pl.program_id(1)
    hrow = lax.broadcasted_iota(jnp.int32, (8, 256), 0)
    lane = lax.broadcasted_iota(jnp.int32, (8, 256), 1)
    hm = (lane >= hrow * HEAD_DIM) & (lane < (hrow + 1) * HEAD_DIM)
    eye = lax.broadcasted_iota(jnp.int32, (256, 256), 0) == lax.broadcasted_iota(jnp.int32, (256, 256), 1)
    r8 = lax.broadcasted_iota(jnp.int32, (8, LANES), 0)
    l8 = lax.broadcasted_iota(jnp.int32, (8, LANES), 1)
    blk = blk_ref[...]

    @pl.when(j == 0)
    def _():
        q = pc_ref[:, 0:256]
        k = pc_ref[:, 256:512]
        v = pc_ref[:, 512:768]
        qn = q * lax.rsqrt(_group_mean(q * q, blk, HEAD_DIM) + EPS) * qn_ref[...]
        kn = k * lax.rsqrt(_group_mean(k * k, blk, HEAD_DIM) + EPS) * kn_ref[...]
        krow_ref[...] = kn
        lf = _log_sigmoid(ps_ref[...] + bf_ref[...])
        lfrow_ref[...] = lf
        q_row = qn[0:1, :] * (HEAD_DIM ** -0.5)
        qb_s[...] = jnp.broadcast_to(_row_to_col(q_row, eye), (256, LANES))
        m_s[...] = jnp.sum(jnp.where(hm, q_row * kn[0:1, :], 0.0), axis=-1, keepdims=True)
        l_s[...] = jnp.ones_like(l_s)
        lane_full = lax.broadcasted_iota(jnp.int32, (256, LANES), 1)
        acc_s[...] = jnp.where(lane_full == 0, _row_to_col(v[0:1, :], eye), 0.0)
        carry_s[...] = jnp.sum(jnp.where(l8 == r8 + FOX_LF_LANE, lf[0:1, :], 0.0), axis=-1, keepdims=True)

    qb = qb_s[...]
    m, l, carry = m_s[...], l_s[...], carry_s[...]
    ust = ust_ref[...]
    scores = []
    for r in reversed(range(npp)):
        prod = k_refs[r][...] * qb
        s = jnp.zeros((8, LANES), F32)
        for h in range(N_HEADS):
            s_h = jnp.sum(prod[h * HEAD_DIM:(h + 1) * HEAD_DIM, :], axis=0, keepdims=True)
            s = s + jnp.where(r8 == h, s_h, 0.0)
        lfp = lf_refs[r][...]
        scores.append(s + _mm_exact_r(lfp, ust) + carry)
        carry = carry + jnp.sum(lfp, axis=-1, keepdims=True)
    m_new = m
    for s in scores:
        m_new = jnp.maximum(m_new, jnp.max(s, axis=-1, keepdims=True))
    alpha = jnp.exp(m - m_new)
    l = alpha * l
    acc = acc_s[...] * _per_head_rows(alpha, HEAD_DIM, 1)
    for idx, r in enumerate(reversed(range(npp))):
        p = jnp.exp(scores[idx] - m_new)
        l = l + jnp.sum(p, axis=-1, keepdims=True)
        acc = acc + _per_head_rows(p, HEAD_DIM, LANES) * v_refs[r][...]
    m_s[...], l_s[...], acc_s[...], carry_s[...] = m_new, l, acc, carry

    @pl.when(j == pl.num_programs(1) - 1)
    def _():
        o = _col_to_row(jnp.sum(acc, axis=-1, keepdims=True), eye)
        l_row = jnp.sum(jnp.where(hm, l, 0.0), axis=0, keepdims=True)
        o = jnp.broadcast_to(o / l_row, (8, 256))
        o = o * lax.rsqrt(_group_mean(o * o, blk, HEAD_DIM) + EPS)
        o_ref[...] = o * on_ref[...] * jax.nn.sigmoid(pc_ref[:, 768:1024])


def _fox_sample(pmr, psr, cache_k, cache_v, cache_logf, page_table, l, b_f, q_norm, k_norm, out_norm):
    bsz, n_pages = page_table.shape
    npp = FOX_PAGES_PER_STEP
    nsteps = n_pages // npp
    depth, n_phys = cache_k.shape[0], cache_k.shape[1]
    ck = jnp.transpose(cache_k, (0, 1, 3, 4, 2)).reshape(depth, n_phys, 256, PAGE_SIZE)
    cv = jnp.transpose(cache_v, (0, 1, 3, 4, 2)).reshape(depth, n_phys, 256, PAGE_SIZE)
    clf = jnp.pad(jnp.swapaxes(cache_logf, 2, 3), ((0, 0), (0, 0), (0, 8 - N_HEADS), (0, 0)))
    bf_row = jnp.zeros((1, LANES), F32).at[0, FOX_LF_LANE:FOX_LF_LANE + N_HEADS].set(b_f)
    tile4 = lambda g: jnp.tile(g, N_HEADS).reshape(1, 256)
    fixed = lambda b, j, pt: (0, 0)

    def page_spec(r, width):
        return pl.BlockSpec((None, None, width[0], width[1]),
                            lambda b, j, pt, r=r: (l, pt[b, (nsteps - 1 - j) * npp + r], 0, 0))

    in_specs = [
        pl.BlockSpec((8, 1024), lambda b, j, pt: (b, 2)),
        pl.BlockSpec((8, LANES), lambda b, j, pt: (b, 0)),
        pl.BlockSpec((1, LANES), fixed),
        pl.BlockSpec((1, 256), fixed),
        pl.BlockSpec((1, 256), fixed),
        pl.BlockSpec((1, 256), fixed),
        pl.BlockSpec((256, 256), fixed),
        pl.BlockSpec((PAGE_SIZE, PAGE_SIZE), fixed),
    ]
    in_specs += [page_spec(r, (256, PAGE_SIZE)) for r in range(npp)]
    in_specs += [page_spec(r, (256, PAGE_SIZE)) for r in range(npp)]
    in_specs += [page_spec(r, (8, PAGE_SIZE)) for r in range(npp)]
    rows = lambda b, j, pt: (b, 0)
    o, krow, lfrow = pl.pallas_call(
        _fox_dec_kernel,
        grid_spec=pltpu.PrefetchScalarGridSpec(
            num_scalar_prefetch=1,
            grid=(bsz, nsteps),
            in_specs=in_specs,
            out_specs=[pl.BlockSpec((8, 256), rows), pl.BlockSpec((8, 256), rows),
                       pl.BlockSpec((8, LANES), rows)],
            scratch_shapes=[pltpu.VMEM((256, LANES), F32), pltpu.VMEM((8, 1), F32), pltpu.VMEM((8, 1), F32),
                            pltpu.VMEM((256, LANES), F32), pltpu.VMEM((8, 1), F32)],
        ),
        out_shape=[jax.ShapeDtypeStruct((bsz * 8, 256), F32), jax.ShapeDtypeStruct((bsz * 8, 256), F32),
                   jax.ShapeDtypeStruct((bsz * 8, LANES), F32)],
        compiler_params=pltpu.CompilerParams(
            dimension_semantics=("parallel", "arbitrary"), vmem_limit_bytes=VMEM_LIMIT),
        name="fox_sample",
    )(page_table, pmr, psr, bf_row, tile4(q_norm), tile4(k_norm), out_norm.reshape(1, 256),
      jnp.asarray(_BLK256, BF16), jnp.asarray(_UST, BF16), *([ck] * npp), *([cv] * npp), *([clf] * npp))
    first = lambda a: a.reshape(bsz, 8, -1)[:, 0]
    return first(o), first(krow), first(lfrow)[:, FOX_LF_LANE:FOX_LF_LANE + N_HEADS]


CHUNK = 128
ROW0 = 8


def _mm(a, b, hp):
    if hp:
        a_hi, a_lo = _split_bf16(a)
        b_hi, b_lo = _split_bf16(b)
        return _dot3(a_hi, a_lo, b_hi, b_lo)
    return _dot(a.astype(BF16), b.astype(BF16))


def _mm_nt(a, b, hp):
    dn = (((1,), (1,)), ((), ()))
    if hp:
        a_hi, a_lo = _split_bf16(a)
        b_hi, b_lo = _split_bf16(b)
        a3 = jnp.concatenate([a_hi, a_lo, a_hi], axis=-1)
        b3 = jnp.concatenate([b_hi, b_hi, b_lo], axis=-1)
        return lax.dot_general(a3, b3, dn, preferred_element_type=F32)
    return lax.dot_general(a.astype(BF16), b.astype(BF16), dn, preferred_element_type=F32)


def _mm_exact(sel, x):
    hi, mid, lo = _split3_bf16(x)
    return _dot(sel, hi) + _dot(sel, mid) + _dot(sel, lo)


def _mm_exact_r(x, sel):
    hi, mid, lo = _split3_bf16(x)
    return _dot(hi, sel) + _dot(mid, sel) + _dot(lo, sel)


def _softplus(x):
    return jnp.maximum(x, 0.0) + jnp.log1p(jnp.exp(-jnp.abs(x)))


def _silu(x):
    return x * jax.nn.sigmoid(x)


def _stage_rows(buf, blk_ref, lo, hi, prev_ref, rows_in, first):
    @pl.when(first)
    def _():
        buf[ROW0 - 3:ROW0, :] = prev_ref[0]
        if rows_in < CHUNK:
            buf[ROW0 + rows_in:ROW0 + CHUNK, :] = jnp.zeros((CHUNK - rows_in, hi - lo), F32)

    @pl.when(jnp.logical_not(first))
    def _():
        buf[ROW0 - 3:ROW0, :] = buf[ROW0 + CHUNK - 3:ROW0 + CHUNK, :]

    buf[ROW0:ROW0 + rows_in, :] = blk_ref[:, lo:hi]


def _conv4(buf, w_ref):
    acc = w_ref[0:1, :] * buf[ROW0 - 3:ROW0 - 3 + CHUNK, :]
    for j in range(1, CONV_WIDTH):
        acc = acc + w_ref[j:j + 1, :] * buf[ROW0 - 3 + j:ROW0 - 3 + j + CHUNK, :]
    return acc


def _pad_rows(x, rows_in):
    if rows_in == CHUNK:
        return x
    return jnp.concatenate([x, jnp.zeros((CHUNK - rows_in, x.shape[1]), x.dtype)], axis=0)


def _head_expand(first_lane, width):
    e = np.zeros((LANES, N_HEADS * width), np.float32)
    for h in range(N_HEADS):
        e[first_lane + h, h * width:(h + 1) * width] = 1.0
    return jnp.asarray(e, BF16)


_TRI = np.tril(np.ones((CHUNK, CHUNK), np.float32))


SSD_DT_LANE = 12


def _ssd_kernel(pd_ref, ps_ref, cprev_ref, s0_ref, w_ref, cb_ref, dtb_ref, alog_ref, dsk_ref, gn_ref,
                e4_ref, tri_ref, y_ref, sfin_ref, buf, s_scr, *, rows_in, t_valid, hp):
    i = pl.program_id(1)
    first = i == 0

    @pl.when(first)
    def _():
        s_scr[...] = s0_ref[0]

    _stage_rows(buf, pd_ref, 256, 1024, cprev_ref, rows_in, first)
    xbc = _silu(_conv4(buf, w_ref) + cb_ref[...])
    xs = xbc[:, 0:256]
    bm = xbc[:, 256:512]
    cm = xbc[:, 512:768]
    z = _pad_rows(pd_ref[:, 0:256], rows_in)
    pre = _mm_exact_r(_pad_rows(ps_ref[...], rows_in), e4_ref[...])
    dt = _softplus(pre + dtb_ref[...])
    tpos = i * CHUNK + lax.broadcasted_iota(jnp.int32, (CHUNK, 1), 0)
    dt = jnp.where(tpos < t_valid, dt, 0.0)
    la = -jnp.exp(alog_ref[...]) * dt
    b = _mm_exact(tri_ref[...], la)
    row = lax.broadcasted_iota(jnp.int32, (CHUNK, CHUNK), 0)
    col = lax.broadcasted_iota(jnp.int32, (CHUNK, CHUNK), 1)
    causal = row >= col
    lane = lax.broadcasted_iota(jnp.int32, (1, 256), 1)
    s_prev = s_scr[...]
    cb = [_mm_nt(cm[:, g * 128:(g + 1) * 128], bm[:, g * 128:(g + 1) * 128], hp) for g in range(D_GROUPS)]
    y = jnp.zeros((CHUNK, 256), F32)
    s_new = jnp.zeros((D_STATE, 256), F32)
    for h in range(N_HEADS):
        g = h // (N_HEADS // D_GROUPS)
        bh = b[:, h * 128:(h + 1) * 128]
        dth = dt[:, h * 128:(h + 1) * 128]
        hmask = (lane >= h * HEAD_DIM) & (lane < (h + 1) * HEAD_DIM)
        dt2 = jnp.concatenate([dth, dth], axis=-1)
        xdt = jnp.where(hmask, xs * dt2, 0.0)
        rel = jnp.where(causal, jnp.exp(jnp.where(causal, bh - bh.T, 0.0)), 0.0)
        y = y + _mm(cb[g] * rel, xdt, hp)
        eb = jnp.exp(bh)
        y = y + _mm(cm[:, g * 128:(g + 1) * 128] * eb, jnp.where(hmask, s_prev, 0.0), hp)
        b_last = bh[CHUNK - 1:CHUNK, :]
        kdec = bm[:, g * 128:(g + 1) * 128] * jnp.exp(b_last - bh)
        s_new = s_new + _mm(kdec.T, xdt, hp)
        a2 = jnp.exp(jnp.concatenate([b_last, b_last], axis=-1))
        s_new = s_new + jnp.where(hmask, a2 * s_prev, 0.0)
    s_scr[...] = s_new
    sfin_ref[0] = s_new
    y = (y + xs * dsk_ref[...]) * _silu(z)
    outs = []
    for g in range(D_GROUPS):
        yg = y[:, g * 128:(g + 1) * 128]
        outs.append(yg * lax.rsqrt(jnp.mean(yg * yg, axis=-1, keepdims=True) + EPS))
    yn = jnp.concatenate(outs, axis=-1) * gn_ref[...]
    y_ref[...] = yn[0:rows_in, :]


def _ssd(pm, ps, conv_prev, s0, conv_w, conv_b, a_log, dt_bias, d_skip, norm_g, bsz, t_valid, rows_in, hp):
    nt = pm.shape[0] // (bsz * rows_in)
    fixed = lambda b, i: (0, 0)
    rows = lambda b, i: (b * nt + i, 0)
    rep128 = lambda v: jnp.repeat(v, 128).reshape(1, 512)
    s0l = jnp.transpose(s0, (0, 2, 1, 3)).reshape(bsz, D_STATE, 256)
    y, sfin = pl.pallas_call(
        functools.partial(_ssd_kernel, rows_in=rows_in, t_valid=t_valid, hp=hp),
        grid=(bsz, nt),
        in_specs=[
            pl.BlockSpec((rows_in, 1024), lambda b, i: (b * nt + i, 3)),
            pl.BlockSpec((rows_in, LANES), rows),
            pl.BlockSpec((1, 3, D_CONV_CH), lambda b, i: (b, 0, 0)),
            pl.BlockSpec((1, D_STATE, 256), lambda b, i: (b, 0, 0)),
            pl.BlockSpec((CONV_WIDTH, D_CONV_CH), fixed),
            pl.BlockSpec((1, D_CONV_CH), fixed),
            pl.BlockSpec((1, 512), fixed),
            pl.BlockSpec((1, 512), fixed),
            pl.BlockSpec((1, 256), fixed),
            pl.BlockSpec((1, 256), fixed),
            pl.BlockSpec((LANES, 512), fixed),
            pl.BlockSpec((CHUNK, CHUNK), fixed),
        ],
        out_specs=[
            pl.BlockSpec((rows_in, 256), rows),
            pl.BlockSpec((1, D_STATE, 256), lambda b, i: (b, 0, 0)),
        ],
        out_shape=[
            jax.ShapeDtypeStruct((pm.shape[0], 256), F32),
            jax.ShapeDtypeStruct((bsz, D_STATE, 256), F32),
        ],
        scratch_shapes=[pltpu.VMEM((ROW0 + CHUNK, D_CONV_CH), F32), pltpu.VMEM((D_STATE, 256), F32)],
        compiler_params=pltpu.CompilerParams(
            dimension_semantics=("parallel", "arbitrary"), vmem_limit_bytes=VMEM_LIMIT),
        name="ssd",
    )(pm, ps, conv_prev, s0l, conv_w, conv_b.reshape(1, -1), rep128(dt_bias), rep128(a_log),
      jnp.repeat(d_skip, HEAD_DIM).reshape(1, 256), norm_g.reshape(1, 256),
      _head_expand(SSD_DT_LANE, 128), jnp.asarray(_TRI, BF16))
    return y, jnp.transpose(sfin.reshape(bsz, D_STATE, N_HEADS, HEAD_DIM), (0, 2, 1, 3))


SUB = CHUNK_A
_SUB_ID = np.arange(CHUNK) // SUB
_SAME_SUB = (_SUB_ID[:, None] == _SUB_ID[None, :]).astype(np.float32)
_BLK256 = (np.arange(256)[:, None] // HEAD_DIM == np.arange(256)[None, :] // HEAD_DIM).astype(np.float32)


def _mm_tn(a, b, hp):
    dn = (((0,), (0,)), ((), ()))
    if hp:
        a_hi, a_lo = _split_bf16(a)
        b_hi, b_lo = _split_bf16(b)
        a3 = jnp.concatenate([a_hi, a_lo, a_hi], axis=0)
        b3 = jnp.concatenate([b_hi, b_hi, b_lo], axis=0)
        return lax.dot_general(a3, b3, dn, preferred_element_type=F32)
    return lax.dot_general(a.astype(BF16), b.astype(BF16), dn, preferred_element_type=F32)


def _group_sum(x, ones_blk, hp):
    if hp:
        return _mm_exact_r(x, ones_blk)
    return _dot(x.astype(BF16), ones_blk)


def _hgrn_kernel(pa_ref, s0_ref, c1_ref, c2_ref, oml_ref, gn_ref, blk_ref, t16_ref, l16_ref,
                 o_ref, sfin_ref, kbuf, vbuf, lbuf, st_scr, *, rows_in, t_valid, hp):
    i = pl.program_id(1)

    @pl.when(i == 0)
    def _():
        st_scr[...] = s0_ref[0]
        zeros = jnp.zeros((SUB, 256), F32)
        kbuf[0:SUB, :] = zeros
        vbuf[0:SUB, :] = zeros
        lbuf[0:SUB, :] = zeros

    q = _pad_rows(pa_ref[:, 0:256], rows_in) * (HEAD_DIM ** -0.5)
    zf = _pad_rows(pa_ref[:, 256:512], rows_in)
    v = _pad_rows(pa_ref[:, 512:768], rows_in)
    g = _pad_rows(pa_ref[:, 768:1024], rows_in)
    la = c1_ref[...]
    lb = c2_ref[...] + _log_sigmoid(zf)
    lf = jnp.maximum(la, lb) + jnp.log1p(jnp.exp(-jnp.abs(la - lb)))
    k = oml_ref[...] * jax.nn.sigmoid(-zf)
    tpos = i * CHUNK + lax.broadcasted_iota(jnp.int32, (CHUNK, 1), 0)
    valid = tpos < t_valid
    lf = jnp.where(valid, lf, 0.0)
    k = jnp.where(valid, k, 0.0)
    v = jnp.where(valid, v, 0.0)
    kbuf[SUB:SUB + CHUNK, :] = k
    vbuf[SUB:SUB + CHUNK, :] = v
    lbuf[SUB:SUB + CHUNK, :] = lf

    blk = blk_ref[...]
    sub = lax.broadcasted_iota(jnp.int32, (CHUNK, 1), 0) % SUB
    o = _group_sum(q * k, blk, hp) * v
    bd = jnp.zeros((CHUNK, 256), F32)
    for d in range(1, SUB):
        bd = bd + lbuf[SUB - d + 1:SUB - d + 1 + CHUNK, :]
        m = sub >= d
        ks = kbuf[SUB - d:SUB - d + CHUNK, :]
        tmp = jnp.where(m, q * ks * jnp.exp(jnp.where(m, bd, 0.0)), 0.0)
        o = o + _group_sum(tmp, blk, hp) * vbuf[SUB - d:SUB - d + CHUNK, :]

    b = _mm_exact(t16_ref[...], lf)
    bl = _mm_exact(l16_ref[...], lf)
    qe = q * jnp.exp(b)
    kd = k * jnp.exp(bl - b)
    st = st_scr[...]
    bdmask = blk > 0
    o_rows = []
    for n in range(CHUNK // SUB):
        r = slice(n * SUB, (n + 1) * SUB)
        o_rows.append(_mm_nt(qe[r], st, hp))
        ds = _mm_tn(v[r], kd[r], hp)
        st = st * jnp.exp(bl[n * SUB:n * SUB + 1, :]) + jnp.where(bdmask, ds, 0.0)
    st_scr[...] = st
    sfin_ref[0] = st
    o = o + jnp.concatenate(o_rows, axis=0)
    o = o * lax.rsqrt(_group_mean(o * o, blk, HEAD_DIM) + EPS) * gn_ref[...] * _silu(g)
    o_ref[...] = o[0:rows_in, :]


def _hgrn(pm, lb, norm_g, s0, bsz, t_valid, rows_in, hp):
    nt = pm.shape[0] // (bsz * rows_in)
    fixed = lambda b, i: (0, 0)
    rows = lambda b, i: (b * nt + i, 0)
    lb = jnp.clip(lb, 0.0, LB_CEIL)
    c1 = jnp.log(jnp.maximum(lb, LB_FLOOR)).reshape(1, 256)
    c2 = jnp.log1p(-lb).reshape(1, 256)
    oml = (1.0 - lb).reshape(1, 256)
    eye = jnp.eye(N_HEADS, dtype=bool)[None, :, None, :, None]
    st0 = jnp.where(eye, jnp.swapaxes(s0, 2, 3)[:, :, :, None, :], 0.0).reshape(bsz, 256, 256)
    o, sfin = pl.pallas_call(
        functools.partial(_hgrn_kernel, rows_in=rows_in, t_valid=t_valid, hp=hp),
        grid=(bsz, nt),
        in_specs=[
            pl.BlockSpec((rows_in, 1024), lambda b, i: (b * nt + i, 0)),
            pl.BlockSpec((1, 256, 256), lambda b, i: (b, 0, 0)),
            pl.BlockSpec((1, 256), fixed),
            pl.BlockSpec((1, 256), fixed),
            pl.BlockSpec((1, 256), fixed),
            pl.BlockSpec((1, 256), fixed),
            pl.BlockSpec((256, 256), fixed),
            pl.BlockSpec((CHUNK, CHUNK), fixed),
            pl.BlockSpec((CHUNK, CHUNK), fixed),
        ],
        out_specs=[
            pl.BlockSpec((rows_in, 256), rows),
            pl.BlockSpec((1, 256, 256), lambda b, i: (b, 0, 0)),
        ],
        out_shape=[
            jax.ShapeDtypeStruct((pm.shape[0], 256), F32),
            jax.ShapeDtypeStruct((bsz, 256, 256), F32),
        ],
        scratch_shapes=[pltpu.VMEM((SUB + CHUNK, 256), F32)] * 3 + [pltpu.VMEM((256, 256), F32)],
        compiler_params=pltpu.CompilerParams(
            dimension_semantics=("parallel", "arbitrary"), vmem_limit_bytes=VMEM_LIMIT),
        name="hgrn",
    )(pm, st0, c1, c2, oml, norm_g.reshape(1, 256), jnp.asarray(_BLK256, BF16),
      jnp.asarray(_TRI * _SAME_SUB, BF16), jnp.asarray(_SAME_SUB, BF16))
    sf = sfin.reshape(bsz, N_HEADS, HEAD_DIM, N_HEADS, HEAD_DIM)
    sf = jnp.stack([sf[:, h, :, h, :] for h in range(N_HEADS)], axis=1)
    return o, jnp.swapaxes(sf, 2, 3)


GDN_BETA_LANE = 0
GDN_DT_LANE = 4


def _block_diag_state(s0, bsz):
    eye = jnp.eye(N_HEADS, dtype=bool)[None, :, None, :, None]
    return jnp.where(eye, jnp.swapaxes(s0, 2, 3)[:, :, :, None, :], 0.0).reshape(bsz, 256, 256)


def _unblock_diag_state(st, bsz):
    sf = st.reshape(bsz, N_HEADS, HEAD_DIM, N_HEADS, HEAD_DIM)
    sf = jnp.stack([sf[:, h, :, h, :] for h in range(N_HEADS)], axis=1)
    return jnp.swapaxes(sf, 2, 3)


def _gdn_kernel(pb_ref, ps_ref, cprev_ref, s0_ref, w_ref, dtb_ref, alog_ref, dtb64_ref, alog64_ref, gn_ref,
                blk_ref, tri_ref, eb128_ref, ed128_ref, eb64_ref, ed64_ref,
                o_ref, sfin_ref, buf, st_scr, *, rows_in, t_valid, hp):
    i = pl.program_id(1)
    first = i == 0

    @pl.when(first)
    def _():
        st_scr[...] = s0_ref[0]

    _stage_rows(buf, pb_ref, 0, 768, cprev_ref, rows_in, first)
    qkv = _silu(_conv4(buf, w_ref))
    blk = blk_ref[...]
    q = qkv[:, 0:256]
    k = qkv[:, 256:512]
    v = qkv[:, 512:768]
    q = q * lax.rsqrt(_group_mean(q * q, blk, 1) + EPS) * (HEAD_DIM ** -0.5)
    k = k * lax.rsqrt(_group_mean(k * k, blk, 1) + EPS)
    gate = _pad_rows(pb_ref[:, 768:1024], rows_in)
    ps = _pad_rows(ps_ref[...], rows_in)
    tpos = i * CHUNK + lax.broadcasted_iota(jnp.int32, (CHUNK, 1), 0)
    valid = tpos < t_valid
    tri = tri_ref[...]
    beta128 = jnp.where(valid, jax.nn.sigmoid(_mm_exact_r(ps, eb128_ref[...])), 0.0)
    la128 = jnp.where(valid, -jnp.exp(alog_ref[...]) * _softplus(_mm_exact_r(ps, ed128_ref[...]) + dtb_ref[...]), 0.0)
    b128 = _mm_exact(tri, la128)
    beta64 = jnp.where(valid, jax.nn.sigmoid(_mm_exact_r(ps, eb64_ref[...])), 0.0)
    la64 = jnp.where(valid, -jnp.exp(alog64_ref[...]) * _softplus(_mm_exact_r(ps, ed64_ref[...]) + dtb64_ref[...]), 0.0)
    b64 = _mm_exact(tri, la64)
    eb64 = jnp.exp(b64)
    b_last64 = b64[CHUNK - 1:CHUNK, :]
    kb = k * beta64
    rv = v * beta64
    rk = kb * eb64
    q_dec = q * eb64
    k_dec = k * jnp.exp(b_last64 - b64)

    row = lax.broadcasted_iota(jnp.int32, (CHUNK, CHUNK), 0)
    col = lax.broadcasted_iota(jnp.int32, (CHUNK, CHUNK), 1)
    causal = row >= col
    strict = row > col
    lane = lax.broadcasted_iota(jnp.int32, (1, 256), 1)
    a_mats, rhs, atts = [], [], []
    for h in range(N_HEADS):
        hs = slice(h * HEAD_DIM, (h + 1) * HEAD_DIM)
        hmask = (lane >= h * HEAD_DIM) & (lane < (h + 1) * HEAD_DIM)
        bh = b128[:, h * 128:(h + 1) * 128]
        decay = jnp.where(causal, jnp.exp(jnp.where(causal, bh - bh.T, 0.0)), 0.0)
        k_h = jnp.where(hmask, k, 0.0)
        a_mats.append(-jnp.where(strict, _mm_nt(jnp.where(hmask, kb, 0.0), k_h, True) * decay, 0.0))
        atts.append(_mm_nt(jnp.where(hmask, q, 0.0), k_h, hp) * decay)
        rhs.append(jnp.concatenate([rv[:, hs], rk[:, hs]], axis=-1))
    zero = jnp.zeros((CHUNK, CHUNK), F32)
    xs = []
    for pair in range(N_HEADS // 2):
        h0, h1 = 2 * pair, 2 * pair + 1
        a = jnp.concatenate([jnp.concatenate([a_mats[h0], zero], axis=-1),
                             jnp.concatenate([zero, a_mats[h1]], axis=-1)], axis=0)
        x = jnp.concatenate([rhs[h0], rhs[h1]], axis=0)
        x = x + _mm(a, x, True)
        for _ in range(6):
            a = _mm(a, a, True)
            x = x + _mm(a, x, True)
        xs += [x[:CHUNK], x[CHUNK:]]
    u_all = jnp.concatenate([x[:, :HEAD_DIM] for x in xs], axis=-1)
    w_all = jnp.concatenate([x[:, HEAD_DIM:] for x in xs], axis=-1)
    st = st_scr[...]
    v_new = u_all - _mm_nt(w_all, st, hp)
    o = _mm_nt(q_dec, st, hp)
    for h in range(N_HEADS):
        hmask = (lane >= h * HEAD_DIM) & (lane < (h + 1) * HEAD_DIM)
        o = o + _mm(atts[h], jnp.where(hmask, v_new, 0.0), hp)
    st = st * jnp.exp(b_last64) + jnp.where(blk > 0, _mm_tn(v_new, k_dec, hp), 0.0)
    st_scr[...] = st
    sfin_ref[0] = st
    o = o * lax.rsqrt(_group_mean(o * o, blk, HEAD_DIM) + EPS) * gn_ref[...] * _silu(gate)
    o_ref[...] = o[0:rows_in, :]


def _gdn(pm, ps, conv_prev, s0, conv_w, a_log, dt_bias, norm_g, bsz, t_valid, rows_in, hp):
    nt = pm.shape[0] // (bsz * rows_in)
    fixed = lambda b, i: (0, 0)
    rows = lambda b, i: (b * nt + i, 0)
    rep = lambda v, w: jnp.repeat(v, w).reshape(1, N_HEADS * w)
    o, sfin = pl.pallas_call(
        functools.partial(_gdn_kernel, rows_in=rows_in, t_valid=t_valid, hp=hp),
        grid=(bsz, nt),
        in_specs=[
            pl.BlockSpec((rows_in, 1024), lambda b, i: (b * nt + i, 1)),
            pl.BlockSpec((rows_in, LANES), rows),
            pl.BlockSpec((1, 3, B_CONV_CH), lambda b, i: (b, 0, 0)),
            pl.BlockSpec((1, 256, 256), lambda b, i: (b, 0, 0)),
            pl.BlockSpec((CONV_WIDTH, B_CONV_CH), fixed),
            pl.BlockSpec((1, 512), fixed),
            pl.BlockSpec((1, 512), fixed),
            pl.BlockSpec((1, 256), fixed),
            pl.BlockSpec((1, 256), fixed),
            pl.BlockSpec((1, 256), fixed),
            pl.BlockSpec((256, 256), fixed),
            pl.BlockSpec((CHUNK, CHUNK), fixed),
            pl.BlockSpec((LANES, 512), fixed),
            pl.BlockSpec((LANES, 512), fixed),
            pl.BlockSpec((LANES, 256), fixed),
            pl.BlockSpec((LANES, 256), fixed),
        ],
        out_specs=[
            pl.BlockSpec((rows_in, 256), rows),
            pl.BlockSpec((1, 256, 256), lambda b, i: (b, 0, 0)),
        ],
        out_shape=[
            jax.ShapeDtypeStruct((pm.shape[0], 256), F32),
            jax.ShapeDtypeStruct((bsz, 256, 256), F32),
        ],
        scratch_shapes=[pltpu.VMEM((ROW0 + CHUNK, B_CONV_CH), F32), pltpu.VMEM((256, 256), F32)],
        compiler_params=pltpu.CompilerParams(
            dimension_semantics=("parallel", "arbitrary"), vmem_limit_bytes=VMEM_LIMIT),
        name="gdn",
    )(pm, ps, conv_prev, _block_diag_state(s0, bsz), conv_w, rep(dt_bias, 128), rep(a_log, 128),
      rep(dt_bias, 64), rep(a_log, 64), norm_g.reshape(1, 256), jnp.asarray(_BLK256, BF16),
      jnp.asarray(_TRI, BF16), _head_expand(GDN_BETA_LANE, 128), _head_expand(GDN_DT_LANE, 128),
      _head_expand(GDN_BETA_LANE, 64), _head_expand(GDN_DT_LANE, 64))
    return o, _unblock_diag_state(sfin, bsz)


def _head_rmsnorm(x, g):
    y = x * lax.rsqrt(jnp.mean(x * x, axis=-1, keepdims=True) + EPS)
    return y * g.reshape((-1, x.shape[-1]))


def _l2norm(x):
    return x * lax.rsqrt(jnp.sum(x * x, axis=-1, keepdims=True) + EPS)


def _masked_exp(logit, mask):
    return jnp.where(mask, jnp.exp(jnp.where(mask, logit, 0.0)), 0.0)


def _causal_conv(x, w, prev):
    t = x.shape[1]
    xp = jnp.concatenate([prev.astype(x.dtype), x], axis=1)
    y = xp[:, 0:t] * w[0]
    for j in range(1, w.shape[0]):
        y = y + xp[:, j:j + t] * w[j]
    return y, xp[:, t:]


def _pad_time(t, pad):
    return jnp.pad(t, [(0, 0), (0, pad)] + [(0, 0)] * (t.ndim - 2))


def _to_chunks(t, chunk):
    b, tt, h = t.shape[:3]
    t = t.reshape((b, tt // chunk, chunk, h) + t.shape[3:])
    return jnp.moveaxis(t, 3, 1)


def _from_chunks(t):
    b, h, n, c = t.shape[:4]
    t = jnp.moveaxis(t, 1, 3)
    return t.reshape((b, n * c, h) + t.shape[4:])


def _chunk_inputs(arrs, chunk):
    t = arrs[0].shape[1]
    pad = (-t) % chunk
    return [_to_chunks(_pad_time(a.astype(F32), pad), chunk) for a in arrs]


def _chunked_gla(q, k, v, log_f, s0, chunk):
    t = q.shape[1]
    q, k, v, log_f = _chunk_inputs([q, k, v, log_f], chunk)
    b = jnp.cumsum(log_f, axis=3)
    causal = jnp.tril(jnp.ones((chunk, chunk), bool))[:, :, None]
    rel = _masked_exp(b[..., :, None, :] - b[..., None, :, :], causal)
    att = jnp.einsum('bhntk,bhnsk,bhntsk->bhnts', q, k, rel)
    o_intra = jnp.einsum('bhnts,bhnsv->bhntv', att, v)
    b_last = b[..., -1:, :]
    ds = jnp.einsum('bhnsk,bhnsv->nbhkv', k * jnp.exp(b_last - b), v)
    a_chunk = jnp.moveaxis(jnp.exp(b_last[..., 0, :]), 2, 0)

    def step(s, inp):
        a, d = inp
        return a[..., None] * s + d, s

    s_final, s_prev = lax.scan(step, s0.astype(F32), (a_chunk, ds))
    o_inter = jnp.einsum('bhntk,nbhkv->bhntv', q * jnp.exp(b), s_prev)
    return _from_chunks(o_intra + o_inter)[:, :t], s_final


def _chunked_ssd(q, k, v, log_a, s0, chunk):
    t = q.shape[1]
    q, k, v, la = _chunk_inputs([q, k, v, log_a], chunk)
    b = jnp.cumsum(la, axis=-1)
    causal = jnp.tril(jnp.ones((chunk, chunk), bool))
    rel = _masked_exp(b[..., :, None] - b[..., None, :], causal)
    att = jnp.einsum('bhntk,bhnsk->bhnts', q, k) * rel
    o_intra = jnp.einsum('bhnts,bhnsv->bhntv', att, v)
    b_last = b[..., -1:]
    ds = jnp.einsum('bhnsk,bhnsv->nbhkv', k * jnp.exp(b_last - b)[..., None], v)
    a_chunk = jnp.moveaxis(jnp.exp(b_last[..., 0]), 2, 0)

    def step(s, inp):
        a, d = inp
        return a[..., None, None] * s + d, s

    s_final, s_prev = lax.scan(step, s0.astype(F32), (a_chunk, ds))
    o_inter = jnp.einsum('bhntk,nbhkv->bhntv', q * jnp.exp(b)[..., None], s_prev)
    return _from_chunks(o_intra + o_inter)[:, :t], s_final


def _chunked_gated_delta(q, k, v, beta, log_a, s0, chunk):
    t = q.shape[1]
    vd = v.shape[-1]
    q, k, v, beta, la = _chunk_inputs([q, k, v, beta, log_a], chunk)
    b = jnp.cumsum(la, axis=-1)
    causal = jnp.tril(jnp.ones((chunk, chunk), bool))
    strict = jnp.tril(jnp.ones((chunk, chunk), bool), k=-1)
    decay = _masked_exp(b[..., :, None] - b[..., None, :], causal)
    kb = k * beta[..., None]
    m = jnp.where(strict, jnp.einsum('bhntk,bhnsk->bhnts', kb, k) * decay, 0.0)
    rhs = jnp.concatenate([v * beta[..., None], kb * jnp.exp(b)[..., None]], axis=-1)
    sol = lax.linalg.triangular_solve(m + jnp.eye(chunk, dtype=F32), rhs, left_side=True,
                                      lower=True, unit_diagonal=True)
    u, w = sol[..., :vd], sol[..., vd:]
    att = jnp.einsum('bhntk,bhnsk->bhnts', q, k) * decay
    q_dec = q * jnp.exp(b)[..., None]
    k_dec = k * jnp.exp(b[..., -1:] - b)[..., None]
    a_chunk = jnp.exp(b[..., -1])
    xs = tuple(jnp.moveaxis(a, 2, 0) for a in (u, w, att, q_dec, k_dec, a_chunk))

    def step(s, inp):
        u_c, w_c, att_c, qd_c, kd_c, a_c = inp
        v_new = u_c - jnp.einsum('bhtk,bhkv->bhtv', w_c, s)
        o = jnp.einsum('bhtk,bhkv->bhtv', qd_c, s) + jnp.einsum('bhts,bhsv->bhtv', att_c, v_new)
        s = a_c[..., None, None] * s + jnp.einsum('bhsk,bhsv->bhkv', kd_c, v_new)
        return s, o

    s_final, o = lax.scan(step, s0.astype(F32), xs)
    return _from_chunks(jnp.moveaxis(o, 0, 2))[:, :t], s_final


def _hgrn2_mixer(pa, lb, norm_g, s0):
    bsz, t, _ = pa.shape
    w = GROUP_WIDTH
    q, zf, inp, g = (pa[..., j * w:(j + 1) * w] for j in range(4))
    lb = jnp.clip(lb, 0.0, LB_CEIL)
    log_f = jnp.logaddexp(jnp.log(jnp.maximum(lb, LB_FLOOR)), jnp.log1p(-lb) + jax.nn.log_sigmoid(zf))
    k = (1.0 - lb) * jax.nn.sigmoid(-zf)
    heads = lambda a: a.reshape(bsz, t, N_HEADS, -1)
    o, s = _chunked_gla(heads(q) * HEAD_DIM ** -0.5, heads(k), heads(inp), heads(log_f), s0, CHUNK_A)
    o = _head_rmsnorm(o, norm_g) * jax.nn.silu(heads(g))
    return o.reshape(bsz, t, w), s


def _gdn_mixer(pb, beta_pre, dt_pre, conv_w, conv_prev, a_log, dt_bias, norm_g, s0):
    bsz, t, _ = pb.shape
    w = GROUP_WIDTH
    qkv, conv_new = _causal_conv(pb[..., :3 * w], conv_w, conv_prev)
    qkv = jax.nn.silu(qkv)
    heads = lambda a: a.reshape(bsz, t, N_HEADS, -1)
    q = _l2norm(heads(qkv[..., :w])) * HEAD_DIM ** -0.5
    k = _l2norm(heads(qkv[..., w:2 * w]))
    v = heads(qkv[..., 2 * w:3 * w])
    gate = heads(pb[..., 3 * w:4 * w])
    beta = jax.nn.sigmoid(beta_pre)
    log_a = -jnp.exp(a_log) * jax.nn.softplus(dt_pre + dt_bias)
    o, s = _chunked_gated_delta(q, k, v, beta, log_a, s0, CHUNK_B)
    o = _head_rmsnorm(o, norm_g) * jax.nn.silu(gate)
    return o.reshape(bsz, t, w), s, conv_new


def _fox_attention_prompt(q, k, v, log_f):
    b, t, h, d = q.shape
    n_blk = -(-t // Q_BLOCK)
    c = jnp.moveaxis(jnp.cumsum(log_f, axis=1), 2, 1)
    key_pos = jnp.arange(t)

    def block(i):
        start = i * Q_BLOCK
        q_i = lax.dynamic_slice_in_dim(q, start, Q_BLOCK, axis=1)
        c_i = lax.dynamic_slice_in_dim(c, start, Q_BLOCK, axis=2)
        s = jnp.einsum('bqhd,bkhd->bhqk', q_i, k) * (d ** -0.5) + c_i[..., :, None] - c[:, :, None, :]
        allowed = (start + jnp.arange(Q_BLOCK))[:, None] >= key_pos[None, :]
        p = jax.nn.softmax(jnp.where(allowed, s, NEG_BIG), axis=-1)
        return jnp.einsum('bhqk,bkhd->bqhd', p, v)

    o = lax.map(block, jnp.arange(n_blk))
    return jnp.moveaxis(o, 0, 1).reshape(b, n_blk * Q_BLOCK, h, d)[:, :t]


def _fox_attention_sample(q, k, v, log_f, k_past, v_past, logf_past):
    n_past = k_past.shape[1]
    s_new = q.shape[1]
    d = q.shape[-1]
    k_all = jnp.concatenate([k_past, k], axis=1)
    v_all = jnp.concatenate([v_past, v], axis=1)
    c = jnp.cumsum(jnp.concatenate([logf_past, log_f], axis=1), axis=1)
    c = jnp.moveaxis(c, 2, 1)
    s = jnp.einsum('bqhd,bkhd->bhqk', q, k_all) * (d ** -0.5) + c[:, :, n_past:, None] - c[:, :, None, :]
    allowed = (n_past + jnp.arange(s_new))[:, None] >= jnp.arange(n_past + s_new)[None, :]
    p = jax.nn.softmax(jnp.where(allowed, s, NEG_BIG), axis=-1)
    return jnp.einsum('bhqk,bkhd->bqhd', p, v_all)


def _fox_mixer(pc, lf_pre, b_f, q_norm, k_norm, out_norm, past):
    bsz, t, _ = pc.shape
    w = GROUP_WIDTH
    heads = lambda a: a.reshape(bsz, t, N_HEADS, HEAD_DIM)
    q = _head_rmsnorm(heads(pc[..., :w]), q_norm)
    k = _head_rmsnorm(heads(pc[..., w:2 * w]), k_norm)
    v = heads(pc[..., 2 * w:3 * w])
    g = heads(pc[..., 3 * w:4 * w])
    log_f = jax.nn.log_sigmoid(lf_pre + b_f)
    if past is None:
        o = _fox_attention_prompt(q, k, v, log_f)
    else:
        o = _fox_attention_sample(q, k, v, log_f, past[0], past[1], past[2])
    o = _head_rmsnorm(o, out_norm) * jax.nn.sigmoid(g)
    return o.reshape(bsz, t, w), k, v, log_f


def _ssd_mixer(pd, dt_pre, conv_w, conv_b, conv_prev, a_log, dt_bias, d_skip, norm_g, s0):
    bsz, t, _ = pd.shape
    w = GROUP_WIDTH
    gn = D_GROUPS * D_STATE
    rep = N_HEADS // D_GROUPS
    z = pd[..., :w].reshape(bsz, t, N_HEADS, HEAD_DIM)
    xbc, conv_new = _causal_conv(pd[..., w:w + D_CONV_CH], conv_w, conv_prev)
    xbc = jax.nn.silu(xbc + conv_b)
    xs = xbc[..., :w].reshape(bsz, t, N_HEADS, HEAD_DIM)
    bm = jnp.repeat(xbc[..., w:w + gn].reshape(bsz, t, D_GROUPS, D_STATE), rep, axis=2)
    cm = jnp.repeat(xbc[..., w + gn:].reshape(bsz, t, D_GROUPS, D_STATE), rep, axis=2)
    dt = jax.nn.softplus(dt_pre + dt_bias)
    log_a = -jnp.exp(a_log) * dt
    y, s = _chunked_ssd(cm, bm, xs * dt[..., None], log_a, s0, CHUNK_D)
    y = (y + xs * d_skip[:, None]) * jax.nn.silu(z)
    y = _head_rmsnorm(y.reshape(bsz, t, D_GROUPS, -1), norm_g)
    return y.reshape(bsz, t, w), s, conv_new


def _gather_pages(pool, page_table):
    g = pool[page_table]
    return g.reshape((g.shape[0], g.shape[1] * g.shape[2]) + g.shape[3:])


def _prep_weights(prm):
    w_in = prm['w_in']
    wm = w_in[:, :, _MAIN_COLS]
    ws = jnp.pad(w_in[:, :, _SMALL_COLS], ((0, 0), (0, 0), (0, LANES - len(_SMALL_COLS))))
    wr = jnp.concatenate([prm['moe_w_expert'], prm['moe_w_group']], axis=-1)
    n_r = N_EXPERTS + N_EXPERT_GROUPS
    wr = jnp.pad(wr, ((0, 0), (0, 0), (0, LANES - n_r)))
    both = lambda pair: [(pair[0][:D_MODEL], pair[1][:D_MODEL]), (pair[0][D_MODEL:], pair[1][D_MODEL:])]
    ws_split = both(_split_weight(ws.reshape(DEPTH * D_MODEL, LANES)))
    wr_split = both(_split_weight(wr.reshape(DEPTH * D_MODEL, LANES)))
    br = jnp.pad(jnp.concatenate([prm['moe_b_expert'], prm['moe_b_group']], axis=-1),
                 ((0, 0), (0, LANES - n_r)))[:, None, :]
    bf = lambda a: a.astype(BF16)
    prompt, sample = [], []
    for l in range(DEPTH):
        common = dict(ws=ws_split[l], wr=wr_split[l], br=br[l])
        if l == 0:
            wm_p, wo_p = tuple(_split_weight(wm[l])), tuple(_split_weight(prm['w_out'][l]))
        else:
            wm_p, wo_p = (bf(wm[l]),), (bf(prm['w_out'][l]),)
        prompt.append(dict(common, wm=wm_p, wo=wo_p, w1=bf(prm['moe_w1'][l]), w3=bf(prm['moe_w3'][l]),
                           w2=bf(prm['moe_w2'][l]), wg=bf(prm['ple_w_gate'][l]), wp=bf(prm['ple_w_proj'][l])))
        sample.append(dict(common, wm=(wm[l],), wo=(prm['w_out'][l],), w1=prm['moe_w1'][l], w3=prm['moe_w3'][l],
                           w2=prm['moe_w2'][l], wg=prm['ple_w_gate'][l], wp=prm['ple_w_proj'][l]))
    return prompt, sample


def _trunk(x, p, init_state, fox_cache, lb_all, prm, wts, tm, hp_layers):
    s_hgrn0, s_gdn0, c_gdn0, s_ssd0, c_ssd0 = init_state
    bsz, t, _ = x.shape
    n = bsz * t
    h = x.reshape(n, D_MODEL)
    outs = [[] for _ in range(8)]
    row = lambda a: a.reshape(1, -1)
    for l in range(DEPTH):
        hp_mix = hp_layers[l]
        w = wts[l]
        pm, ps = _inproj(h, row(prm['g_mix'][l]), w['wm'], w['ws'][0], w['ws'][1], tm)
        if fox_cache is None:
            o_c, k_c, lf_c = _fox_prompt(pm, ps, prm['fox_b_f'][l], prm['fox_q_norm'][l], prm['fox_k_norm'][l],
                                         prm['fox_out_norm'][l], bsz, t, hp_mix)
            o_c = o_c.reshape(bsz, t, GROUP_WIDTH)
            k_c = k_c.reshape(bsz, t, N_HEADS, HEAD_DIM)
            lf_c = lf_c.reshape(bsz, t, N_HEADS)
            v_c = pm[:, 2560:2816].reshape(bsz, t, N_HEADS, HEAD_DIM)
        rows_in = CHUNK if t % CHUNK == 0 else 8
        if rows_in == CHUNK:
            pmr, psr = pm, ps
        else:
            padr = lambda a: jnp.pad(a.reshape(bsz, t, -1), ((0, 0), (0, rows_in - t), (0, 0))).reshape(
                bsz * rows_in, -1)
            pmr, psr = padr(pm), padr(ps)
        unpad = lambda a: a.reshape(bsz, -1, GROUP_WIDTH)[:, :t]
        o_a, s_a = _hgrn(pmr, lb_all[l], prm['hgrn_norm'][l], s_hgrn0[l], bsz, t, rows_in, hp_mix)
        o_d, s_d = _ssd(pmr, psr, c_ssd0[l], s_ssd0[l], prm['ssd_conv_w'][l], prm['ssd_conv_b'][l],
                        prm['ssd_a_log'][l], prm['ssd_dt_bias'][l], prm['ssd_d'][l], prm['ssd_norm'][l],
                        bsz, t, rows_in, hp_mix)
        o_b, s_b = _gdn(pmr, psr, c_gdn0[l], s_gdn0[l], prm['gdn_conv_w'][l], prm['gdn_a_log'][l],
                        prm['gdn_dt_bias'][l], prm['gdn_norm'][l], bsz, t, rows_in, hp_mix)
        o_a, o_b, o_d = unpad(o_a), unpad(o_b), unpad(o_d)
        pm = pm.reshape(bsz, t, N_MAIN)
        ps = ps.reshape(bsz, t, LANES)
        c_d = jnp.concatenate([c_ssd0[l], pm[:, max(0, t - 3):, 3328:4096]], axis=1)[:, -(CONV_WIDTH - 1):]
        c_b = jnp.concatenate([c_gdn0[l], pm[:, max(0, t - 3):, 1024:1792]], axis=1)[:, -(CONV_WIDTH - 1):]
        if fox_cache is not None:
            cache_k, cache_v, cache_logf, page_table = fox_cache
            o_c, k_c, lf_c = _fox_sample(pmr, psr, cache_k, cache_v, cache_logf, page_table, l,
                                         prm['fox_b_f'][l], prm['fox_q_norm'][l], prm['fox_k_norm'][l],
                                         prm['fox_out_norm'][l])
            o_c = o_c.reshape(bsz, t, GROUP_WIDTH)
            k_c = k_c.reshape(bsz, t, N_HEADS, HEAD_DIM)
            lf_c = lf_c.reshape(bsz, t, N_HEADS)
            v_c = pm[..., 2560:2816].reshape(bsz, t, N_HEADS, HEAD_DIM)
        mix = jnp.concatenate([o_a, o_b, o_c, o_d], axis=-1).reshape(n, D_MODEL)
        h2, u2, cw = _outproj(h, mix, w['wo'], row(prm['g_ffn'][l]), w['wr'][0], w['wr'][1], w['br'], tm)
        y = _moe(u2, cw, w['w1'], w['w3'], w['w2'], tm)
        h = _ple(h2, y, p[l].reshape(n, PLE_DIM), row(prm['g_ple'][l]), w['wg'], w['wp'],
                 row(prm['g_final']), tm, final=(l == DEPTH - 1))
        for acc, val in zip(outs, (k_c, v_c, lf_c, s_a, s_b, c_b, s_d, c_d)):
            acc.append(val)
    return (h.reshape(bsz, t, D_MODEL),) + tuple(jnp.stack(acc) for acc in outs)


def _hgrn_lower_bounds(lb_param):
    sm = jax.nn.softmax(lb_param, axis=0)
    return jnp.concatenate([jnp.zeros_like(sm[:1]), jnp.cumsum(sm[1:], axis=0)], axis=0)


def kernel(x_prompt, x_sample, cache_fox_k, cache_fox_v, cache_fox_logf, state_hgrn, state_gdn,
           state_gdn_conv, state_ssd, state_ssd_conv, page_table, p_prompt, p_sample, w_in, w_out,
           g_mix, g_ffn, g_ple, g_final, hgrn_lb, hgrn_norm, gdn_conv_w, gdn_a_log, gdn_dt_bias,
           gdn_norm, fox_b_f, fox_q_norm, fox_k_norm, fox_out_norm, ssd_conv_w, ssd_conv_b, ssd_a_log,
           ssd_dt_bias, ssd_d, ssd_norm, moe_w_group, moe_b_group, moe_w_expert, moe_b_expert, moe_w1,
           moe_w3, moe_w2, ple_w_gate, ple_w_proj):
    prm = dict(w_in=w_in, w_out=w_out, g_mix=g_mix, g_ffn=g_ffn, g_ple=g_ple, g_final=g_final,
               hgrn_norm=hgrn_norm, gdn_conv_w=gdn_conv_w, gdn_a_log=gdn_a_log, gdn_dt_bias=gdn_dt_bias,
               gdn_norm=gdn_norm, fox_b_f=fox_b_f, fox_q_norm=fox_q_norm, fox_k_norm=fox_k_norm,
               fox_out_norm=fox_out_norm, ssd_conv_w=ssd_conv_w, ssd_conv_b=ssd_conv_b, ssd_a_log=ssd_a_log,
               ssd_dt_bias=ssd_dt_bias, ssd_d=ssd_d, ssd_norm=ssd_norm, moe_w_group=moe_w_group,
               moe_b_group=moe_b_group, moe_w_expert=moe_w_expert, moe_b_expert=moe_b_expert,
               moe_w1=moe_w1, moe_w3=moe_w3, moe_w2=moe_w2, ple_w_gate=ple_w_gate, ple_w_proj=ple_w_proj)
    wts_prompt, wts_sample = _prep_weights(prm)
    lb_all = _hgrn_lower_bounds(hgrn_lb)
    bp = x_prompt.shape[0]
    zero_state = (jnp.zeros((DEPTH, bp, N_HEADS, HEAD_DIM, HEAD_DIM), F32),
                  jnp.zeros((DEPTH, bp, N_HEADS, HEAD_DIM, HEAD_DIM), F32),
                  jnp.zeros((DEPTH, bp, CONV_WIDTH - 1, B_CONV_CH), F32),
                  jnp.zeros((DEPTH, bp, N_HEADS, D_STATE, HEAD_DIM), F32),
                  jnp.zeros((DEPTH, bp, CONV_WIDTH - 1, D_CONV_CH), F32))
    pr = _trunk(x_prompt, p_prompt, zero_state, None, lb_all, prm, wts_prompt, tm=512, hp_layers=(True, False))
    sm = _trunk(x_sample, p_sample, (state_hgrn, state_gdn, state_gdn_conv, state_ssd, state_ssd_conv),
                (cache_fox_k, cache_fox_v, cache_fox_logf, page_table), lb_all, prm, wts_sample, tm=32,
                hp_layers=(True, True))
    return (pr[0], sm[0]) + tuple(pr[1:]) + tuple(sm[1:])
```

```python
import functools
import math

import jax
import jax.numpy as jnp
import numpy as np
from jax import lax
from jax.experimental import pallas as pl
from jax.experimental.pallas import tpu as pltpu

F32 = jnp.float32
BF16 = jnp.bfloat16

D_MODEL = 1024
DEPTH = 2
PAGE_SIZE = 128
EPS = 1e-6
NEG_BIG = -1e30
LB_FLOOR = 1e-30
LB_CEIL = 1.0 - 1e-6
PLE_DIM = 256
GROUP_WIDTH = 256
HEAD_DIM = 64
N_HEADS = 4
D_GROUPS = 2
D_STATE = 128
CONV_WIDTH = 4
B_CONV_CH = 3 * GROUP_WIDTH
D_CONV_CH = GROUP_WIDTH + 2 * D_GROUPS * D_STATE
SIZE_A = 4 * GROUP_WIDTH
SIZE_B = 4 * GROUP_WIDTH + 2 * N_HEADS
SIZE_C = 4 * GROUP_WIDTH + N_HEADS
SIZE_D = GROUP_WIDTH + D_CONV_CH + N_HEADS
OFF_B = SIZE_A
OFF_C = OFF_B + SIZE_B
OFF_D = OFF_C + SIZE_C
N_IN = OFF_D + SIZE_D
CHUNK_A = 16
CHUNK_B = 64
CHUNK_D = 128
Q_BLOCK = 128
N_EXPERT_GROUPS = 4
EXPERTS_PER_GROUP = 4
N_EXPERTS = 16
D_EXPERT = 512

LANES = 128
N_MAIN = 4096
VMEM_LIMIT = 48 * 1024 * 1024

_MAIN_COLS = np.concatenate([
    np.arange(0, SIZE_A),
    np.arange(OFF_B, OFF_B + 4 * GROUP_WIDTH),
    np.arange(OFF_C, OFF_C + 4 * GROUP_WIDTH),
    np.arange(OFF_D, OFF_D + 4 * GROUP_WIDTH),
])
_SMALL_COLS = np.concatenate([
    np.arange(OFF_B + 4 * GROUP_WIDTH, OFF_B + SIZE_B),
    np.arange(OFF_C + 4 * GROUP_WIDTH, OFF_C + SIZE_C),
    np.arange(OFF_D + 4 * GROUP_WIDTH, OFF_D + SIZE_D),
])


def _split_bf16(x):
    hi = x.astype(BF16)
    lo = (x - hi.astype(F32)).astype(BF16)
    return hi, lo


def _dot(a, b):
    return jnp.dot(a, b, preferred_element_type=F32)


def _dot3(a_hi, a_lo, b_hi, b_lo):
    return _dot(a_hi, b_hi) + _dot(a_lo, b_hi) + _dot(a_hi, b_lo)


def _rms(x, g):
    return x * lax.rsqrt(jnp.mean(x * x, axis=-1, keepdims=True) + EPS) * g


def _wmode(w):
    if len(w) == 2:
        return 'x3'
    return 'f32' if w[0].dtype == F32 else 'x1'


def _wcount(mode):
    return 2 if mode == 'x3' else 1


def _wload(refs, mode, idx=None):
    get = (lambda r: r[...]) if idx is None else (lambda r: r[idx])
    if mode == 'x1':
        return get(refs[0]), None
    if mode == 'x3':
        return get(refs[0]), get(refs[1])
    return _split_bf16(get(refs[0]))


def _wdot(a, refs, mode, idx=None):
    w_hi, w_lo = _wload(refs, mode, idx)
    if w_lo is None:
        return _dot(a.astype(BF16), w_hi)
    a_hi, a_lo = _split_bf16(a)
    return _dot3(a_hi, a_lo, w_hi, w_lo)


def _split_kernel(w_ref, hi_ref, lo_ref):
    hi, lo = _split_bf16(w_ref[...])
    hi_ref[...] = hi
    lo_ref[...] = lo


def _split_weight(w):
    rows, cols = w.shape
    spec = pl.BlockSpec((256, cols), lambda i: (i, 0))
    return pl.pallas_call(
        _split_kernel,
        grid=(rows // 256,),
        in_specs=[spec],
        out_specs=[spec, spec],
        out_shape=[jax.ShapeDtypeStruct((rows, cols), BF16)] * 2,
        compiler_params=pltpu.CompilerParams(dimension_semantics=("parallel",)),
        name="split_weight",
    )(w)


def _inproj_kernel(x_ref, g_ref, *refs, mode):
    nw = _wcount(mode)
    wm_refs = refs[:nw]
    wsh_ref, wsl_ref, om_ref, os_ref, uh_ref, ul_ref = refs[nw:]

    @pl.when(pl.program_id(1) == 0)
    def _():
        u = _rms(x_ref[...], g_ref[...])
        uh, ul = _split_bf16(u)
        uh_ref[...] = uh
        ul_ref[...] = ul
        os_ref[...] = _dot3(uh, ul, wsh_ref[...], wsl_ref[...])

    w_hi, w_lo = _wload(wm_refs, mode)
    if w_lo is None:
        om_ref[...] = _dot(uh_ref[...], w_hi)
    else:
        om_ref[...] = _dot3(uh_ref[...], ul_ref[...], w_hi, w_lo)


def _inproj(x, g, wm, wsh, wsl, tm):
    n = x.shape[0]
    tn = 1024
    return pl.pallas_call(
        functools.partial(_inproj_kernel, mode=_wmode(wm)),
        grid=(n // tm, N_MAIN // tn),
        in_specs=[
            pl.BlockSpec((tm, D_MODEL), lambda i, j: (i, 0)),
            pl.BlockSpec((1, D_MODEL), lambda i, j: (0, 0)),
        ] + [pl.BlockSpec((D_MODEL, tn), lambda i, j: (0, j))] * len(wm) + [
            pl.BlockSpec((D_MODEL, LANES), lambda i, j: (0, 0)),
            pl.BlockSpec((D_MODEL, LANES), lambda i, j: (0, 0)),
        ],
        out_specs=[
            pl.BlockSpec((tm, tn), lambda i, j: (i, j)),
            pl.BlockSpec((tm, LANES), lambda i, j: (i, 0)),
        ],
        out_shape=[
            jax.ShapeDtypeStruct((n, N_MAIN), F32),
            jax.ShapeDtypeStruct((n, LANES), F32),
        ],
        scratch_shapes=[pltpu.VMEM((tm, D_MODEL), BF16), pltpu.VMEM((tm, D_MODEL), BF16)],
        compiler_params=pltpu.CompilerParams(
            dimension_semantics=("parallel", "arbitrary"), vmem_limit_bytes=VMEM_LIMIT),
        name="inproj",
    )(x, g, *wm, wsh, wsl)


def _route(logits):
    lane = lax.broadcasted_iota(jnp.int32, logits.shape, 1)
    gmask = (lane >= N_EXPERTS) & (lane < N_EXPERTS + N_EXPERT_GROUPS)
    gl = jnp.where(gmask, logits, -jnp.inf)
    gmax = jnp.max(gl, axis=-1, keepdims=True)
    gidx = jnp.min(jnp.where(gl == gmax, lane, 4 * LANES), axis=-1, keepdims=True) - N_EXPERTS
    gw = 1.0 / jnp.sum(jnp.where(gmask, jnp.exp(gl - gmax), 0.0), axis=-1, keepdims=True)
    lo = gidx * EXPERTS_PER_GROUP
    emask = (lane >= lo) & (lane < lo + EXPERTS_PER_GROUP)
    el = jnp.where(emask, logits, -jnp.inf)
    m1 = jnp.max(el, axis=-1, keepdims=True)
    i1 = jnp.min(jnp.where(el == m1, lane, 4 * LANES), axis=-1, keepdims=True)
    el2 = jnp.where(lane == i1, -jnp.inf, el)
    m2 = jnp.max(el2, axis=-1, keepdims=True)
    i2 = jnp.min(jnp.where(el2 == m2, lane, 4 * LANES), axis=-1, keepdims=True)
    e2 = jnp.exp(m2 - m1)
    den = 1.0 + e2
    g1 = gw / den
    g2 = gw * e2 / den
    return jnp.where(lane == i1, g1, jnp.where(lane == i2, g2, 0.0))


def _outproj_kernel(h_ref, mix_ref, *refs, mode):
    nw = _wcount(mode)
    g_ref, wrh_ref, wrl_ref, br_ref, h2_ref, u2_ref, cw_ref = refs[nw:]
    h2 = h_ref[...] + _wdot(mix_ref[...], refs[:nw], mode)
    h2_ref[...] = h2
    u = _rms(h2, g_ref[...])
    u2_ref[...] = u
    uh, ul = _split_bf16(u)
    logits = _dot3(uh, ul, wrh_ref[...], wrl_ref[...]) + br_ref[...]
    cw_ref[...] = _route(logits)


def _outproj(h, mix, wo, g, wrh, wrl, br, tm):
    n = h.shape[0]
    row = lambda i: (i, 0)
    fixed = lambda i: (0, 0)
    return pl.pallas_call(
        functools.partial(_outproj_kernel, mode=_wmode(wo)),
        grid=(n // tm,),
        in_specs=[
            pl.BlockSpec((tm, D_MODEL), row),
            pl.BlockSpec((tm, D_MODEL), row),
        ] + [pl.BlockSpec((D_MODEL, D_MODEL), fixed)] * len(wo) + [
            pl.BlockSpec((1, D_MODEL), fixed),
            pl.BlockSpec((D_MODEL, LANES), fixed),
            pl.BlockSpec((D_MODEL, LANES), fixed),
            pl.BlockSpec((1, LANES), fixed),
        ],
        out_specs=[
            pl.BlockSpec((tm, D_MODEL), row),
            pl.BlockSpec((tm, D_MODEL), row),
            pl.BlockSpec((tm, LANES), row),
        ],
        out_shape=[
            jax.ShapeDtypeStruct((n, D_MODEL), F32),
            jax.ShapeDtypeStruct((n, D_MODEL), F32),
            jax.ShapeDtypeStruct((n, LANES), F32),
        ],
        compiler_params=pltpu.CompilerParams(
            dimension_semantics=("parallel",), vmem_limit_bytes=VMEM_LIMIT),
        name="outproj",
    )(h, mix, *wo, g, wrh, wrl, br)


def _moe_kernel(x_ref, cw_ref, w1_ref, w3_ref, w2_ref, y_ref, xh_ref, xl_ref, *, mode):
    e = pl.program_id(1)

    @pl.when(e == 0)
    def _():
        y_ref[...] = jnp.zeros_like(y_ref)
        xh, xl = _split_bf16(x_ref[...])
        xh_ref[...] = xh
        xl_ref[...] = xl

    cw = cw_ref[...]
    lane = lax.broadcasted_iota(jnp.int32, cw.shape, 1)
    col = jnp.sum(jnp.where(lane == e, cw, 0.0), axis=-1, keepdims=True)
    xh = xh_ref[...]
    w1_hi, w1_lo = _wload((w1_ref,), mode, 0)
    w3_hi, w3_lo = _wload((w3_ref,), mode, 0)
    if w1_lo is None:
        a = _dot(xh, w1_hi)
        b = _dot(xh, w3_hi)
    else:
        xl = xl_ref[...]
        a = _dot3(xh, xl, w1_hi, w1_lo)
        b = _dot3(xh, xl, w3_hi, w3_lo)
    hid = (a * jax.nn.sigmoid(a)) * b
    y_ref[...] += col * _wdot(hid, (w2_ref,), mode, 0)


def _moe(u2, cw, w1, w3, w2, tm):
    n = u2.shape[0]
    mode = 'f32' if w1.dtype == F32 else 'x1'
    return pl.pallas_call(
        functools.partial(_moe_kernel, mode=mode),
        grid=(n // tm, N_EXPERTS),
        in_specs=[
            pl.BlockSpec((tm, D_MODEL), lambda i, e: (i, 0)),
            pl.BlockSpec((tm, LANES), lambda i, e: (i, 0)),
            pl.BlockSpec((1, D_MODEL, D_EXPERT), lambda i, e: (e, 0, 0)),
            pl.BlockSpec((1, D_MODEL, D_EXPERT), lambda i, e: (e, 0, 0)),
            pl.BlockSpec((1, D_EXPERT, D_MODEL), lambda i, e: (e, 0, 0)),
        ],
        out_specs=pl.BlockSpec((tm, D_MODEL), lambda i, e: (i, 0)),
        out_shape=jax.ShapeDtypeStruct((n, D_MODEL), F32),
        scratch_shapes=[pltpu.VMEM((tm, D_MODEL), BF16), pltpu.VMEM((tm, D_MODEL), BF16)],
        compiler_params=pltpu.CompilerParams(
            dimension_semantics=("parallel", "arbitrary"), vmem_limit_bytes=VMEM_LIMIT),
        name="moe",
    )(u2, cw, w1, w3, w2)


def _ple_kernel(h_ref, y_ref, p_ref, g_ref, wg_ref, wp_ref, gf_ref, o_ref, *, final, mode):
    h3 = h_ref[...] + y_ref[...]
    u = _rms(h3, g_ref[...])
    gate = jax.nn.sigmoid(_wdot(u, (wg_ref,), mode))
    h4 = h3 + gate * _wdot(p_ref[...], (wp_ref,), mode)
    if final:
        h4 = _rms(h4, gf_ref[...])
    o_ref[...] = h4


def _ple(h2, y, p, g, wg, wp, gf, tm, final):
    n = h2.shape[0]
    row = lambda i: (i, 0)
    fixed = lambda i: (0, 0)
    mode = 'f32' if wg.dtype == F32 else 'x1'
    return pl.pallas_call(
        functools.partial(_ple_kernel, final=final, mode=mode),
        grid=(n // tm,),
        in_specs=[
            pl.BlockSpec((tm, D_MODEL), row),
            pl.BlockSpec((tm, D_MODEL), row),
            pl.BlockSpec((tm, PLE_DIM), row),
            pl.BlockSpec((1, D_MODEL), fixed),
            pl.BlockSpec((D_MODEL, D_MODEL), fixed),
            pl.BlockSpec((PLE_DIM, D_MODEL), fixed),
            pl.BlockSpec((1, D_MODEL), fixed),
        ],
        out_specs=pl.BlockSpec((tm, D_MODEL), row),
        out_shape=jax.ShapeDtypeStruct((n, D_MODEL), F32),
        compiler_params=pltpu.CompilerParams(
            dimension_semantics=("parallel",), vmem_limit_bytes=VMEM_LIMIT),
        name="ple",
    )(h2, y, p, g, wg, wp, gf)


FOX_LF_LANE = 8
FOX_TB = 256
FOX_TQ = 256


def _split3_bf16(x):
    hi = x.astype(BF16)
    r = x - hi.astype(F32)
    mid = r.astype(BF16)
    lo = (r - mid.astype(F32)).astype(BF16)
    return hi, mid, lo


def _log_sigmoid(x):
    return jnp.minimum(x, 0.0) - jnp.log1p(jnp.exp(-jnp.abs(x)))


def _group_mean(x2, ones_blk, width):
    hi, mid, lo = _split3_bf16(x2)
    return (_dot(hi, ones_blk) + _dot(mid, ones_blk) + _dot(lo, ones_blk)) * (1.0 / width)


def _fox_layout(hp):
    ka = 256 if hp else 128
    a0 = 192 if hp else 64
    nx = 512 + 3 * LANES
    mq = np.zeros((nx, 4 * ka), np.float32)
    mk = np.zeros((nx, 4 * ka), np.float32)
    rq = np.zeros((1, 4 * ka), np.float32)
    rk = np.zeros((1, 4 * ka), np.float32)
    mv = np.zeros((512, 4 * LANES), np.float32)
    for h in range(N_HEADS):
        for d in range(HEAD_DIM):
            src_hi, src_lo = 64 * h + d, 256 + 64 * h + d
            mq[src_hi, h * ka + d] = 1.0
            mk[src_hi, h * ka + d] = 1.0
            if hp:
                mq[src_lo, h * ka + 64 + d] = 1.0
                mq[src_hi, h * ka + 128 + d] = 1.0
                mk[src_hi, h * ka + 64 + d] = 1.0
                mk[src_lo, h * ka + 128 + d] = 1.0
            mv[src_hi, h * LANES + d] = 1.0
            if hp:
                mv[src_lo, h * LANES + 64 + d] = 1.0
        for part in range(3):
            src = 512 + part * LANES + FOX_LF_LANE + h
            mq[src, h * ka + a0 + part] = 1.0
            mk[src, h * ka + a0 + 3 + part] = -1.0
            rq[0, h * ka + a0 + 3 + part] = 1.0
            rk[0, h * ka + a0 + part] = 1.0
    blk = (np.arange(256)[:, None] // HEAD_DIM == np.arange(256)[None, :] // HEAD_DIM).astype(np.float32)
    tri = np.tril(np.ones((FOX_TB, FOX_TB), np.float32))
    as_bf = lambda a: jnp.asarray(a, BF16)
    return dict(ka=ka, mq=as_bf(mq), mk=as_bf(mk), mv=as_bf(mv), rq=jnp.asarray(rq), rk=jnp.asarray(rk),
                blk=as_bf(blk), tri=as_bf(tri))


def _fox_prep_kernel(pc_ref, ps_ref, bf_ref, qn_ref, kn_ref, blk_ref, tri_ref, mq_ref, mk_ref, mv_ref,
                     rq_ref, rk_ref, krow_ref, lf_ref, qa_ref, ka_ref, vv_ref, cb_ref, carry_ref, *, ka):
    @pl.when(pl.program_id(1) == 0)
    def _():
        carry_ref[...] = jnp.zeros_like(carry_ref)

    blk = blk_ref[...]
    q = pc_ref[:, 0:256]
    k = pc_ref[:, 256:512]
    v = pc_ref[:, 512:768]
    qn = q * lax.rsqrt(_group_mean(q * q, blk, HEAD_DIM) + EPS) * qn_ref[...]
    kn = k * lax.rsqrt(_group_mean(k * k, blk, HEAD_DIM) + EPS) * kn_ref[...]
    krow_ref[...] = kn
    lf = _log_sigmoid(ps_ref[...] + bf_ref[...])
    lf_ref[...] = lf
    l_hi, l_mid, l_lo = _split3_bf16(lf)
    tri = tri_ref[...]
    c = _dot(tri, l_hi) + _dot(tri, l_mid) + _dot(tri, l_lo) + carry_ref[...]
    carry_ref[...] = c[FOX_TB - 1:FOX_TB, :]
    r8 = lax.broadcasted_iota(jnp.int32, (8, LANES), 0)
    cb_ref[...] = jnp.where(r8 == 0, c[0:1, :], jnp.where(r8 == 1, c[FOX_TB - 1:FOX_TB, :], 0.0))
    c_hi, c_mid, c_lo = _split3_bf16(c)
    q_hi, q_lo = _split_bf16(qn * (HEAD_DIM ** -0.5))
    k_hi, k_lo = _split_bf16(kn)
    xq = jnp.concatenate([q_hi, q_lo, c_hi, c_mid, c_lo], axis=-1)
    xk = jnp.concatenate([k_hi, k_lo, c_hi, c_mid, c_lo], axis=-1)
    qa = (_dot(xq, mq_ref[...]) + rq_ref[...]).astype(BF16)
    kk = (_dot(xk, mk_ref[...]) + rk_ref[...]).astype(BF16)
    v_hi, v_lo = _split_bf16(v)
    vv = _dot(jnp.concatenate([v_hi, v_lo], axis=-1), mv_ref[...]).astype(BF16)
    for h in range(N_HEADS):
        qa_ref[0, h] = qa[:, h * ka:(h + 1) * ka]
        ka_ref[0, h] = kk[:, h * ka:(h + 1) * ka]
        vv_ref[0, h] = vv[:, h * LANES:(h + 1) * LANES]


def _fox_prep(pm, ps, bf_row, qn_row, kn_row, lay, bsz, t):
    ka = lay['ka']
    nt = t // FOX_TB
    fixed = lambda b, i: (0, 0)
    rows = lambda b, i: (b * nt + i, 0)
    hm = lambda b, i: (b, 0, i, 0)
    return pl.pallas_call(
        functools.partial(_fox_prep_kernel, ka=ka),
        grid=(bsz, nt),
        in_specs=[
            pl.BlockSpec((FOX_TB, 1024), lambda b, i: (b * nt + i, 2)),
            pl.BlockSpec((FOX_TB, LANES), rows),
            pl.BlockSpec((1, LANES), fixed),
            pl.BlockSpec((1, 256), fixed),
            pl.BlockSpec((1, 256), fixed),
            pl.BlockSpec((256, 256), fixed),
            pl.BlockSpec((FOX_TB, FOX_TB), fixed),
            pl.BlockSpec(lay['mq'].shape, fixed),
            pl.BlockSpec(lay['mk'].shape, fixed),
            pl.BlockSpec(lay['mv'].shape, fixed),
            pl.BlockSpec((1, 4 * ka), fixed),
            pl.BlockSpec((1, 4 * ka), fixed),
        ],
        out_specs=[
            pl.BlockSpec((FOX_TB, 256), rows),
            pl.BlockSpec((FOX_TB, LANES), rows),
            pl.BlockSpec((1, N_HEADS, FOX_TB, ka), hm),
            pl.BlockSpec((1, N_HEADS, FOX_TB, ka), hm),
            pl.BlockSpec((1, N_HEADS, FOX_TB, LANES), hm),
            pl.BlockSpec((8, LANES), rows),
        ],
        out_shape=[
            jax.ShapeDtypeStruct((bsz * t, 256), F32),
            jax.ShapeDtypeStruct((bsz * t, LANES), F32),
            jax.ShapeDtypeStruct((bsz, N_HEADS, t, ka), BF16),
            jax.ShapeDtypeStruct((bsz, N_HEADS, t, ka), BF16),
            jax.ShapeDtypeStruct((bsz, N_HEADS, t, LANES), BF16),
            jax.ShapeDtypeStruct((bsz * nt * 8, LANES), F32),
        ],
        scratch_shapes=[pltpu.VMEM((1, LANES), F32)],
        compiler_params=pltpu.CompilerParams(
            dimension_semantics=("parallel", "arbitrary"), vmem_limit_bytes=VMEM_LIMIT),
        name="fox_prep",
    )(pm, ps, bf_row, qn_row, kn_row, lay['blk'], lay['tri'], lay['mq'], lay['mk'], lay['mv'],
      lay['rq'], lay['rk'])


def _fox_flash_kernel(j0_ref, qa_ref, ka_ref, vv_ref, g_ref, on_ref, o_ref, m_s, l_s, acc_s, *, hp):
    i = pl.program_id(1)
    j_first = j0_ref[pl.program_id(0) * pl.num_programs(1) + i]
    tq = FOX_TQ
    row = lax.broadcasted_iota(jnp.int32, (tq, tq), 0)
    col = lax.broadcasted_iota(jnp.int32, (tq, tq), 1)
    m_s[...] = jnp.full(m_s.shape, NEG_BIG, F32)
    l_s[...] = jnp.zeros(l_s.shape, F32)
    acc_s[...] = jnp.zeros(acc_s.shape, F32)

    def tile(j, masked):
        start = pl.multiple_of(j * tq, tq)
        for h in range(N_HEADS):
            kt = ka_ref[0, h, pl.ds(start, tq), :]
            s = lax.dot_general(qa_ref[0, h], kt, (((1,), (1,)), ((), ())), preferred_element_type=F32)
            if masked:
                s = jnp.where(row >= col, s, NEG_BIG)
            m = m_s[h]
            m_new = jnp.maximum(m, jnp.max(s, axis=-1, keepdims=True))
            alpha = jnp.exp(m - m_new)
            p = jnp.exp(s - m_new)
            l_s[h] = alpha * l_s[h] + jnp.sum(p, axis=-1, keepdims=True)
            m_s[h] = m_new
            vt = vv_ref[0, h, pl.ds(start, tq), :]
            if hp:
                p_hi, p_lo = _split_bf16(p)
                pv = _dot(p_hi, vt) + _dot(p_lo, vt)
            else:
                pv = _dot(p.astype(BF16), vt)
            acc_s[h] = alpha * acc_s[h] + pv

    def body(jj, carry):
        tile(j_first + jj, False)
        return carry

    lax.fori_loop(0, i - j_first, body, 0)
    tile(i, True)
    outs = []
    for h in range(N_HEADS):
        acc = acc_s[h]
        o = (acc[:, :HEAD_DIM] + acc[:, HEAD_DIM:]) / l_s[h]
        o = o * lax.rsqrt(jnp.mean(o * o, axis=-1, keepdims=True) + EPS)
        sl = slice(h * HEAD_DIM, (h + 1) * HEAD_DIM)
        outs.append(o * on_ref[:, sl] * jax.nn.sigmoid(g_ref[:, sl]))
    o_ref[...] = jnp.concatenate(outs, axis=-1)


def _fox_flash(j0, qa, ka, vv, pm, on_row, bsz, t, hp):
    kad = qa.shape[-1]
    nq = t // FOX_TQ
    whole = lambda b, i, j0: (b, 0, 0, 0)
    return pl.pallas_call(
        functools.partial(_fox_flash_kernel, hp=hp),
        grid_spec=pltpu.PrefetchScalarGridSpec(
            num_scalar_prefetch=1,
            grid=(bsz, nq),
            in_specs=[
                pl.BlockSpec((1, N_HEADS, FOX_TQ, kad), lambda b, i, j0: (b, 0, i, 0)),
                pl.BlockSpec((1, N_HEADS, t, kad), whole, pipeline_mode=pl.Buffered(1)),
                pl.BlockSpec((1, N_HEADS, t, LANES), whole, pipeline_mode=pl.Buffered(1)),
                pl.BlockSpec((FOX_TQ, 256), lambda b, i, j0: (b * nq + i, 11)),
                pl.BlockSpec((1, 256), lambda b, i, j0: (0, 0)),
            ],
            out_specs=pl.BlockSpec((FOX_TQ, 256), lambda b, i, j0: (b * nq + i, 0)),
            scratch_shapes=[pltpu.VMEM((N_HEADS, FOX_TQ, 1), F32), pltpu.VMEM((N_HEADS, FOX_TQ, 1), F32),
                            pltpu.VMEM((N_HEADS, FOX_TQ, LANES), F32)],
        ),
        out_shape=jax.ShapeDtypeStruct((bsz * t, 256), F32),
        compiler_params=pltpu.CompilerParams(
            dimension_semantics=("parallel", "arbitrary"), vmem_limit_bytes=56 * 1024 * 1024),
        name="fox_flash",
    )(j0.reshape(-1), qa, ka, vv, pm, on_row)


FOX_UNDERFLOW = -120.0


def _fox_first_tile(cb, q_norm, k_norm, bsz, nq):
    cb = cb.reshape(bsz, nq, 8, LANES)[:, :, :, FOX_LF_LANE:FOX_LF_LANE + N_HEADS]
    c_first, c_last = cb[:, :, 0], cb[:, :, 1]
    qk_bound = math.sqrt(HEAD_DIM) * jnp.max(jnp.abs(q_norm)) * jnp.max(jnp.abs(k_norm)) * 1.01
    gap = c_first[:, :, None, :] - c_last[:, None, :, :]
    needed = jnp.any(gap >= FOX_UNDERFLOW - 2.0 * qk_bound, axis=-1)
    tiles = jnp.arange(nq)
    needed = needed | (tiles[None, None, :] >= tiles[None, :, None])
    return jnp.sum(jnp.cumsum(needed.astype(jnp.int32), axis=-1) == 0, axis=-1).astype(jnp.int32)


def _fox_prompt(pm, ps, b_f, q_norm, k_norm, out_norm, bsz, t, hp):
    lay = _fox_layout(hp)
    bf_row = jnp.zeros((1, LANES), F32).at[0, FOX_LF_LANE:FOX_LF_LANE + N_HEADS].set(b_f)
    tile4 = lambda g: jnp.tile(g, N_HEADS).reshape(1, 256)
    krow, lf, qa, ka, vv, cb = _fox_prep(pm, ps, bf_row, tile4(q_norm), tile4(k_norm), lay, bsz, t)
    j0 = _fox_first_tile(cb, q_norm, k_norm, bsz, t // FOX_TQ)
    o = _fox_flash(j0, qa, ka, vv, pm, out_norm.reshape(1, 256), bsz, t, hp)
    return o, krow, lf[:, FOX_LF_LANE:FOX_LF_LANE + N_HEADS]


FOX_PAGES_PER_STEP = 8
_UST = np.tril(np.ones((PAGE_SIZE, PAGE_SIZE), np.float32), k=-1)


def _row_to_col(row, eye):
    return jnp.sum(jnp.where(eye, row, 0.0), axis=-1, keepdims=True)


def _col_to_row(col, eye):
    return jnp.sum(jnp.where(eye, col, 0.0), axis=0, keepdims=True)


def _per_head_rows(x8, rows_per_head, width):
    return jnp.concatenate([jnp.broadcast_to(x8[h:h + 1, :], (rows_per_head, width)) for h in range(N_HEADS)],
                           axis=0)


def _fox_dec_kernel(pt_ref, pc_ref, ps_ref, bf_ref, qn_ref, kn_ref, on_ref, blk_ref, ust_ref, *refs):
    npp = FOX_PAGES_PER_STEP
    k_refs, v_refs, lf_refs = refs[0:npp], refs[npp:2 * npp], refs[2 * npp:3 * npp]
    o_ref, krow_ref, lfrow_ref, qb_s, m_s, l_s, acc_s, carry_s = refs[3 * npp:]
    j = pl.program_id(1)
    hrow = lax.broadcasted_iota(jnp.int32, (8, 256), 0)
    lane = lax.broadcasted_iota(jnp.int32, (8, 256), 1)
    hm = (lane >= hrow * HEAD_DIM) & (lane < (hrow + 1) * HEAD_DIM)
    eye = lax.broadcasted_iota(jnp.int32, (256, 256), 0) == lax.broadcasted_iota(jnp.int32, (256, 256), 1)
    r8 = lax.broadcasted_iota(jnp.int32, (8, LANES), 0)
    l8 = lax.broadcasted_iota(jnp.int32, (8, LANES), 1)
    blk = blk_ref[...]

    @pl.when(j == 0)
    def _():
        q = pc_ref[:, 0:256]
        k = pc_ref[:, 256:512]
        v = pc_ref[:, 512:768]
        qn = q * lax.rsqrt(_group_mean(q * q, blk, HEAD_DIM) + EPS) * qn_ref[...]
        kn = k * lax.rsqrt(_group_mean(k * k, blk, HEAD_DIM) + EPS) * kn_ref[...]
        krow_ref[...] = kn
        lf = _log_sigmoid(ps_ref[...] + bf_ref[...])
        lfrow_ref[...] = lf
        q_row = qn[0:1, :] * (HEAD_DIM ** -0.5)
        qb_s[...] = jnp.broadcast_to(_row_to_col(q_row, eye), (256, LANES))
        m_s[...] = jnp.sum(jnp.where(hm, q_row * kn[0:1, :], 0.0), axis=-1, keepdims=True)
        l_s[...] = jnp.ones_like(l_s)
        lane_full = lax.broadcasted_iota(jnp.int32, (256, LANES), 1)
        acc_s[...] = jnp.where(lane_full == 0, _row_to_col(v[0:1, :], eye), 0.0)
        carry_s[...] = jnp.sum(jnp.where(l8 == r8 + FOX_LF_LANE, lf[0:1, :], 0.0), axis=-1, keepdims=True)

    qb = qb_s[...]
    m, l, carry = m_s[...], l_s[...], carry_s[...]
    lf_all = jnp.concatenate([lf_refs[r][...] for r in range(npp)], axis=0)
    suffix = _mm_exact_r(lf_all, ust_ref[...])
    totals = jnp.sum(lf_all, axis=-1, keepdims=True)
    scores = []
    for r in reversed(range(npp)):
        prod = k_refs[r][...] * qb
        s = jnp.zeros((8, LANES), F32)
        for h in range(N_HEADS):
            s_h = jnp.sum(prod[h * HEAD_DIM:(h + 1) * HEAD_DIM, :], axis=0, keepdims=True)
            s = s + jnp.where(r8 == h, s_h, 0.0)
        scores.append(s + suffix[8 * r:8 * r + 8, :] + carry)
        carry = carry + totals[8 * r:8 * r + 8, :]
    m_new = m
    for s in scores:
        m_new = jnp.maximum(m_new, jnp.max(s, axis=-1, keepdims=True))
    alpha = jnp.exp(m - m_new)
    l = alpha * l
    acc = acc_s[...] * _per_head_rows(alpha, HEAD_DIM, 1)
    for idx, r in enumerate(reversed(range(npp))):
        p = jnp.exp(scores[idx] - m_new)
        l = l + jnp.sum(p, axis=-1, keepdims=True)
        acc = acc + _per_head_rows(p, HEAD_DIM, LANES) * v_refs[r][...]
    m_s[...], l_s[...], acc_s[...], carry_s[...] = m_new, l, acc, carry

    @pl.when(j == pl.num_programs(1) - 1)
    def _():
        o = _col_to_row(jnp.sum(acc, axis=-1, keepdims=True), eye)
        l_row = jnp.sum(jnp.where(hm, l, 0.0), axis=0, keepdims=True)
        o = jnp.broadcast_to(o / l_row, (8, 256))
        o = o * lax.rsqrt(_group_mean(o * o, blk, HEAD_DIM) + EPS)
        o_ref[...] = o * on_ref[...] * jax.nn.sigmoid(pc_ref[:, 768:1024])


def _fox_sample(pmr, psr, cache_k, cache_v, cache_logf, page_table, l, b_f, q_norm, k_norm, out_norm):
    bsz, n_pages = page_table.shape
    npp = FOX_PAGES_PER_STEP
    nsteps = n_pages // npp
    depth, n_phys = cache_k.shape[0], cache_k.shape[1]
    ck = jnp.transpose(cache_k, (0, 1, 3, 4, 2)).reshape(depth, n_phys, 256, PAGE_SIZE)
    cv = jnp.transpose(cache_v, (0, 1, 3, 4, 2)).reshape(depth, n_phys, 256, PAGE_SIZE)
    clf = jnp.pad(jnp.swapaxes(cache_logf, 2, 3), ((0, 0), (0, 0), (0, 8 - N_HEADS), (0, 0)))
    bf_row = jnp.zeros((1, LANES), F32).at[0, FOX_LF_LANE:FOX_LF_LANE + N_HEADS].set(b_f)
    tile4 = lambda g: jnp.tile(g, N_HEADS).reshape(1, 256)
    fixed = lambda b, j, pt: (0, 0)

    def page_spec(r, width):
        return pl.BlockSpec((None, None, width[0], width[1]),
                            lambda b, j, pt, r=r: (l, pt[b, (nsteps - 1 - j) * npp + r], 0, 0))

    in_specs = [
        pl.BlockSpec((8, 1024), lambda b, j, pt: (b, 2)),
        pl.BlockSpec((8, LANES), lambda b, j, pt: (b, 0)),
        pl.BlockSpec((1, LANES), fixed),
        pl.BlockSpec((1, 256), fixed),
        pl.BlockSpec((1, 256), fixed),
        pl.BlockSpec((1, 256), fixed),
        pl.BlockSpec((256, 256), fixed),
        pl.BlockSpec((PAGE_SIZE, PAGE_SIZE), fixed),
    ]
    in_specs += [page_spec(r, (256, PAGE_SIZE)) for r in range(npp)]
    in_specs += [page_spec(r, (256, PAGE_SIZE)) for r in range(npp)]
    in_specs += [page_spec(r, (8, PAGE_SIZE)) for r in range(npp)]
    rows = lambda b, j, pt: (b, 0)
    o, krow, lfrow = pl.pallas_call(
        _fox_dec_kernel,
        grid_spec=pltpu.PrefetchScalarGridSpec(
            num_scalar_prefetch=1,
            grid=(bsz, nsteps),
            in_specs=in_specs,
            out_specs=[pl.BlockSpec((8, 256), rows), pl.BlockSpec((8, 256), rows),
                       pl.BlockSpec((8, LANES), rows)],
            scratch_shapes=[pltpu.VMEM((256, LANES), F32), pltpu.VMEM((8, 1), F32), pltpu.VMEM((8, 1), F32),
                            pltpu.VMEM((256, LANES), F32), pltpu.VMEM((8, 1), F32)],
        ),
        out_shape=[jax.ShapeDtypeStruct((bsz * 8, 256), F32), jax.ShapeDtypeStruct((bsz * 8, 256), F32),
                   jax.ShapeDtypeStruct((bsz * 8, LANES), F32)],
        compiler_params=pltpu.CompilerParams(
            dimension_semantics=("parallel", "arbitrary"), vmem_limit_bytes=VMEM_LIMIT),
        name="fox_sample",
    )(page_table, pmr, psr, bf_row, tile4(q_norm), tile4(k_norm), out_norm.reshape(1, 256),
      jnp.asarray(_BLK256, BF16), jnp.asarray(_UST, BF16), *([ck] * npp), *([cv] * npp), *([clf] * npp))
    first = lambda a: a.reshape(bsz, 8, -1)[:, 0]
    return first(o), first(krow), first(lfrow)[:, FOX_LF_LANE:FOX_LF_LANE + N_HEADS]


CHUNK = 128
ROW0 = 8


def _mm(a, b, hp):
    if hp:
        a_hi, a_lo = _split_bf16(a)
        b_hi, b_lo = _split_bf16(b)
        return _dot3(a_hi, a_lo, b_hi, b_lo)
    return _dot(a.astype(BF16), b.astype(BF16))


def _mm_nt(a, b, hp):
    dn = (((1,), (1,)), ((), ()))
    if hp:
        a_hi, a_lo = _split_bf16(a)
        b_hi, b_lo = _split_bf16(b)
        a3 = jnp.concatenate([a_hi, a_lo, a_hi], axis=-1)
        b3 = jnp.concatenate([b_hi, b_hi, b_lo], axis=-1)
        return lax.dot_general(a3, b3, dn, preferred_element_type=F32)
    return lax.dot_general(a.astype(BF16), b.astype(BF16), dn, preferred_element_type=F32)


def _mm_exact(sel, x):
    hi, mid, lo = _split3_bf16(x)
    return _dot(sel, hi) + _dot(sel, mid) + _dot(sel, lo)


def _mm_exact_r(x, sel):
    hi, mid, lo = _split3_bf16(x)
    return _dot(hi, sel) + _dot(mid, sel) + _dot(lo, sel)


def _softplus(x):
    return jnp.maximum(x, 0.0) + jnp.log1p(jnp.exp(-jnp.abs(x)))


def _silu(x):
    return x * jax.nn.sigmoid(x)


def _stage_rows(buf, blk_ref, lo, hi, prev_ref, rows_in, first):
    @pl.when(first)
    def _():
        buf[ROW0 - 3:ROW0, :] = prev_ref[0]
        if rows_in < CHUNK:
            buf[ROW0 + rows_in:ROW0 + CHUNK, :] = jnp.zeros((CHUNK - rows_in, hi - lo), F32)

    @pl.when(jnp.logical_not(first))
    def _():
        buf[ROW0 - 3:ROW0, :] = buf[ROW0 + CHUNK - 3:ROW0 + CHUNK, :]

    buf[ROW0:ROW0 + rows_in, :] = blk_ref[:, lo:hi]


def _conv4(buf, w_ref):
    acc = w_ref[0:1, :] * buf[ROW0 - 3:ROW0 - 3 + CHUNK, :]
    for j in range(1, CONV_WIDTH):
        acc = acc + w_ref[j:j + 1, :] * buf[ROW0 - 3 + j:ROW0 - 3 + j + CHUNK, :]
    return acc


def _pad_rows(x, rows_in):
    if rows_in == CHUNK:
        return x
    return jnp.concatenate([x, jnp.zeros((CHUNK - rows_in, x.shape[1]), x.dtype)], axis=0)


def _head_expand(first_lane, width):
    e = np.zeros((LANES, N_HEADS * width), np.float32)
    for h in range(N_HEADS):
        e[first_lane + h, h * width:(h + 1) * width] = 1.0
    return jnp.asarray(e, BF16)


_TRI = np.tril(np.ones((CHUNK, CHUNK), np.float32))


SSD_DT_LANE = 12


def _ssd_kernel(pd_ref, ps_ref, cprev_ref, s0_ref, w_ref, cb_ref, dtb_ref, alog_ref, dsk_ref, gn_ref,
                e4_ref, tri_ref, y_ref, sfin_ref, buf, s_scr, *, rows_in, t_valid, hp):
    i = pl.program_id(1)
    first = i == 0

    @pl.when(first)
    def _():
        s_scr[...] = s0_ref[0]

    _stage_rows(buf, pd_ref, 256, 1024, cprev_ref, rows_in, first)
    xbc = _silu(_conv4(buf, w_ref) + cb_ref[...])
    xs = xbc[:, 0:256]
    bm = xbc[:, 256:512]
    cm = xbc[:, 512:768]
    z = _pad_rows(pd_ref[:, 0:256], rows_in)
    pre = _mm_exact_r(_pad_rows(ps_ref[...], rows_in), e4_ref[...])
    dt = _softplus(pre + dtb_ref[...])
    tpos = i * CHUNK + lax.broadcasted_iota(jnp.int32, (CHUNK, 1), 0)
    dt = jnp.where(tpos < t_valid, dt, 0.0)
    la = -jnp.exp(alog_ref[...]) * dt
    b = _mm_exact(tri_ref[...], la)
    row = lax.broadcasted_iota(jnp.int32, (CHUNK, CHUNK), 0)
    col = lax.broadcasted_iota(jnp.int32, (CHUNK, CHUNK), 1)
    causal = row >= col
    lane = lax.broadcasted_iota(jnp.int32, (1, 256), 1)
    s_prev = s_scr[...]
    cb = [_mm_nt(cm[:, g * 128:(g + 1) * 128], bm[:, g * 128:(g + 1) * 128], hp) for g in range(D_GROUPS)]
    y = jnp.zeros((CHUNK, 256), F32)
    s_new = jnp.zeros((D_STATE, 256), F32)
    for h in range(N_HEADS):
        g = h // (N_HEADS // D_GROUPS)
        bh = b[:, h * 128:(h + 1) * 128]
        dth = dt[:, h * 128:(h + 1) * 128]
        hmask = (lane >= h * HEAD_DIM) & (lane < (h + 1) * HEAD_DIM)
        dt2 = jnp.concatenate([dth, dth], axis=-1)
        xdt = jnp.where(hmask, xs * dt2, 0.0)
        rel = jnp.where(causal, jnp.exp(jnp.where(causal, bh - bh.T, 0.0)), 0.0)
        y = y + _mm(cb[g] * rel, xdt, hp)
        eb = jnp.exp(bh)
        y = y + _mm(cm[:, g * 128:(g + 1) * 128] * eb, jnp.where(hmask, s_prev, 0.0), hp)
        b_last = bh[CHUNK - 1:CHUNK, :]
        kdec = bm[:, g * 128:(g + 1) * 128] * jnp.exp(b_last - bh)
        s_new = s_new + _mm(kdec.T, xdt, hp)
        a2 = jnp.exp(jnp.concatenate([b_last, b_last], axis=-1))
        s_new = s_new + jnp.where(hmask, a2 * s_prev, 0.0)
    s_scr[...] = s_new
    sfin_ref[0] = s_new
    y = (y + xs * dsk_ref[...]) * _silu(z)
    outs = []
    for g in range(D_GROUPS):
        yg = y[:, g * 128:(g + 1) * 128]
        outs.append(yg * lax.rsqrt(jnp.mean(yg * yg, axis=-1, keepdims=True) + EPS))
    yn = jnp.concatenate(outs, axis=-1) * gn_ref[...]
    y_ref[...] = yn[0:rows_in, :]


def _ssd(pm, ps, conv_prev, s0, conv_w, conv_b, a_log, dt_bias, d_skip, norm_g, bsz, t_valid, rows_in, hp):
    nt = pm.shape[0] // (bsz * rows_in)
    fixed = lambda b, i: (0, 0)
    rows = lambda b, i: (b * nt + i, 0)
    rep128 = lambda v: jnp.repeat(v, 128).reshape(1, 512)
    s0l = jnp.transpose(s0, (0, 2, 1, 3)).reshape(bsz, D_STATE, 256)
    y, sfin = pl.pallas_call(
        functools.partial(_ssd_kernel, rows_in=rows_in, t_valid=t_valid, hp=hp),
        grid=(bsz, nt),
        in_specs=[
            pl.BlockSpec((rows_in, 1024), lambda b, i: (b * nt + i, 3)),
            pl.BlockSpec((rows_in, LANES), rows),
            pl.BlockSpec((1, 3, D_CONV_CH), lambda b, i: (b, 0, 0)),
            pl.BlockSpec((1, D_STATE, 256), lambda b, i: (b, 0, 0)),
            pl.BlockSpec((CONV_WIDTH, D_CONV_CH), fixed),
            pl.BlockSpec((1, D_CONV_CH), fixed),
            pl.BlockSpec((1, 512), fixed),
            pl.BlockSpec((1, 512), fixed),
            pl.BlockSpec((1, 256), fixed),
            pl.BlockSpec((1, 256), fixed),
            pl.BlockSpec((LANES, 512), fixed),
            pl.BlockSpec((CHUNK, CHUNK), fixed),
        ],
        out_specs=[
            pl.BlockSpec((rows_in, 256), rows),
            pl.BlockSpec((1, D_STATE, 256), lambda b, i: (b, 0, 0)),
        ],
        out_shape=[
            jax.ShapeDtypeStruct((pm.shape[0], 256), F32),
            jax.ShapeDtypeStruct((bsz, D_STATE, 256), F32),
        ],
        scratch_shapes=[pltpu.VMEM((ROW0 + CHUNK, D_CONV_CH), F32), pltpu.VMEM((D_STATE, 256), F32)],
        compiler_params=pltpu.CompilerParams(
            dimension_semantics=("parallel", "arbitrary"), vmem_limit_bytes=VMEM_LIMIT),
        name="ssd",
    )(pm, ps, conv_prev, s0l, conv_w, conv_b.reshape(1, -1), rep128(dt_bias), rep128(a_log),
      jnp.repeat(d_skip, HEAD_DIM).reshape(1, 256), norm_g.reshape(1, 256),
      _head_expand(SSD_DT_LANE, 128), jnp.asarray(_TRI, BF16))
    return y, jnp.transpose(sfin.reshape(bsz, D_STATE, N_HEADS, HEAD_DIM), (0, 2, 1, 3))


SUB = CHUNK_A
_SUB_ID = np.arange(CHUNK) // SUB
_SAME_SUB = (_SUB_ID[:, None] == _SUB_ID[None, :]).astype(np.float32)
_BLK256 = (np.arange(256)[:, None] // HEAD_DIM == np.arange(256)[None, :] // HEAD_DIM).astype(np.float32)


def _mm_tn(a, b, hp):
    dn = (((0,), (0,)), ((), ()))
    if hp:
        a_hi, a_lo = _split_bf16(a)
        b_hi, b_lo = _split_bf16(b)
        a3 = jnp.concatenate([a_hi, a_lo, a_hi], axis=0)
        b3 = jnp.concatenate([b_hi, b_hi, b_lo], axis=0)
        return lax.dot_general(a3, b3, dn, preferred_element_type=F32)
    return lax.dot_general(a.astype(BF16), b.astype(BF16), dn, preferred_element_type=F32)


def _group_sum(x, ones_blk, hp):
    if hp:
        return _mm_exact_r(x, ones_blk)
    return _dot(x.astype(BF16), ones_blk)


def _hgrn_kernel(pa_ref, s0_ref, c1_ref, c2_ref, oml_ref, gn_ref, blk_ref, t16_ref, l16_ref,
                 o_ref, sfin_ref, kbuf, vbuf, lbuf, st_scr, *, rows_in, t_valid, hp):
    i = pl.program_id(1)

    @pl.when(i == 0)
    def _():
        st_scr[...] = s0_ref[0]
        zeros = jnp.zeros((SUB, 256), F32)
        kbuf[0:SUB, :] = zeros
        vbuf[0:SUB, :] = zeros
        lbuf[0:SUB, :] = zeros

    q = _pad_rows(pa_ref[:, 0:256], rows_in) * (HEAD_DIM ** -0.5)
    zf = _pad_rows(pa_ref[:, 256:512], rows_in)
    v = _pad_rows(pa_ref[:, 512:768], rows_in)
    g = _pad_rows(pa_ref[:, 768:1024], rows_in)
    la = c1_ref[...]
    lb = c2_ref[...] + _log_sigmoid(zf)
    lf = jnp.maximum(la, lb) + jnp.log1p(jnp.exp(-jnp.abs(la - lb)))
    k = oml_ref[...] * jax.nn.sigmoid(-zf)
    tpos = i * CHUNK + lax.broadcasted_iota(jnp.int32, (CHUNK, 1), 0)
    valid = tpos < t_valid
    lf = jnp.where(valid, lf, 0.0)
    k = jnp.where(valid, k, 0.0)
    v = jnp.where(valid, v, 0.0)
    kbuf[SUB:SUB + CHUNK, :] = k
    vbuf[SUB:SUB + CHUNK, :] = v
    lbuf[SUB:SUB + CHUNK, :] = lf

    blk = blk_ref[...]
    sub = lax.broadcasted_iota(jnp.int32, (CHUNK, 1), 0) % SUB
    o = _group_sum(q * k, blk, hp) * v
    bd = jnp.zeros((CHUNK, 256), F32)
    for d in range(1, min(SUB, t_valid)):
        bd = bd + lbuf[SUB - d + 1:SUB - d + 1 + CHUNK, :]
        m = sub >= d
        ks = kbuf[SUB - d:SUB - d + CHUNK, :]
        tmp = jnp.where(m, q * ks * jnp.exp(jnp.where(m, bd, 0.0)), 0.0)
        o = o + _group_sum(tmp, blk, hp) * vbuf[SUB - d:SUB - d + CHUNK, :]

    b = _mm_exact(t16_ref[...], lf)
    bl = _mm_exact(l16_ref[...], lf)
    qe = q * jnp.exp(b)
    kd = k * jnp.exp(bl - b)
    st = st_scr[...]
    bdmask = blk > 0
    o_rows = []
    for n in range(CHUNK // SUB):
        r = slice(n * SUB, (n + 1) * SUB)
        o_rows.append(_mm_nt(qe[r], st, hp))
        ds = _mm_tn(v[r], kd[r], hp)
        st = st * jnp.exp(bl[n * SUB:n * SUB + 1, :]) + jnp.where(bdmask, ds, 0.0)
    st_scr[...] = st
    sfin_ref[0] = st
    o = o + jnp.concatenate(o_rows, axis=0)
    o = o * lax.rsqrt(_group_mean(o * o, blk, HEAD_DIM) + EPS) * gn_ref[...] * _silu(g)
    o_ref[...] = o[0:rows_in, :]


def _hgrn(pm, lb, norm_g, s0, bsz, t_valid, rows_in, hp):
    nt = pm.shape[0] // (bsz * rows_in)
    fixed = lambda b, i: (0, 0)
    rows = lambda b, i: (b * nt + i, 0)
    lb = jnp.clip(lb, 0.0, LB_CEIL)
    c1 = jnp.log(jnp.maximum(lb, LB_FLOOR)).reshape(1, 256)
    c2 = jnp.log1p(-lb).reshape(1, 256)
    oml = (1.0 - lb).reshape(1, 256)
    eye = jnp.eye(N_HEADS, dtype=bool)[None, :, None, :, None]
    st0 = jnp.where(eye, jnp.swapaxes(s0, 2, 3)[:, :, :, None, :], 0.0).reshape(bsz, 256, 256)
    o, sfin = pl.pallas_call(
        functools.partial(_hgrn_kernel, rows_in=rows_in, t_valid=t_valid, hp=hp),
        grid=(bsz, nt),
        in_specs=[
            pl.BlockSpec((rows_in, 1024), lambda b, i: (b * nt + i, 0)),
            pl.BlockSpec((1, 256, 256), lambda b, i: (b, 0, 0)),
            pl.BlockSpec((1, 256), fixed),
            pl.BlockSpec((1, 256), fixed),
            pl.BlockSpec((1, 256), fixed),
            pl.BlockSpec((1, 256), fixed),
            pl.BlockSpec((256, 256), fixed),
            pl.BlockSpec((CHUNK, CHUNK), fixed),
            pl.BlockSpec((CHUNK, CHUNK), fixed),
        ],
        out_specs=[
            pl.BlockSpec((rows_in, 256), rows),
            pl.BlockSpec((1, 256, 256), lambda b, i: (b, 0, 0)),
        ],
        out_shape=[
            jax.ShapeDtypeStruct((pm.shape[0], 256), F32),
            jax.ShapeDtypeStruct((bsz, 256, 256), F32),
        ],
        scratch_shapes=[pltpu.VMEM((SUB + CHUNK, 256), F32)] * 3 + [pltpu.VMEM((256, 256), F32)],
        compiler_params=pltpu.CompilerParams(
            dimension_semantics=("parallel", "arbitrary"), vmem_limit_bytes=VMEM_LIMIT),
        name="hgrn",
    )(pm, st0, c1, c2, oml, norm_g.reshape(1, 256), jnp.asarray(_BLK256, BF16),
      jnp.asarray(_TRI * _SAME_SUB, BF16), jnp.asarray(_SAME_SUB, BF16))
    sf = sfin.reshape(bsz, N_HEADS, HEAD_DIM, N_HEADS, HEAD_DIM)
    sf = jnp.stack([sf[:, h, :, h, :] for h in range(N_HEADS)], axis=1)
    return o, jnp.swapaxes(sf, 2, 3)


GDN_BETA_LANE = 0
GDN_DT_LANE = 4


def _block_diag_state(s0, bsz):
    eye = jnp.eye(N_HEADS, dtype=bool)[None, :, None, :, None]
    return jnp.where(eye, jnp.swapaxes(s0, 2, 3)[:, :, :, None, :], 0.0).reshape(bsz, 256, 256)


def _unblock_diag_state(st, bsz):
    sf = st.reshape(bsz, N_HEADS, HEAD_DIM, N_HEADS, HEAD_DIM)
    sf = jnp.stack([sf[:, h, :, h, :] for h in range(N_HEADS)], axis=1)
    return jnp.swapaxes(sf, 2, 3)


def _gdn_kernel(pb_ref, ps_ref, cprev_ref, s0_ref, w_ref, dtb_ref, alog_ref, dtb64_ref, alog64_ref, gn_ref,
                blk_ref, tri_ref, eb128_ref, ed128_ref, eb64_ref, ed64_ref,
                o_ref, sfin_ref, buf, st_scr, *, rows_in, t_valid, hp):
    i = pl.program_id(1)
    first = i == 0

    @pl.when(first)
    def _():
        st_scr[...] = s0_ref[0]

    _stage_rows(buf, pb_ref, 0, 768, cprev_ref, rows_in, first)
    qkv = _silu(_conv4(buf, w_ref))
    blk = blk_ref[...]
    q = qkv[:, 0:256]
    k = qkv[:, 256:512]
    v = qkv[:, 512:768]
    q = q * lax.rsqrt(_group_mean(q * q, blk, 1) + EPS) * (HEAD_DIM ** -0.5)
    k = k * lax.rsqrt(_group_mean(k * k, blk, 1) + EPS)
    gate = _pad_rows(pb_ref[:, 768:1024], rows_in)
    ps = _pad_rows(ps_ref[...], rows_in)
    tpos = i * CHUNK + lax.broadcasted_iota(jnp.int32, (CHUNK, 1), 0)
    valid = tpos < t_valid
    tri = tri_ref[...]
    beta128 = jnp.where(valid, jax.nn.sigmoid(_mm_exact_r(ps, eb128_ref[...])), 0.0)
    la128 = jnp.where(valid, -jnp.exp(alog_ref[...]) * _softplus(_mm_exact_r(ps, ed128_ref[...]) + dtb_ref[...]), 0.0)
    b128 = _mm_exact(tri, la128)
    beta64 = jnp.where(valid, jax.nn.sigmoid(_mm_exact_r(ps, eb64_ref[...])), 0.0)
    la64 = jnp.where(valid, -jnp.exp(alog64_ref[...]) * _softplus(_mm_exact_r(ps, ed64_ref[...]) + dtb64_ref[...]), 0.0)
    b64 = _mm_exact(tri, la64)
    eb64 = jnp.exp(b64)
    b_last64 = b64[CHUNK - 1:CHUNK, :]
    kb = k * beta64
    rv = v * beta64
    rk = kb * eb64
    q_dec = q * eb64
    k_dec = k * jnp.exp(b_last64 - b64)

    row = lax.broadcasted_iota(jnp.int32, (CHUNK, CHUNK), 0)
    col = lax.broadcasted_iota(jnp.int32, (CHUNK, CHUNK), 1)
    causal = row >= col
    strict = row > col
    lane = lax.broadcasted_iota(jnp.int32, (1, 256), 1)
    a_mats, rhs, atts = [], [], []
    for h in range(N_HEADS):
        hs = slice(h * HEAD_DIM, (h + 1) * HEAD_DIM)
        hmask = (lane >= h * HEAD_DIM) & (lane < (h + 1) * HEAD_DIM)
        bh = b128[:, h * 128:(h + 1) * 128]
        decay = jnp.where(causal, jnp.exp(jnp.where(causal, bh - bh.T, 0.0)), 0.0)
        k_h = jnp.where(hmask, k, 0.0)
        a_mats.append(-jnp.where(strict, _mm_nt(jnp.where(hmask, kb, 0.0), k_h, True) * decay, 0.0))
        atts.append(_mm_nt(jnp.where(hmask, q, 0.0), k_h, hp) * decay)
        rhs.append(jnp.concatenate([rv[:, hs], rk[:, hs]], axis=-1))
    zero = jnp.zeros((CHUNK, CHUNK), F32)
    xs = []
    for pair in range(N_HEADS // 2):
        h0, h1 = 2 * pair, 2 * pair + 1
        a = jnp.concatenate([jnp.concatenate([a_mats[h0], zero], axis=-1),
                             jnp.concatenate([zero, a_mats[h1]], axis=-1)], axis=0)
        x = jnp.concatenate([rhs[h0], rhs[h1]], axis=0)
        if t_valid > 1:
            x = x + _mm(a, x, True)
            for _ in range(6):
                a = _mm(a, a, True)
                x = x + _mm(a, x, True)
        xs += [x[:CHUNK], x[CHUNK:]]
    u_all = jnp.concatenate([x[:, :HEAD_DIM] for x in xs], axis=-1)
    w_all = jnp.concatenate([x[:, HEAD_DIM:] for x in xs], axis=-1)
    st = st_scr[...]
    v_new = u_all - _mm_nt(w_all, st, hp)
    o = _mm_nt(q_dec, st, hp)
    for h in range(N_HEADS):
        hmask = (lane >= h * HEAD_DIM) & (lane < (h + 1) * HEAD_DIM)
        o = o + _mm(atts[h], jnp.where(hmask, v_new, 0.0), hp)
    st = st * jnp.exp(b_last64) + jnp.where(blk > 0, _mm_tn(v_new, k_dec, hp), 0.0)
    st_scr[...] = st
    sfin_ref[0] = st
    o = o * lax.rsqrt(_group_mean(o * o, blk, HEAD_DIM) + EPS) * gn_ref[...] * _silu(gate)
    o_ref[...] = o[0:rows_in, :]


def _gdn(pm, ps, conv_prev, s0, conv_w, a_log, dt_bias, norm_g, bsz, t_valid, rows_in, hp):
    nt = pm.shape[0] // (bsz * rows_in)
    fixed = lambda b, i: (0, 0)
    rows = lambda b, i: (b * nt + i, 0)
    rep = lambda v, w: jnp.repeat(v, w).reshape(1, N_HEADS * w)
    o, sfin = pl.pallas_call(
        functools.partial(_gdn_kernel, rows_in=rows_in, t_valid=t_valid, hp=hp),
        grid=(bsz, nt),
        in_specs=[
            pl.BlockSpec((rows_in, 1024), lambda b, i: (b * nt + i, 1)),
            pl.BlockSpec((rows_in, LANES), rows),
            pl.BlockSpec((1, 3, B_CONV_CH), lambda b, i: (b, 0, 0)),
            pl.BlockSpec((1, 256, 256), lambda b, i: (b, 0, 0)),
            pl.BlockSpec((CONV_WIDTH, B_CONV_CH), fixed),
            pl.BlockSpec((1, 512), fixed),
            pl.BlockSpec((1, 512), fixed),
            pl.BlockSpec((1, 256), fixed),
            pl.BlockSpec((1, 256), fixed),
            pl.BlockSpec((1, 256), fixed),
            pl.BlockSpec((256, 256), fixed),
            pl.BlockSpec((CHUNK, CHUNK), fixed),
            pl.BlockSpec((LANES, 512), fixed),
            pl.BlockSpec((LANES, 512), fixed),
            pl.BlockSpec((LANES, 256), fixed),
            pl.BlockSpec((LANES, 256), fixed),
        ],
        out_specs=[
            pl.BlockSpec((rows_in, 256), rows),
            pl.BlockSpec((1, 256, 256), lambda b, i: (b, 0, 0)),
        ],
        out_shape=[
            jax.ShapeDtypeStruct((pm.shape[0], 256), F32),
            jax.ShapeDtypeStruct((bsz, 256, 256), F32),
        ],
        scratch_shapes=[pltpu.VMEM((ROW0 + CHUNK, B_CONV_CH), F32), pltpu.VMEM((256, 256), F32)],
        compiler_params=pltpu.CompilerParams(
            dimension_semantics=("parallel", "arbitrary"), vmem_limit_bytes=VMEM_LIMIT),
        name="gdn",
    )(pm, ps, conv_prev, _block_diag_state(s0, bsz), conv_w, rep(dt_bias, 128), rep(a_log, 128),
      rep(dt_bias, 64), rep(a_log, 64), norm_g.reshape(1, 256), jnp.asarray(_BLK256, BF16),
      jnp.asarray(_TRI, BF16), _head_expand(GDN_BETA_LANE, 128), _head_expand(GDN_DT_LANE, 128),
      _head_expand(GDN_BETA_LANE, 64), _head_expand(GDN_DT_LANE, 64))
    return o, _unblock_diag_state(sfin, bsz)


def _prep_weights(prm):
    w_in = prm['w_in']
    wm = w_in[:, :, _MAIN_COLS]
    ws = jnp.pad(w_in[:, :, _SMALL_COLS], ((0, 0), (0, 0), (0, LANES - len(_SMALL_COLS))))
    wr = jnp.concatenate([prm['moe_w_expert'], prm['moe_w_group']], axis=-1)
    n_r = N_EXPERTS + N_EXPERT_GROUPS
    wr = jnp.pad(wr, ((0, 0), (0, 0), (0, LANES - n_r)))
    both = lambda pair: [(pair[0][:D_MODEL], pair[1][:D_MODEL]), (pair[0][D_MODEL:], pair[1][D_MODEL:])]
    ws_split = both(_split_weight(ws.reshape(DEPTH * D_MODEL, LANES)))
    wr_split = both(_split_weight(wr.reshape(DEPTH * D_MODEL, LANES)))
    br = jnp.pad(jnp.concatenate([prm['moe_b_expert'], prm['moe_b_group']], axis=-1),
                 ((0, 0), (0, LANES - n_r)))[:, None, :]
    bf = lambda a: a.astype(BF16)
    prompt, sample = [], []
    for l in range(DEPTH):
        common = dict(ws=ws_split[l], wr=wr_split[l], br=br[l])
        if l == 0:
            wm_p, wo_p = tuple(_split_weight(wm[l])), tuple(_split_weight(prm['w_out'][l]))
        else:
            wm_p, wo_p = (bf(wm[l]),), (bf(prm['w_out'][l]),)
        prompt.append(dict(common, wm=wm_p, wo=wo_p, w1=bf(prm['moe_w1'][l]), w3=bf(prm['moe_w3'][l]),
                           w2=bf(prm['moe_w2'][l]), wg=bf(prm['ple_w_gate'][l]), wp=bf(prm['ple_w_proj'][l])))
        sample.append(dict(common, wm=(wm[l],), wo=(prm['w_out'][l],), w1=prm['moe_w1'][l], w3=prm['moe_w3'][l],
                           w2=prm['moe_w2'][l], wg=prm['ple_w_gate'][l], wp=prm['ple_w_proj'][l]))
    return prompt, sample


def _trunk(x, p, init_state, fox_cache, lb_all, prm, wts, tm, hp_layers):
    s_hgrn0, s_gdn0, c_gdn0, s_ssd0, c_ssd0 = init_state
    bsz, t, _ = x.shape
    n = bsz * t
    h = x.reshape(n, D_MODEL)
    outs = [[] for _ in range(8)]
    row = lambda a: a.reshape(1, -1)
    for l in range(DEPTH):
        hp_mix = hp_layers[l]
        w = wts[l]
        pm, ps = _inproj(h, row(prm['g_mix'][l]), w['wm'], w['ws'][0], w['ws'][1], tm)
        if fox_cache is None:
            o_c, k_c, lf_c = _fox_prompt(pm, ps, prm['fox_b_f'][l], prm['fox_q_norm'][l], prm['fox_k_norm'][l],
                                         prm['fox_out_norm'][l], bsz, t, hp_mix)
            o_c = o_c.reshape(bsz, t, GROUP_WIDTH)
            k_c = k_c.reshape(bsz, t, N_HEADS, HEAD_DIM)
            lf_c = lf_c.reshape(bsz, t, N_HEADS)
            v_c = pm[:, 2560:2816].reshape(bsz, t, N_HEADS, HEAD_DIM)
        rows_in = CHUNK if t % CHUNK == 0 else 8
        if rows_in == CHUNK:
            pmr, psr = pm, ps
        else:
            padr = lambda a: jnp.pad(a.reshape(bsz, t, -1), ((0, 0), (0, rows_in - t), (0, 0))).reshape(
                bsz * rows_in, -1)
            pmr, psr = padr(pm), padr(ps)
        unpad = lambda a: a.reshape(bsz, -1, GROUP_WIDTH)[:, :t]
        o_a, s_a = _hgrn(pmr, lb_all[l], prm['hgrn_norm'][l], s_hgrn0[l], bsz, t, rows_in, hp_mix)
        o_d, s_d = _ssd(pmr, psr, c_ssd0[l], s_ssd0[l], prm['ssd_conv_w'][l], prm['ssd_conv_b'][l],
                        prm['ssd_a_log'][l], prm['ssd_dt_bias'][l], prm['ssd_d'][l], prm['ssd_norm'][l],
                        bsz, t, rows_in, hp_mix)
        o_b, s_b = _gdn(pmr, psr, c_gdn0[l], s_gdn0[l], prm['gdn_conv_w'][l], prm['gdn_a_log'][l],
                        prm['gdn_dt_bias'][l], prm['gdn_norm'][l], bsz, t, rows_in, hp_mix)
        o_a, o_b, o_d = unpad(o_a), unpad(o_b), unpad(o_d)
        pm = pm.reshape(bsz, t, N_MAIN)
        ps = ps.reshape(bsz, t, LANES)
        c_d = jnp.concatenate([c_ssd0[l], pm[:, max(0, t - 3):, 3328:4096]], axis=1)[:, -(CONV_WIDTH - 1):]
        c_b = jnp.concatenate([c_gdn0[l], pm[:, max(0, t - 3):, 1024:1792]], axis=1)[:, -(CONV_WIDTH - 1):]
        if fox_cache is not None:
            cache_k, cache_v, cache_logf, page_table = fox_cache
            o_c, k_c, lf_c = _fox_sample(pmr, psr, cache_k, cache_v, cache_logf, page_table, l,
                                         prm['fox_b_f'][l], prm['fox_q_norm'][l], prm['fox_k_norm'][l],
                                         prm['fox_out_norm'][l])
            o_c = o_c.reshape(bsz, t, GROUP_WIDTH)
            k_c = k_c.reshape(bsz, t, N_HEADS, HEAD_DIM)
            lf_c = lf_c.reshape(bsz, t, N_HEADS)
            v_c = pm[..., 2560:2816].reshape(bsz, t, N_HEADS, HEAD_DIM)
        mix = jnp.concatenate([o_a, o_b, o_c, o_d], axis=-1).reshape(n, D_MODEL)
        h2, u2, cw = _outproj(h, mix, w['wo'], row(prm['g_ffn'][l]), w['wr'][0], w['wr'][1], w['br'], tm)
        y = _moe(u2, cw, w['w1'], w['w3'], w['w2'], tm)
        h = _ple(h2, y, p[l].reshape(n, PLE_DIM), row(prm['g_ple'][l]), w['wg'], w['wp'],
                 row(prm['g_final']), tm, final=(l == DEPTH - 1))
        for acc, val in zip(outs, (k_c, v_c, lf_c, s_a, s_b, c_b, s_d, c_d)):
            acc.append(val)
    return (h.reshape(bsz, t, D_MODEL),) + tuple(jnp.stack(acc) for acc in outs)


def _hgrn_lower_bounds(lb_param):
    sm = jax.nn.softmax(lb_param, axis=0)
    return jnp.concatenate([jnp.zeros_like(sm[:1]), jnp.cumsum(sm[1:], axis=0)], axis=0)


def kernel(x_prompt, x_sample, cache_fox_k, cache_fox_v, cache_fox_logf, state_hgrn, state_gdn,
           state_gdn_conv, state_ssd, state_ssd_conv, page_table, p_prompt, p_sample, w_in, w_out,
           g_mix, g_ffn, g_ple, g_final, hgrn_lb, hgrn_norm, gdn_conv_w, gdn_a_log, gdn_dt_bias,
           gdn_norm, fox_b_f, fox_q_norm, fox_k_norm, fox_out_norm, ssd_conv_w, ssd_conv_b, ssd_a_log,
           ssd_dt_bias, ssd_d, ssd_norm, moe_w_group, moe_b_group, moe_w_expert, moe_b_expert, moe_w1,
           moe_w3, moe_w2, ple_w_gate, ple_w_proj):
    prm = dict(w_in=w_in, w_out=w_out, g_mix=g_mix, g_ffn=g_ffn, g_ple=g_ple, g_final=g_final,
               hgrn_norm=hgrn_norm, gdn_conv_w=gdn_conv_w, gdn_a_log=gdn_a_log, gdn_dt_bias=gdn_dt_bias,
               gdn_norm=gdn_norm, fox_b_f=fox_b_f, fox_q_norm=fox_q_norm, fox_k_norm=fox_k_norm,
               fox_out_norm=fox_out_norm, ssd_conv_w=ssd_conv_w, ssd_conv_b=ssd_conv_b, ssd_a_log=ssd_a_log,
               ssd_dt_bias=ssd_dt_bias, ssd_d=ssd_d, ssd_norm=ssd_norm, moe_w_group=moe_w_group,
               moe_b_group=moe_b_group, moe_w_expert=moe_w_expert, moe_b_expert=moe_b_expert,
               moe_w1=moe_w1, moe_w3=moe_w3, moe_w2=moe_w2, ple_w_gate=ple_w_gate, ple_w_proj=ple_w_proj)
    wts_prompt, wts_sample = _prep_weights(prm)
    lb_all = _hgrn_lower_bounds(hgrn_lb)
    bp = x_prompt.shape[0]
    zero_state = (jnp.zeros((DEPTH, bp, N_HEADS, HEAD_DIM, HEAD_DIM), F32),
                  jnp.zeros((DEPTH, bp, N_HEADS, HEAD_DIM, HEAD_DIM), F32),
                  jnp.zeros((DEPTH, bp, CONV_WIDTH - 1, B_CONV_CH), F32),
                  jnp.zeros((DEPTH, bp, N_HEADS, D_STATE, HEAD_DIM), F32),
                  jnp.zeros((DEPTH, bp, CONV_WIDTH - 1, D_CONV_CH), F32))
    pr = _trunk(x_prompt, p_prompt, zero_state, None, lb_all, prm, wts_prompt, tm=512, hp_layers=(True, False))
    sm = _trunk(x_sample, p_sample, (state_hgrn, state_gdn, state_gdn_conv, state_ssd, state_ssd_conv),
                (cache_fox_k, cache_fox_v, cache_fox_logf, page_table), lb_all, prm, wts_sample, tm=32,
                hp_layers=(True, True))
    return (pr[0], sm[0]) + tuple(pr[1:]) + tuple(sm[1:])
```

```python
import functools
import math

import jax
import jax.numpy as jnp
import numpy as np
from jax import lax
from jax.experimental import pallas as pl
from jax.experimental.pallas import tpu as pltpu

F32 = jnp.float32
BF16 = jnp.bfloat16

D_MODEL = 1024
DEPTH = 2
PAGE_SIZE = 128
EPS = 1e-6
NEG_BIG = -1e30
LB_FLOOR = 1e-30
LB_CEIL = 1.0 - 1e-6
PLE_DIM = 256
GROUP_WIDTH = 256
HEAD_DIM = 64
N_HEADS = 4
D_GROUPS = 2
D_STATE = 128
CONV_WIDTH = 4
B_CONV_CH = 3 * GROUP_WIDTH
D_CONV_CH = GROUP_WIDTH + 2 * D_GROUPS * D_STATE
SIZE_A = 4 * GROUP_WIDTH
SIZE_B = 4 * GROUP_WIDTH + 2 * N_HEADS
SIZE_C = 4 * GROUP_WIDTH + N_HEADS
SIZE_D = GROUP_WIDTH + D_CONV_CH + N_HEADS
OFF_B = SIZE_A
OFF_C = OFF_B + SIZE_B
OFF_D = OFF_C + SIZE_C
N_IN = OFF_D + SIZE_D
CHUNK_A = 16
CHUNK_B = 64
CHUNK_D = 128
Q_BLOCK = 128
N_EXPERT_GROUPS = 4
EXPERTS_PER_GROUP = 4
N_EXPERTS = 16
D_EXPERT = 512

LANES = 128
N_MAIN = 4096
VMEM_LIMIT = 48 * 1024 * 1024

_MAIN_COLS = np.concatenate([
    np.arange(0, SIZE_A),
    np.arange(OFF_B, OFF_B + 4 * GROUP_WIDTH),
    np.arange(OFF_C, OFF_C + 4 * GROUP_WIDTH),
    np.arange(OFF_D, OFF_D + 4 * GROUP_WIDTH),
])
_SMALL_COLS = np.concatenate([
    np.arange(OFF_B + 4 * GROUP_WIDTH, OFF_B + SIZE_B),
    np.arange(OFF_C + 4 * GROUP_WIDTH, OFF_C + SIZE_C),
    np.arange(OFF_D + 4 * GROUP_WIDTH, OFF_D + SIZE_D),
])


def _split_bf16(x):
    hi = x.astype(BF16)
    lo = (x - hi.astype(F32)).astype(BF16)
    return hi, lo


def _dot(a, b):
    return jnp.dot(a, b, preferred_element_type=F32)


def _dot3(a_hi, a_lo, b_hi, b_lo):
    return _dot(a_hi, b_hi) + _dot(a_lo, b_hi) + _dot(a_hi, b_lo)


def _rms(x, g):
    return x * lax.rsqrt(jnp.mean(x * x, axis=-1, keepdims=True) + EPS) * g


def _wmode(w):
    if len(w) == 2:
        return 'x3'
    return 'f32' if w[0].dtype == F32 else 'x1'


def _wcount(mode):
    return 2 if mode == 'x3' else 1


def _wload(refs, mode, idx=None):
    get = (lambda r: r[...]) if idx is None else (lambda r: r[idx])
    if mode == 'x1':
        return get(refs[0]), None
    if mode == 'x3':
        return get(refs[0]), get(refs[1])
    return _split_bf16(get(refs[0]))


def _wdot(a, refs, mode, idx=None):
    w_hi, w_lo = _wload(refs, mode, idx)
    if w_lo is None:
        return _dot(a.astype(BF16), w_hi)
    a_hi, a_lo = _split_bf16(a)
    return _dot3(a_hi, a_lo, w_hi, w_lo)


def _split_kernel(w_ref, hi_ref, lo_ref):
    hi, lo = _split_bf16(w_ref[...])
    hi_ref[...] = hi
    lo_ref[...] = lo


def _split_weight(w):
    rows, cols = w.shape
    spec = pl.BlockSpec((256, cols), lambda i: (i, 0))
    return pl.pallas_call(
        _split_kernel,
        grid=(rows // 256,),
        in_specs=[spec],
        out_specs=[spec, spec],
        out_shape=[jax.ShapeDtypeStruct((rows, cols), BF16)] * 2,
        compiler_params=pltpu.CompilerParams(dimension_semantics=("parallel",)),
        name="split_weight",
    )(w)


def _inproj_kernel(x_ref, g_ref, *refs, mode):
    nw = _wcount(mode)
    wm_refs = refs[:nw]
    wsh_ref, wsl_ref, om_ref, os_ref, uh_ref, ul_ref = refs[nw:]

    @pl.when(pl.program_id(1) == 0)
    def _():
        u = _rms(x_ref[...], g_ref[...])
        uh, ul = _split_bf16(u)
        uh_ref[...] = uh
        ul_ref[...] = ul
        os_ref[...] = _dot3(uh, ul, wsh_ref[...], wsl_ref[...])

    w_hi, w_lo = _wload(wm_refs, mode)
    if w_lo is None:
        om_ref[...] = _dot(uh_ref[...], w_hi)
    else:
        om_ref[...] = _dot3(uh_ref[...], ul_ref[...], w_hi, w_lo)


def _inproj(x, g, wm, wsh, wsl, tm):
    n = x.shape[0]
    tn = 1024
    return pl.pallas_call(
        functools.partial(_inproj_kernel, mode=_wmode(wm)),
        grid=(n // tm, N_MAIN // tn),
        in_specs=[
            pl.BlockSpec((tm, D_MODEL), lambda i, j: (i, 0)),
            pl.BlockSpec((1, D_MODEL), lambda i, j: (0, 0)),
        ] + [pl.BlockSpec((D_MODEL, tn), lambda i, j: (0, j))] * len(wm) + [
            pl.BlockSpec((D_MODEL, LANES), lambda i, j: (0, 0)),
            pl.BlockSpec((D_MODEL, LANES), lambda i, j: (0, 0)),
        ],
        out_specs=[
            pl.BlockSpec((tm, tn), lambda i, j: (i, j)),
            pl.BlockSpec((tm, LANES), lambda i, j: (i, 0)),
        ],
        out_shape=[
            jax.ShapeDtypeStruct((n, N_MAIN), F32),
            jax.ShapeDtypeStruct((n, LANES), F32),
        ],
        scratch_shapes=[pltpu.VMEM((tm, D_MODEL), BF16), pltpu.VMEM((tm, D_MODEL), BF16)],
        compiler_params=pltpu.CompilerParams(
            dimension_semantics=("parallel", "arbitrary"), vmem_limit_bytes=VMEM_LIMIT),
        name="inproj",
    )(x, g, *wm, wsh, wsl)


def _route(logits):
    lane = lax.broadcasted_iota(jnp.int32, logits.shape, 1)
    gmask = (lane >= N_EXPERTS) & (lane < N_EXPERTS + N_EXPERT_GROUPS)
    gl = jnp.where(gmask, logits, -jnp.inf)
    gmax = jnp.max(gl, axis=-1, keepdims=True)
    gidx = jnp.min(jnp.where(gl == gmax, lane, 4 * LANES), axis=-1, keepdims=True) - N_EXPERTS
    gw = 1.0 / jnp.sum(jnp.where(gmask, jnp.exp(gl - gmax), 0.0), axis=-1, keepdims=True)
    lo = gidx * EXPERTS_PER_GROUP
    emask = (lane >= lo) & (lane < lo + EXPERTS_PER_GROUP)
    el = jnp.where(emask, logits, -jnp.inf)
    m1 = jnp.max(el, axis=-1, keepdims=True)
    i1 = jnp.min(jnp.where(el == m1, lane, 4 * LANES), axis=-1, keepdims=True)
    el2 = jnp.where(lane == i1, -jnp.inf, el)
    m2 = jnp.max(el2, axis=-1, keepdims=True)
    i2 = jnp.min(jnp.where(el2 == m2, lane, 4 * LANES), axis=-1, keepdims=True)
    e2 = jnp.exp(m2 - m1)
    den = 1.0 + e2
    g1 = gw / den
    g2 = gw * e2 / den
    return jnp.where(lane == i1, g1, jnp.where(lane == i2, g2, 0.0))


def _outproj_kernel(h_ref, mix_ref, *refs, mode):
    nw = _wcount(mode)
    g_ref, wrh_ref, wrl_ref, br_ref, h2_ref, u2_ref, cw_ref = refs[nw:]
    h2 = h_ref[...] + _wdot(mix_ref[...], refs[:nw], mode)
    h2_ref[...] = h2
    u = _rms(h2, g_ref[...])
    u2_ref[...] = u
    uh, ul = _split_bf16(u)
    logits = _dot3(uh, ul, wrh_ref[...], wrl_ref[...]) + br_ref[...]
    cw_ref[...] = _route(logits)


def _outproj(h, mix, wo, g, wrh, wrl, br, tm):
    n = h.shape[0]
    row = lambda i: (i, 0)
    fixed = lambda i: (0, 0)
    return pl.pallas_call(
        functools.partial(_outproj_kernel, mode=_wmode(wo)),
        grid=(n // tm,),
        in_specs=[
            pl.BlockSpec((tm, D_MODEL), row),
            pl.BlockSpec((tm, D_MODEL), row),
        ] + [pl.BlockSpec((D_MODEL, D_MODEL), fixed)] * len(wo) + [
            pl.BlockSpec((1, D_MODEL), fixed),
            pl.BlockSpec((D_MODEL, LANES), fixed),
            pl.BlockSpec((D_MODEL, LANES), fixed),
            pl.BlockSpec((1, LANES), fixed),
        ],
        out_specs=[
            pl.BlockSpec((tm, D_MODEL), row),
            pl.BlockSpec((tm, D_MODEL), row),
            pl.BlockSpec((tm, LANES), row),
        ],
        out_shape=[
            jax.ShapeDtypeStruct((n, D_MODEL), F32),
            jax.ShapeDtypeStruct((n, D_MODEL), F32),
            jax.ShapeDtypeStruct((n, LANES), F32),
        ],
        compiler_params=pltpu.CompilerParams(
            dimension_semantics=("parallel",), vmem_limit_bytes=VMEM_LIMIT),
        name="outproj",
    )(h, mix, *wo, g, wrh, wrl, br)


MOE_TM = 1024


def _moe_kernel(x_ref, cw_ref, w1_ref, w3_ref, w2_ref, y_ref, xh_ref, xl_ref, *, mode):
    e = pl.program_id(1)

    @pl.when(e == 0)
    def _():
        y_ref[...] = jnp.zeros_like(y_ref)
        xh, xl = _split_bf16(x_ref[...])
        xh_ref[...] = xh
        xl_ref[...] = xl

    cw = cw_ref[...]
    lane = lax.broadcasted_iota(jnp.int32, cw.shape, 1)
    col = jnp.sum(jnp.where(lane == e, cw, 0.0), axis=-1, keepdims=True)
    xh = xh_ref[...]
    w1_hi, w1_lo = _wload((w1_ref,), mode, 0)
    w3_hi, w3_lo = _wload((w3_ref,), mode, 0)
    if w1_lo is None:
        a = _dot(xh, w1_hi)
        b = _dot(xh, w3_hi)
    else:
        xl = xl_ref[...]
        a = _dot3(xh, xl, w1_hi, w1_lo)
        b = _dot3(xh, xl, w3_hi, w3_lo)
    hid = (a * jax.nn.sigmoid(a)) * b
    y_ref[...] += col * _wdot(hid, (w2_ref,), mode, 0)


def _moe(u2, cw, w1, w3, w2, tm):
    n = u2.shape[0]
    mode = 'f32' if w1.dtype == F32 else 'x1'
    return pl.pallas_call(
        functools.partial(_moe_kernel, mode=mode),
        grid=(n // tm, N_EXPERTS),
        in_specs=[
            pl.BlockSpec((tm, D_MODEL), lambda i, e: (i, 0)),
            pl.BlockSpec((tm, LANES), lambda i, e: (i, 0)),
            pl.BlockSpec((1, D_MODEL, D_EXPERT), lambda i, e: (e, 0, 0)),
            pl.BlockSpec((1, D_MODEL, D_EXPERT), lambda i, e: (e, 0, 0)),
            pl.BlockSpec((1, D_EXPERT, D_MODEL), lambda i, e: (e, 0, 0)),
        ],
        out_specs=pl.BlockSpec((tm, D_MODEL), lambda i, e: (i, 0)),
        out_shape=jax.ShapeDtypeStruct((n, D_MODEL), F32),
        scratch_shapes=[pltpu.VMEM((tm, D_MODEL), BF16), pltpu.VMEM((tm, D_MODEL), BF16)],
        compiler_params=pltpu.CompilerParams(
            dimension_semantics=("parallel", "arbitrary"), vmem_limit_bytes=VMEM_LIMIT),
        name="moe",
    )(u2, cw, w1, w3, w2)


def _ple_kernel(h_ref, y_ref, p_ref, g_ref, wg_ref, wp_ref, gf_ref, o_ref, *, final, mode):
    h3 = h_ref[...] + y_ref[...]
    u = _rms(h3, g_ref[...])
    gate = jax.nn.sigmoid(_wdot(u, (wg_ref,), mode))
    h4 = h3 + gate * _wdot(p_ref[...], (wp_ref,), mode)
    if final:
        h4 = _rms(h4, gf_ref[...])
    o_ref[...] = h4


def _ple(h2, y, p, g, wg, wp, gf, tm, final):
    n = h2.shape[0]
    row = lambda i: (i, 0)
    fixed = lambda i: (0, 0)
    mode = 'f32' if wg.dtype == F32 else 'x1'
    return pl.pallas_call(
        functools.partial(_ple_kernel, final=final, mode=mode),
        grid=(n // tm,),
        in_specs=[
            pl.BlockSpec((tm, D_MODEL), row),
            pl.BlockSpec((tm, D_MODEL), row),
            pl.BlockSpec((tm, PLE_DIM), row),
            pl.BlockSpec((1, D_MODEL), fixed),
            pl.BlockSpec((D_MODEL, D_MODEL), fixed),
            pl.BlockSpec((PLE_DIM, D_MODEL), fixed),
            pl.BlockSpec((1, D_MODEL), fixed),
        ],
        out_specs=pl.BlockSpec((tm, D_MODEL), row),
        out_shape=jax.ShapeDtypeStruct((n, D_MODEL), F32),
        compiler_params=pltpu.CompilerParams(
            dimension_semantics=("parallel",), vmem_limit_bytes=VMEM_LIMIT),
        name="ple",
    )(h2, y, p, g, wg, wp, gf)


FOX_LF_LANE = 8
FOX_TB = 256
FOX_TQ = 256


def _split3_bf16(x):
    hi = x.astype(BF16)
    r = x - hi.astype(F32)
    mid = r.astype(BF16)
    lo = (r - mid.astype(F32)).astype(BF16)
    return hi, mid, lo


def _log_sigmoid(x):
    return jnp.minimum(x, 0.0) - jnp.log1p(jnp.exp(-jnp.abs(x)))


def _group_mean(x2, ones_blk, width):
    hi, mid, lo = _split3_bf16(x2)
    return (_dot(hi, ones_blk) + _dot(mid, ones_blk) + _dot(lo, ones_blk)) * (1.0 / width)


def _fox_layout(hp):
    ka = 256 if hp else 128
    a0 = 192 if hp else 64
    nx = 512 + 3 * LANES
    mq = np.zeros((nx, 4 * ka), np.float32)
    mk = np.zeros((nx, 4 * ka), np.float32)
    rq = np.zeros((1, 4 * ka), np.float32)
    rk = np.zeros((1, 4 * ka), np.float32)
    mv = np.zeros((512, 4 * LANES), np.float32)
    for h in range(N_HEADS):
        for d in range(HEAD_DIM):
            src_hi, src_lo = 64 * h + d, 256 + 64 * h + d
            mq[src_hi, h * ka + d] = 1.0
            mk[src_hi, h * ka + d] = 1.0
            if hp:
                mq[src_lo, h * ka + 64 + d] = 1.0
                mq[src_hi, h * ka + 128 + d] = 1.0
                mk[src_hi, h * ka + 64 + d] = 1.0
                mk[src_lo, h * ka + 128 + d] = 1.0
            mv[src_hi, h * LANES + d] = 1.0
            if hp:
                mv[src_lo, h * LANES + 64 + d] = 1.0
        for part in range(3):
            src = 512 + part * LANES + FOX_LF_LANE + h
            mq[src, h * ka + a0 + part] = 1.0
            mk[src, h * ka + a0 + 3 + part] = -1.0
            rq[0, h * ka + a0 + 3 + part] = 1.0
            rk[0, h * ka + a0 + part] = 1.0
    blk = (np.arange(256)[:, None] // HEAD_DIM == np.arange(256)[None, :] // HEAD_DIM).astype(np.float32)
    tri = np.tril(np.ones((FOX_TB, FOX_TB), np.float32))
    as_bf = lambda a: jnp.asarray(a, BF16)
    return dict(ka=ka, mq=as_bf(mq), mk=as_bf(mk), mv=as_bf(mv), rq=jnp.asarray(rq), rk=jnp.asarray(rk),
                blk=as_bf(blk), tri=as_bf(tri))


def _fox_prep_kernel(pc_ref, ps_ref, bf_ref, qn_ref, kn_ref, blk_ref, tri_ref, mq_ref, mk_ref, mv_ref,
                     rq_ref, rk_ref, krow_ref, lf_ref, qa_ref, ka_ref, vv_ref, cb_ref, carry_ref, *, ka):
    @pl.when(pl.program_id(1) == 0)
    def _():
        carry_ref[...] = jnp.zeros_like(carry_ref)

    blk = blk_ref[...]
    q = pc_ref[:, 0:256]
    k = pc_ref[:, 256:512]
    v = pc_ref[:, 512:768]
    qn = q * lax.rsqrt(_group_mean(q * q, blk, HEAD_DIM) + EPS) * qn_ref[...]
    kn = k * lax.rsqrt(_group_mean(k * k, blk, HEAD_DIM) + EPS) * kn_ref[...]
    krow_ref[...] = kn
    lf = _log_sigmoid(ps_ref[...] + bf_ref[...])
    lf_ref[...] = lf
    l_hi, l_mid, l_lo = _split3_bf16(lf)
    tri = tri_ref[...]
    c = _dot(tri, l_hi) + _dot(tri, l_mid) + _dot(tri, l_lo) + carry_ref[...]
    carry_ref[...] = c[FOX_TB - 1:FOX_TB, :]
    r8 = lax.broadcasted_iota(jnp.int32, (8, LANES), 0)
    cb_ref[...] = jnp.where(r8 == 0, c[0:1, :], jnp.where(r8 == 1, c[FOX_TB - 1:FOX_TB, :], 0.0))
    c_hi, c_mid, c_lo = _split3_bf16(c)
    q_hi, q_lo = _split_bf16(qn * (HEAD_DIM ** -0.5))
    k_hi, k_lo = _split_bf16(kn)
    xq = jnp.concatenate([q_hi, q_lo, c_hi, c_mid, c_lo], axis=-1)
    xk = jnp.concatenate([k_hi, k_lo, c_hi, c_mid, c_lo], axis=-1)
    qa = (_dot(xq, mq_ref[...]) + rq_ref[...]).astype(BF16)
    kk = (_dot(xk, mk_ref[...]) + rk_ref[...]).astype(BF16)
    v_hi, v_lo = _split_bf16(v)
    vv = _dot(jnp.concatenate([v_hi, v_lo], axis=-1), mv_ref[...]).astype(BF16)
    for h in range(N_HEADS):
        qa_ref[0, h] = qa[:, h * ka:(h + 1) * ka]
        ka_ref[0, h] = kk[:, h * ka:(h + 1) * ka]
        vv_ref[0, h] = vv[:, h * LANES:(h + 1) * LANES]


def _fox_prep(pm, ps, bf_row, qn_row, kn_row, lay, bsz, t):
    ka = lay['ka']
    nt = t // FOX_TB
    fixed = lambda b, i: (0, 0)
    rows = lambda b, i: (b * nt + i, 0)
    hm = lambda b, i: (b, 0, i, 0)
    return pl.pallas_call(
        functools.partial(_fox_prep_kernel, ka=ka),
        grid=(bsz, nt),
        in_specs=[
            pl.BlockSpec((FOX_TB, 1024), lambda b, i: (b * nt + i, 2)),
            pl.BlockSpec((FOX_TB, LANES), rows),
            pl.BlockSpec((1, LANES), fixed),
            pl.BlockSpec((1, 256), fixed),
            pl.BlockSpec((1, 256), fixed),
            pl.BlockSpec((256, 256), fixed),
            pl.BlockSpec((FOX_TB, FOX_TB), fixed),
            pl.BlockSpec(lay['mq'].shape, fixed),
            pl.BlockSpec(lay['mk'].shape, fixed),
            pl.BlockSpec(lay['mv'].shape, fixed),
            pl.BlockSpec((1, 4 * ka), fixed),
            pl.BlockSpec((1, 4 * ka), fixed),
        ],
        out_specs=[
            pl.BlockSpec((FOX_TB, 256), rows),
            pl.BlockSpec((FOX_TB, LANES), rows),
            pl.BlockSpec((1, N_HEADS, FOX_TB, ka), hm),
            pl.BlockSpec((1, N_HEADS, FOX_TB, ka), hm),
            pl.BlockSpec((1, N_HEADS, FOX_TB, LANES), hm),
            pl.BlockSpec((8, LANES), rows),
        ],
        out_shape=[
            jax.ShapeDtypeStruct((bsz * t, 256), F32),
            jax.ShapeDtypeStruct((bsz * t, LANES), F32),
            jax.ShapeDtypeStruct((bsz, N_HEADS, t, ka), BF16),
            jax.ShapeDtypeStruct((bsz, N_HEADS, t, ka), BF16),
            jax.ShapeDtypeStruct((bsz, N_HEADS, t, LANES), BF16),
            jax.ShapeDtypeStruct((bsz * nt * 8, LANES), F32),
        ],
        scratch_shapes=[pltpu.VMEM((1, LANES), F32)],
        compiler_params=pltpu.CompilerParams(
            dimension_semantics=("parallel", "arbitrary"), vmem_limit_bytes=VMEM_LIMIT),
        name="fox_prep",
    )(pm, ps, bf_row, qn_row, kn_row, lay['blk'], lay['tri'], lay['mq'], lay['mk'], lay['mv'],
      lay['rq'], lay['rk'])


def _fox_flash_kernel(j0_ref, qa_ref, ka_ref, vv_ref, g_ref, on_ref, o_ref, m_s, l_s, acc_s, *, hp):
    i = pl.program_id(1)
    j_first = j0_ref[pl.program_id(0) * pl.num_programs(1) + i]
    tq = FOX_TQ
    row = lax.broadcasted_iota(jnp.int32, (tq, tq), 0)
    col = lax.broadcasted_iota(jnp.int32, (tq, tq), 1)
    m_s[...] = jnp.full(m_s.shape, NEG_BIG, F32)
    l_s[...] = jnp.zeros(l_s.shape, F32)
    acc_s[...] = jnp.zeros(acc_s.shape, F32)

    def tile(j, masked):
        start = pl.multiple_of(j * tq, tq)
        for h in range(N_HEADS):
            kt = ka_ref[0, h, pl.ds(start, tq), :]
            s = lax.dot_general(qa_ref[0, h], kt, (((1,), (1,)), ((), ())), preferred_element_type=F32)
            if masked:
                s = jnp.where(row >= col, s, NEG_BIG)
            m = m_s[h]
            m_new = jnp.maximum(m, jnp.max(s, axis=-1, keepdims=True))
            alpha = jnp.exp(m - m_new)
            p = jnp.exp(s - m_new)
            l_s[h] = alpha * l_s[h] + jnp.sum(p, axis=-1, keepdims=True)
            m_s[h] = m_new
            vt = vv_ref[0, h, pl.ds(start, tq), :]
            if hp:
                p_hi, p_lo = _split_bf16(p)
                pv = _dot(jnp.concatenate([p_hi, p_lo], axis=-1), jnp.concatenate([vt, vt], axis=0))
            else:
                pv = _dot(p.astype(BF16), vt)
            acc_s[h] = alpha * acc_s[h] + pv

    def body(jj, carry):
        tile(j_first + jj, False)
        return carry

    lax.fori_loop(0, i - j_first, body, 0)
    tile(i, True)
    outs = []
    for h in range(N_HEADS):
        acc = acc_s[h]
        o = (acc[:, :HEAD_DIM] + acc[:, HEAD_DIM:]) / l_s[h]
        o = o * lax.rsqrt(jnp.mean(o * o, axis=-1, keepdims=True) + EPS)
        sl = slice(h * HEAD_DIM, (h + 1) * HEAD_DIM)
        outs.append(o * on_ref[:, sl] * jax.nn.sigmoid(g_ref[:, sl]))
    o_ref[...] = jnp.concatenate(outs, axis=-1)


def _fox_flash(j0, qa, ka, vv, pm, on_row, bsz, t, hp):
    kad = qa.shape[-1]
    nq = t // FOX_TQ
    whole = lambda b, i, j0: (b, 0, 0, 0)
    return pl.pallas_call(
        functools.partial(_fox_flash_kernel, hp=hp),
        grid_spec=pltpu.PrefetchScalarGridSpec(
            num_scalar_prefetch=1,
            grid=(bsz, nq),
            in_specs=[
                pl.BlockSpec((1, N_HEADS, FOX_TQ, kad), lambda b, i, j0: (b, 0, i, 0)),
                pl.BlockSpec((1, N_HEADS, t, kad), whole, pipeline_mode=pl.Buffered(1)),
                pl.BlockSpec((1, N_HEADS, t, LANES), whole, pipeline_mode=pl.Buffered(1)),
                pl.BlockSpec((FOX_TQ, 256), lambda b, i, j0: (b * nq + i, 11)),
                pl.BlockSpec((1, 256), lambda b, i, j0: (0, 0)),
            ],
            out_specs=pl.BlockSpec((FOX_TQ, 256), lambda b, i, j0: (b * nq + i, 0)),
            scratch_shapes=[pltpu.VMEM((N_HEADS, FOX_TQ, 1), F32), pltpu.VMEM((N_HEADS, FOX_TQ, 1), F32),
                            pltpu.VMEM((N_HEADS, FOX_TQ, LANES), F32)],
        ),
        out_shape=jax.ShapeDtypeStruct((bsz * t, 256), F32),
        compiler_params=pltpu.CompilerParams(
            dimension_semantics=("parallel", "arbitrary"), vmem_limit_bytes=56 * 1024 * 1024),
        name="fox_flash",
    )(j0.reshape(-1), qa, ka, vv, pm, on_row)


FOX_UNDERFLOW = -120.0


def _fox_first_tile(cb, q_norm, k_norm, bsz, nq):
    cb = cb.reshape(bsz, nq, 8, LANES)[:, :, :, FOX_LF_LANE:FOX_LF_LANE + N_HEADS]
    c_first, c_last = cb[:, :, 0], cb[:, :, 1]
    qk_bound = math.sqrt(HEAD_DIM) * jnp.max(jnp.abs(q_norm)) * jnp.max(jnp.abs(k_norm)) * 1.01
    gap = c_first[:, :, None, :] - c_last[:, None, :, :]
    needed = jnp.any(gap >= FOX_UNDERFLOW - 2.0 * qk_bound, axis=-1)
    tiles = jnp.arange(nq)
    needed = needed | (tiles[None, None, :] >= tiles[None, :, None])
    return jnp.sum(jnp.cumsum(needed.astype(jnp.int32), axis=-1) == 0, axis=-1).astype(jnp.int32)


def _fox_prompt(pm, ps, b_f, q_norm, k_norm, out_norm, bsz, t, hp):
    lay = _fox_layout(hp)
    bf_row = jnp.zeros((1, LANES), F32).at[0, FOX_LF_LANE:FOX_LF_LANE + N_HEADS].set(b_f)
    tile4 = lambda g: jnp.tile(g, N_HEADS).reshape(1, 256)
    krow, lf, qa, ka, vv, cb = _fox_prep(pm, ps, bf_row, tile4(q_norm), tile4(k_norm), lay, bsz, t)
    j0 = _fox_first_tile(cb, q_norm, k_norm, bsz, t // FOX_TQ)
    o = _fox_flash(j0, qa, ka, vv, pm, out_norm.reshape(1, 256), bsz, t, hp)
    return o, krow, lf[:, FOX_LF_LANE:FOX_LF_LANE + N_HEADS]


FOX_PAGES_PER_STEP = 16
_UST = np.tril(np.ones((PAGE_SIZE, PAGE_SIZE), np.float32), k=-1)


def _row_to_col(row, eye):
    return jnp.sum(jnp.where(eye, row, 0.0), axis=-1, keepdims=True)


def _col_to_row(col, eye):
    return jnp.sum(jnp.where(eye, col, 0.0), axis=0, keepdims=True)


def _per_head_rows(x8, rows_per_head, width):
    return jnp.concatenate([jnp.broadcast_to(x8[h:h + 1, :], (rows_per_head, width)) for h in range(N_HEADS)],
                           axis=0)


def _fox_dec_kernel(pt_ref, pc_ref, ps_ref, bf_ref, qn_ref, kn_ref, on_ref, blk_ref, ust_ref, *refs):
    npp = FOX_PAGES_PER_STEP
    k_refs, v_refs, lf_refs = refs[0:npp], refs[npp:2 * npp], refs[2 * npp:3 * npp]
    o_ref, krow_ref, lfrow_ref, qb_s, m_s, l_s, acc_s, carry_s = refs[3 * npp:]
    j = pl.program_id(1)
    hrow = lax.broadcasted_iota(jnp.int32, (8, 256), 0)
    lane = lax.broadcasted_iota(jnp.int32, (8, 256), 1)
    hm = (lane >= hrow * HEAD_DIM) & (lane < (hrow + 1) * HEAD_DIM)
    eye = lax.broadcasted_iota(jnp.int32, (256, 256), 0) == lax.broadcasted_iota(jnp.int32, (256, 256), 1)
    r8 = lax.broadcasted_iota(jnp.int32, (8, LANES), 0)
    l8 = lax.broadcasted_iota(jnp.int32, (8, LANES), 1)
    blk = blk_ref[...]

    @pl.when(j == 0)
    def _():
        q = pc_ref[:, 0:256]
        k = pc_ref[:, 256:512]
        v = pc_ref[:, 512:768]
        qn = q * lax.rsqrt(_group_mean(q * q, blk, HEAD_DIM) + EPS) * qn_ref[...]
        kn = k * lax.rsqrt(_group_mean(k * k, blk, HEAD_DIM) + EPS) * kn_ref[...]
        krow_ref[...] = kn
        lf = _log_sigmoid(ps_ref[...] + bf_ref[...])
        lfrow_ref[...] = lf
        q_row = qn[0:1, :] * (HEAD_DIM ** -0.5)
        qb_s[...] = jnp.broadcast_to(_row_to_col(q_row, eye), (256, LANES))
        m_s[...] = jnp.sum(jnp.where(hm, q_row * kn[0:1, :], 0.0), axis=-1, keepdims=True)
        l_s[...] = jnp.ones_like(l_s)
        lane_full = lax.broadcasted_iota(jnp.int32, (256, LANES), 1)
        acc_s[...] = jnp.where(lane_full == 0, _row_to_col(v[0:1, :], eye), 0.0)
        carry_s[...] = jnp.sum(jnp.where(l8 == r8 + FOX_LF_LANE, lf[0:1, :], 0.0), axis=-1, keepdims=True)

    qb = qb_s[...]
    m, l, carry = m_s[...], l_s[...], carry_s[...]
    lf_all = jnp.concatenate([lf_refs[r][...] for r in range(npp)], axis=0)
    suffix = _mm_exact_r(lf_all, ust_ref[...])
    totals = jnp.sum(lf_all, axis=-1, keepdims=True)
    scores = []
    for r in reversed(range(npp)):
        prod = k_refs[r][...] * qb
        s = jnp.zeros((8, LANES), F32)
        for h in range(N_HEADS):
            s_h = jnp.sum(prod[h * HEAD_DIM:(h + 1) * HEAD_DIM, :], axis=0, keepdims=True)
            s = s + jnp.where(r8 == h, s_h, 0.0)
        scores.append(s + suffix[8 * r:8 * r + 8, :] + carry)
        carry = carry + totals[8 * r:8 * r + 8, :]
    m_new = m
    for s in scores:
        m_new = jnp.maximum(m_new, jnp.max(s, axis=-1, keepdims=True))
    alpha = jnp.exp(m - m_new)
    l = alpha * l
    acc = acc_s[...] * _per_head_rows(alpha, HEAD_DIM, 1)
    for idx, r in enumerate(reversed(range(npp))):
        p = jnp.exp(scores[idx] - m_new)
        l = l + jnp.sum(p, axis=-1, keepdims=True)
        acc = acc + _per_head_rows(p, HEAD_DIM, LANES) * v_refs[r][...]
    m_s[...], l_s[...], acc_s[...], carry_s[...] = m_new, l, acc, carry

    @pl.when(j == pl.num_programs(1) - 1)
    def _():
        o = _col_to_row(jnp.sum(acc, axis=-1, keepdims=True), eye)
        l_row = jnp.sum(jnp.where(hm, l, 0.0), axis=0, keepdims=True)
        o = jnp.broadcast_to(o / l_row, (8, 256))
        o = o * lax.rsqrt(_group_mean(o * o, blk, HEAD_DIM) + EPS)
        o_ref[...] = o * on_ref[...] * jax.nn.sigmoid(pc_ref[:, 768:1024])


def _fox_sample(pmr, psr, cache_k, cache_v, cache_logf, page_table, l, b_f, q_norm, k_norm, out_norm):
    bsz, n_pages = page_table.shape
    npp = FOX_PAGES_PER_STEP
    nsteps = n_pages // npp
    depth, n_phys = cache_k.shape[0], cache_k.shape[1]
    ck = jnp.transpose(cache_k, (0, 1, 3, 4, 2)).reshape(depth, n_phys, 256, PAGE_SIZE)
    cv = jnp.transpose(cache_v, (0, 1, 3, 4, 2)).reshape(depth, n_phys, 256, PAGE_SIZE)
    clf = jnp.pad(jnp.swapaxes(cache_logf, 2, 3), ((0, 0), (0, 0), (0, 8 - N_HEADS), (0, 0)))
    bf_row = jnp.zeros((1, LANES), F32).at[0, FOX_LF_LANE:FOX_LF_LANE + N_HEADS].set(b_f)
    tile4 = lambda g: jnp.tile(g, N_HEADS).reshape(1, 256)
    fixed = lambda b, j, pt: (0, 0)

    def page_spec(r, width):
        return pl.BlockSpec((None, None, width[0], width[1]),
                            lambda b, j, pt, r=r: (l, pt[b, (nsteps - 1 - j) * npp + r], 0, 0))

    in_specs = [
        pl.BlockSpec((8, 1024), lambda b, j, pt: (b, 2)),
        pl.BlockSpec((8, LANES), lambda b, j, pt: (b, 0)),
        pl.BlockSpec((1, LANES), fixed),
        pl.BlockSpec((1, 256), fixed),
        pl.BlockSpec((1, 256), fixed),
        pl.BlockSpec((1, 256), fixed),
        pl.BlockSpec((256, 256), fixed),
        pl.BlockSpec((PAGE_SIZE, PAGE_SIZE), fixed),
    ]
    in_specs += [page_spec(r, (256, PAGE_SIZE)) for r in range(npp)]
    in_specs += [page_spec(r, (256, PAGE_SIZE)) for r in range(npp)]
    in_specs += [page_spec(r, (8, PAGE_SIZE)) for r in range(npp)]
    rows = lambda b, j, pt: (b, 0)
    o, krow, lfrow = pl.pallas_call(
        _fox_dec_kernel,
        grid_spec=pltpu.PrefetchScalarGridSpec(
            num_scalar_prefetch=1,
            grid=(bsz, nsteps),
            in_specs=in_specs,
            out_specs=[pl.BlockSpec((8, 256), rows), pl.BlockSpec((8, 256), rows),
                       pl.BlockSpec((8, LANES), rows)],
            scratch_shapes=[pltpu.VMEM((256, LANES), F32), pltpu.VMEM((8, 1), F32), pltpu.VMEM((8, 1), F32),
                            pltpu.VMEM((256, LANES), F32), pltpu.VMEM((8, 1), F32)],
        ),
        out_shape=[jax.ShapeDtypeStruct((bsz * 8, 256), F32), jax.ShapeDtypeStruct((bsz * 8, 256), F32),
                   jax.ShapeDtypeStruct((bsz * 8, LANES), F32)],
        compiler_params=pltpu.CompilerParams(
            dimension_semantics=("parallel", "arbitrary"), vmem_limit_bytes=VMEM_LIMIT),
        name="fox_sample",
    )(page_table, pmr, psr, bf_row, tile4(q_norm), tile4(k_norm), out_norm.reshape(1, 256),
      jnp.asarray(_BLK256, BF16), jnp.asarray(_UST, BF16), *([ck] * npp), *([cv] * npp), *([clf] * npp))
    first = lambda a: a.reshape(bsz, 8, -1)[:, 0]
    return first(o), first(krow), first(lfrow)[:, FOX_LF_LANE:FOX_LF_LANE + N_HEADS]


CHUNK = 128
ROW0 = 8


def _mm(a, b, hp):
    if hp:
        a_hi, a_lo = _split_bf16(a)
        b_hi, b_lo = _split_bf16(b)
        return _dot3(a_hi, a_lo, b_hi, b_lo)
    return _dot(a.astype(BF16), b.astype(BF16))


def _mm_nt(a, b, hp):
    dn = (((1,), (1,)), ((), ()))
    if hp:
        a_hi, a_lo = _split_bf16(a)
        b_hi, b_lo = _split_bf16(b)
        a3 = jnp.concatenate([a_hi, a_lo, a_hi], axis=-1)
        b3 = jnp.concatenate([b_hi, b_hi, b_lo], axis=-1)
        return lax.dot_general(a3, b3, dn, preferred_element_type=F32)
    return lax.dot_general(a.astype(BF16), b.astype(BF16), dn, preferred_element_type=F32)


def _mm_exact(sel, x):
    hi, mid, lo = _split3_bf16(x)
    return _dot(sel, hi) + _dot(sel, mid) + _dot(sel, lo)


def _mm_exact_r(x, sel):
    hi, mid, lo = _split3_bf16(x)
    return _dot(hi, sel) + _dot(mid, sel) + _dot(lo, sel)


def _softplus(x):
    return jnp.maximum(x, 0.0) + jnp.log1p(jnp.exp(-jnp.abs(x)))


def _silu(x):
    return x * jax.nn.sigmoid(x)


def _stage_rows(buf, blk_ref, lo, hi, prev_ref, rows_in, first):
    @pl.when(first)
    def _():
        buf[ROW0 - 3:ROW0, :] = prev_ref[0]
        if rows_in < CHUNK:
            buf[ROW0 + rows_in:ROW0 + CHUNK, :] = jnp.zeros((CHUNK - rows_in, hi - lo), F32)

    @pl.when(jnp.logical_not(first))
    def _():
        buf[ROW0 - 3:ROW0, :] = buf[ROW0 + CHUNK - 3:ROW0 + CHUNK, :]

    buf[ROW0:ROW0 + rows_in, :] = blk_ref[:, lo:hi]


def _conv4(buf, w_ref):
    acc = w_ref[0:1, :] * buf[ROW0 - 3:ROW0 - 3 + CHUNK, :]
    for j in range(1, CONV_WIDTH):
        acc = acc + w_ref[j:j + 1, :] * buf[ROW0 - 3 + j:ROW0 - 3 + j + CHUNK, :]
    return acc


def _pad_rows(x, rows_in):
    if rows_in == CHUNK:
        return x
    return jnp.concatenate([x, jnp.zeros((CHUNK - rows_in, x.shape[1]), x.dtype)], axis=0)


def _head_expand(first_lane, width):
    e = np.zeros((LANES, N_HEADS * width), np.float32)
    for h in range(N_HEADS):
        e[first_lane + h, h * width:(h + 1) * width] = 1.0
    return jnp.asarray(e, BF16)


_TRI = np.tril(np.ones((CHUNK, CHUNK), np.float32))


SSD_DT_LANE = 12


def _ssd_kernel(pd_ref, ps_ref, cprev_ref, s0_ref, w_ref, cb_ref, dtb_ref, alog_ref, dsk_ref, gn_ref,
                e4_ref, tri_ref, y_ref, sfin_ref, buf, s_scr, *, rows_in, t_valid, hp):
    i = pl.program_id(1)
    first = i == 0

    @pl.when(first)
    def _():
        s_scr[...] = s0_ref[0]

    _stage_rows(buf, pd_ref, 256, 1024, cprev_ref, rows_in, first)
    xbc = _silu(_conv4(buf, w_ref) + cb_ref[...])
    xs = xbc[:, 0:256]
    bm = xbc[:, 256:512]
    cm = xbc[:, 512:768]
    z = _pad_rows(pd_ref[:, 0:256], rows_in)
    pre = _mm_exact_r(_pad_rows(ps_ref[...], rows_in), e4_ref[...])
    dt = _softplus(pre + dtb_ref[...])
    tpos = i * CHUNK + lax.broadcasted_iota(jnp.int32, (CHUNK, 1), 0)
    dt = jnp.where(tpos < t_valid, dt, 0.0)
    la = -jnp.exp(alog_ref[...]) * dt
    b = _mm_exact(tri_ref[...], la)
    row = lax.broadcasted_iota(jnp.int32, (CHUNK, CHUNK), 0)
    col = lax.broadcasted_iota(jnp.int32, (CHUNK, CHUNK), 1)
    causal = row >= col
    lane = lax.broadcasted_iota(jnp.int32, (1, 256), 1)
    s_prev = s_scr[...]
    cb = [_mm_nt(cm[:, g * 128:(g + 1) * 128], bm[:, g * 128:(g + 1) * 128], hp) for g in range(D_GROUPS)]
    y = jnp.zeros((CHUNK, 256), F32)
    s_new = jnp.zeros((D_STATE, 256), F32)
    for h in range(N_HEADS):
        g = h // (N_HEADS // D_GROUPS)
        bh = b[:, h * 128:(h + 1) * 128]
        dth = dt[:, h * 128:(h + 1) * 128]
        hmask = (lane >= h * HEAD_DIM) & (lane < (h + 1) * HEAD_DIM)
        dt2 = jnp.concatenate([dth, dth], axis=-1)
        xdt = jnp.where(hmask, xs * dt2, 0.0)
        rel = jnp.where(causal, jnp.exp(jnp.where(causal, bh - bh.T, 0.0)), 0.0)
        y = y + _mm(cb[g] * rel, xdt, hp)
        eb = jnp.exp(bh)
        y = y + _mm(cm[:, g * 128:(g + 1) * 128] * eb, jnp.where(hmask, s_prev, 0.0), hp)
        b_last = bh[CHUNK - 1:CHUNK, :]
        kdec = bm[:, g * 128:(g + 1) * 128] * jnp.exp(b_last - bh)
        s_new = s_new + _mm(kdec.T, xdt, hp)
        a2 = jnp.exp(jnp.concatenate([b_last, b_last], axis=-1))
        s_new = s_new + jnp.where(hmask, a2 * s_prev, 0.0)
    s_scr[...] = s_new
    sfin_ref[0] = s_new
    y = (y + xs * dsk_ref[...]) * _silu(z)
    outs = []
    for g in range(D_GROUPS):
        yg = y[:, g * 128:(g + 1) * 128]
        outs.append(yg * lax.rsqrt(jnp.mean(yg * yg, axis=-1, keepdims=True) + EPS))
    yn = jnp.concatenate(outs, axis=-1) * gn_ref[...]
    y_ref[...] = yn[0:rows_in, :]


def _ssd(pm, ps, conv_prev, s0, conv_w, conv_b, a_log, dt_bias, d_skip, norm_g, bsz, t_valid, rows_in, hp):
    nt = pm.shape[0] // (bsz * rows_in)
    fixed = lambda b, i: (0, 0)
    rows = lambda b, i: (b * nt + i, 0)
    rep128 = lambda v: jnp.repeat(v, 128).reshape(1, 512)
    s0l = jnp.transpose(s0, (0, 2, 1, 3)).reshape(bsz, D_STATE, 256)
    y, sfin = pl.pallas_call(
        functools.partial(_ssd_kernel, rows_in=rows_in, t_valid=t_valid, hp=hp),
        grid=(bsz, nt),
        in_specs=[
            pl.BlockSpec((rows_in, 1024), lambda b, i: (b * nt + i, 3)),
            pl.BlockSpec((rows_in, LANES), rows),
            pl.BlockSpec((1, 3, D_CONV_CH), lambda b, i: (b, 0, 0)),
            pl.BlockSpec((1, D_STATE, 256), lambda b, i: (b, 0, 0)),
            pl.BlockSpec((CONV_WIDTH, D_CONV_CH), fixed),
            pl.BlockSpec((1, D_CONV_CH), fixed),
            pl.BlockSpec((1, 512), fixed),
            pl.BlockSpec((1, 512), fixed),
            pl.BlockSpec((1, 256), fixed),
            pl.BlockSpec((1, 256), fixed),
            pl.BlockSpec((LANES, 512), fixed),
            pl.BlockSpec((CHUNK, CHUNK), fixed),
        ],
        out_specs=[
            pl.BlockSpec((rows_in, 256), rows),
            pl.BlockSpec((1, D_STATE, 256), lambda b, i: (b, 0, 0)),
        ],
        out_shape=[
            jax.ShapeDtypeStruct((pm.shape[0], 256), F32),
            jax.ShapeDtypeStruct((bsz, D_STATE, 256), F32),
        ],
        scratch_shapes=[pltpu.VMEM((ROW0 + CHUNK, D_CONV_CH), F32), pltpu.VMEM((D_STATE, 256), F32)],
        compiler_params=pltpu.CompilerParams(
            dimension_semantics=("parallel", "arbitrary"), vmem_limit_bytes=VMEM_LIMIT),
        name="ssd",
    )(pm, ps, conv_prev, s0l, conv_w, conv_b.reshape(1, -1), rep128(dt_bias), rep128(a_log),
      jnp.repeat(d_skip, HEAD_DIM).reshape(1, 256), norm_g.reshape(1, 256),
      _head_expand(SSD_DT_LANE, 128), jnp.asarray(_TRI, BF16))
    return y, jnp.transpose(sfin.reshape(bsz, D_STATE, N_HEADS, HEAD_DIM), (0, 2, 1, 3))


SUB = CHUNK_A
_SUB_ID = np.arange(CHUNK) // SUB
_SAME_SUB = (_SUB_ID[:, None] == _SUB_ID[None, :]).astype(np.float32)
_BLK256 = (np.arange(256)[:, None] // HEAD_DIM == np.arange(256)[None, :] // HEAD_DIM).astype(np.float32)


def _mm_tn(a, b, hp):
    dn = (((0,), (0,)), ((), ()))
    if hp:
        a_hi, a_lo = _split_bf16(a)
        b_hi, b_lo = _split_bf16(b)
        a3 = jnp.concatenate([a_hi, a_lo, a_hi], axis=0)
        b3 = jnp.concatenate([b_hi, b_hi, b_lo], axis=0)
        return lax.dot_general(a3, b3, dn, preferred_element_type=F32)
    return lax.dot_general(a.astype(BF16), b.astype(BF16), dn, preferred_element_type=F32)


def _group_sum(x, ones_blk, hp):
    if hp:
        return _mm_exact_r(x, ones_blk)
    return _dot(x.astype(BF16), ones_blk)


def _hgrn_kernel(pa_ref, s0_ref, c1_ref, c2_ref, oml_ref, gn_ref, blk_ref, t16_ref, l16_ref,
                 o_ref, sfin_ref, kbuf, vbuf, lbuf, st_scr, *, rows_in, t_valid, hp):
    i = pl.program_id(1)

    @pl.when(i == 0)
    def _():
        st_scr[...] = s0_ref[0]
        zeros = jnp.zeros((SUB, 256), F32)
        kbuf[0:SUB, :] = zeros
        vbuf[0:SUB, :] = zeros
        lbuf[0:SUB, :] = zeros

    q = _pad_rows(pa_ref[:, 0:256], rows_in) * (HEAD_DIM ** -0.5)
    zf = _pad_rows(pa_ref[:, 256:512], rows_in)
    v = _pad_rows(pa_ref[:, 512:768], rows_in)
    g = _pad_rows(pa_ref[:, 768:1024], rows_in)
    la = c1_ref[...]
    lb = c2_ref[...] + _log_sigmoid(zf)
    lf = jnp.maximum(la, lb) + jnp.log1p(jnp.exp(-jnp.abs(la - lb)))
    k = oml_ref[...] * jax.nn.sigmoid(-zf)
    tpos = i * CHUNK + lax.broadcasted_iota(jnp.int32, (CHUNK, 1), 0)
    valid = tpos < t_valid
    lf = jnp.where(valid, lf, 0.0)
    k = jnp.where(valid, k, 0.0)
    v = jnp.where(valid, v, 0.0)
    kbuf[SUB:SUB + CHUNK, :] = k
    vbuf[SUB:SUB + CHUNK, :] = v
    lbuf[SUB:SUB + CHUNK, :] = lf

    blk = blk_ref[...]
    sub = lax.broadcasted_iota(jnp.int32, (CHUNK, 1), 0) % SUB
    o = _group_sum(q * k, blk, hp) * v
    bd = jnp.zeros((CHUNK, 256), F32)
    for d in range(1, min(SUB, t_valid)):
        bd = bd + lbuf[SUB - d + 1:SUB - d + 1 + CHUNK, :]
        m = sub >= d
        ks = kbuf[SUB - d:SUB - d + CHUNK, :]
        tmp = jnp.where(m, q * ks * jnp.exp(jnp.where(m, bd, 0.0)), 0.0)
        o = o + _group_sum(tmp, blk, hp) * vbuf[SUB - d:SUB - d + CHUNK, :]

    b = _mm_exact(t16_ref[...], lf)
    bl = _mm_exact(l16_ref[...], lf)
    qe = q * jnp.exp(b)
    kd = k * jnp.exp(bl - b)
    st = st_scr[...]
    bdmask = blk > 0
    o_rows = []
    for n in range(CHUNK // SUB):
        r = slice(n * SUB, (n + 1) * SUB)
        o_rows.append(_mm_nt(qe[r], st, hp))
        ds = _mm_tn(v[r], kd[r], hp)
        st = st * jnp.exp(bl[n * SUB:n * SUB + 1, :]) + jnp.where(bdmask, ds, 0.0)
    st_scr[...] = st
    sfin_ref[0] = st
    o = o + jnp.concatenate(o_rows, axis=0)
    o = o * lax.rsqrt(_group_mean(o * o, blk, HEAD_DIM) + EPS) * gn_ref[...] * _silu(g)
    o_ref[...] = o[0:rows_in, :]


def _hgrn(pm, lb, norm_g, s0, bsz, t_valid, rows_in, hp):
    nt = pm.shape[0] // (bsz * rows_in)
    fixed = lambda b, i: (0, 0)
    rows = lambda b, i: (b * nt + i, 0)
    lb = jnp.clip(lb, 0.0, LB_CEIL)
    c1 = jnp.log(jnp.maximum(lb, LB_FLOOR)).reshape(1, 256)
    c2 = jnp.log1p(-lb).reshape(1, 256)
    oml = (1.0 - lb).reshape(1, 256)
    eye = jnp.eye(N_HEADS, dtype=bool)[None, :, None, :, None]
    st0 = jnp.where(eye, jnp.swapaxes(s0, 2, 3)[:, :, :, None, :], 0.0).reshape(bsz, 256, 256)
    o, sfin = pl.pallas_call(
        functools.partial(_hgrn_kernel, rows_in=rows_in, t_valid=t_valid, hp=hp),
        grid=(bsz, nt),
        in_specs=[
            pl.BlockSpec((rows_in, 1024), lambda b, i: (b * nt + i, 0)),
            pl.BlockSpec((1, 256, 256), lambda b, i: (b, 0, 0)),
            pl.BlockSpec((1, 256), fixed),
            pl.BlockSpec((1, 256), fixed),
            pl.BlockSpec((1, 256), fixed),
            pl.BlockSpec((1, 256), fixed),
            pl.BlockSpec((256, 256), fixed),
            pl.BlockSpec((CHUNK, CHUNK), fixed),
            pl.BlockSpec((CHUNK, CHUNK), fixed),
        ],
        out_specs=[
            pl.BlockSpec((rows_in, 256), rows),
            pl.BlockSpec((1, 256, 256), lambda b, i: (b, 0, 0)),
        ],
        out_shape=[
            jax.ShapeDtypeStruct((pm.shape[0], 256), F32),
            jax.ShapeDtypeStruct((bsz, 256, 256), F32),
        ],
        scratch_shapes=[pltpu.VMEM((SUB + CHUNK, 256), F32)] * 3 + [pltpu.VMEM((256, 256), F32)],
        compiler_params=pltpu.CompilerParams(
            dimension_semantics=("parallel", "arbitrary"), vmem_limit_bytes=VMEM_LIMIT),
        name="hgrn",
    )(pm, st0, c1, c2, oml, norm_g.reshape(1, 256), jnp.asarray(_BLK256, BF16),
      jnp.asarray(_TRI * _SAME_SUB, BF16), jnp.asarray(_SAME_SUB, BF16))
    sf = sfin.reshape(bsz, N_HEADS, HEAD_DIM, N_HEADS, HEAD_DIM)
    sf = jnp.stack([sf[:, h, :, h, :] for h in range(N_HEADS)], axis=1)
    return o, jnp.swapaxes(sf, 2, 3)


GDN_BETA_LANE = 0
GDN_DT_LANE = 4


def _block_diag_state(s0, bsz):
    eye = jnp.eye(N_HEADS, dtype=bool)[None, :, None, :, None]
    return jnp.where(eye, jnp.swapaxes(s0, 2, 3)[:, :, :, None, :], 0.0).reshape(bsz, 256, 256)


def _unblock_diag_state(st, bsz):
    sf = st.reshape(bsz, N_HEADS, HEAD_DIM, N_HEADS, HEAD_DIM)
    sf = jnp.stack([sf[:, h, :, h, :] for h in range(N_HEADS)], axis=1)
    return jnp.swapaxes(sf, 2, 3)


def _gdn_kernel(pb_ref, ps_ref, cprev_ref, s0_ref, w_ref, dtb_ref, alog_ref, dtb64_ref, alog64_ref, gn_ref,
                blk_ref, tri_ref, eb128_ref, ed128_ref, eb64_ref, ed64_ref,
                o_ref, sfin_ref, buf, st_scr, *, rows_in, t_valid, hp):
    i = pl.program_id(1)
    first = i == 0

    @pl.when(first)
    def _():
        st_scr[...] = s0_ref[0]

    _stage_rows(buf, pb_ref, 0, 768, cprev_ref, rows_in, first)
    qkv = _silu(_conv4(buf, w_ref))
    blk = blk_ref[...]
    q = qkv[:, 0:256]
    k = qkv[:, 256:512]
    v = qkv[:, 512:768]
    q = q * lax.rsqrt(_group_mean(q * q, blk, 1) + EPS) * (HEAD_DIM ** -0.5)
    k = k * lax.rsqrt(_group_mean(k * k, blk, 1) + EPS)
    gate = _pad_rows(pb_ref[:, 768:1024], rows_in)
    ps = _pad_rows(ps_ref[...], rows_in)
    tpos = i * CHUNK + lax.broadcasted_iota(jnp.int32, (CHUNK, 1), 0)
    valid = tpos < t_valid
    tri = tri_ref[...]
    beta128 = jnp.where(valid, jax.nn.sigmoid(_mm_exact_r(ps, eb128_ref[...])), 0.0)
    la128 = jnp.where(valid, -jnp.exp(alog_ref[...]) * _softplus(_mm_exact_r(ps, ed128_ref[...]) + dtb_ref[...]), 0.0)
    b128 = _mm_exact(tri, la128)
    beta64 = jnp.where(valid, jax.nn.sigmoid(_mm_exact_r(ps, eb64_ref[...])), 0.0)
    la64 = jnp.where(valid, -jnp.exp(alog64_ref[...]) * _softplus(_mm_exact_r(ps, ed64_ref[...]) + dtb64_ref[...]), 0.0)
    b64 = _mm_exact(tri, la64)
    eb64 = jnp.exp(b64)
    b_last64 = b64[CHUNK - 1:CHUNK, :]
    kb = k * beta64
    rv = v * beta64
    rk = kb * eb64
    q_dec = q * eb64
    k_dec = k * jnp.exp(b_last64 - b64)

    row = lax.broadcasted_iota(jnp.int32, (CHUNK, CHUNK), 0)
    col = lax.broadcasted_iota(jnp.int32, (CHUNK, CHUNK), 1)
    causal = row >= col
    strict = row > col
    lane = lax.broadcasted_iota(jnp.int32, (1, 256), 1)
    a_mats, rhs, atts = [], [], []
    for h in range(N_HEADS):
        hs = slice(h * HEAD_DIM, (h + 1) * HEAD_DIM)
        hmask = (lane >= h * HEAD_DIM) & (lane < (h + 1) * HEAD_DIM)
        bh = b128[:, h * 128:(h + 1) * 128]
        decay = jnp.where(causal, jnp.exp(jnp.where(causal, bh - bh.T, 0.0)), 0.0)
        k_h = jnp.where(hmask, k, 0.0)
        a_mats.append(-jnp.where(strict, _mm_nt(jnp.where(hmask, kb, 0.0), k_h, True) * decay, 0.0))
        atts.append(_mm_nt(jnp.where(hmask, q, 0.0), k_h, hp) * decay)
        rhs.append(jnp.concatenate([rv[:, hs], rk[:, hs]], axis=-1))
    zero = jnp.zeros((CHUNK, CHUNK), F32)
    xs = []
    for pair in range(N_HEADS // 2):
        h0, h1 = 2 * pair, 2 * pair + 1
        a = jnp.concatenate([jnp.concatenate([a_mats[h0], zero], axis=-1),
                             jnp.concatenate([zero, a_mats[h1]], axis=-1)], axis=0)
        x = jnp.concatenate([rhs[h0], rhs[h1]], axis=0)
        if t_valid > 1:
            x = x + _mm(a, x, True)
            for _ in range(6):
                a = _mm(a, a, True)
                x = x + _mm(a, x, True)
        xs += [x[:CHUNK], x[CHUNK:]]
    u_all = jnp.concatenate([x[:, :HEAD_DIM] for x in xs], axis=-1)
    w_all = jnp.concatenate([x[:, HEAD_DIM:] for x in xs], axis=-1)
    st = st_scr[...]
    v_new = u_all - _mm_nt(w_all, st, hp)
    o = _mm_nt(q_dec, st, hp)
    for h in range(N_HEADS):
        hmask = (lane >= h * HEAD_DIM) & (lane < (h + 1) * HEAD_DIM)
        o = o + _mm(atts[h], jnp.where(hmask, v_new, 0.0), hp)
    st = st * jnp.exp(b_last64) + jnp.where(blk > 0, _mm_tn(v_new, k_dec, hp), 0.0)
    st_scr[...] = st
    sfin_ref[0] = st
    o = o * lax.rsqrt(_group_mean(o * o, blk, HEAD_DIM) + EPS) * gn_ref[...] * _silu(gate)
    o_ref[...] = o[0:rows_in, :]


def _gdn(pm, ps, conv_prev, s0, conv_w, a_log, dt_bias, norm_g, bsz, t_valid, rows_in, hp):
    nt = pm.shape[0] // (bsz * rows_in)
    fixed = lambda b, i: (0, 0)
    rows = lambda b, i: (b * nt + i, 0)
    rep = lambda v, w: jnp.repeat(v, w).reshape(1, N_HEADS * w)
    o, sfin = pl.pallas_call(
        functools.partial(_gdn_kernel, rows_in=rows_in, t_valid=t_valid, hp=hp),
        grid=(bsz, nt),
        in_specs=[
            pl.BlockSpec((rows_in, 1024), lambda b, i: (b * nt + i, 1)),
            pl.BlockSpec((rows_in, LANES), rows),
            pl.BlockSpec((1, 3, B_CONV_CH), lambda b, i: (b, 0, 0)),
            pl.BlockSpec((1, 256, 256), lambda b, i: (b, 0, 0)),
            pl.BlockSpec((CONV_WIDTH, B_CONV_CH), fixed),
            pl.BlockSpec((1, 512), fixed),
            pl.BlockSpec((1, 512), fixed),
            pl.BlockSpec((1, 256), fixed),
            pl.BlockSpec((1, 256), fixed),
            pl.BlockSpec((1, 256), fixed),
            pl.BlockSpec((256, 256), fixed),
            pl.BlockSpec((CHUNK, CHUNK), fixed),
            pl.BlockSpec((LANES, 512), fixed),
            pl.BlockSpec((LANES, 512), fixed),
            pl.BlockSpec((LANES, 256), fixed),
            pl.BlockSpec((LANES, 256), fixed),
        ],
        out_specs=[
            pl.BlockSpec((rows_in, 256), rows),
            pl.BlockSpec((1, 256, 256), lambda b, i: (b, 0, 0)),
        ],
        out_shape=[
            jax.ShapeDtypeStruct((pm.shape[0], 256), F32),
            jax.ShapeDtypeStruct((bsz, 256, 256), F32),
        ],
        scratch_shapes=[pltpu.VMEM((ROW0 + CHUNK, B_CONV_CH), F32), pltpu.VMEM((256, 256), F32)],
        compiler_params=pltpu.CompilerParams(
            dimension_semantics=("parallel", "arbitrary"), vmem_limit_bytes=VMEM_LIMIT),
        name="gdn",
    )(pm, ps, conv_prev, _block_diag_state(s0, bsz), conv_w, rep(dt_bias, 128), rep(a_log, 128),
      rep(dt_bias, 64), rep(a_log, 64), norm_g.reshape(1, 256), jnp.asarray(_BLK256, BF16),
      jnp.asarray(_TRI, BF16), _head_expand(GDN_BETA_LANE, 128), _head_expand(GDN_DT_LANE, 128),
      _head_expand(GDN_BETA_LANE, 64), _head_expand(GDN_DT_LANE, 64))
    return o, _unblock_diag_state(sfin, bsz)


def _prep_weights(prm):
    w_in = prm['w_in']
    wm = w_in[:, :, _MAIN_COLS]
    ws = jnp.pad(w_in[:, :, _SMALL_COLS], ((0, 0), (0, 0), (0, LANES - len(_SMALL_COLS))))
    wr = jnp.concatenate([prm['moe_w_expert'], prm['moe_w_group']], axis=-1)
    n_r = N_EXPERTS + N_EXPERT_GROUPS
    wr = jnp.pad(wr, ((0, 0), (0, 0), (0, LANES - n_r)))
    both = lambda pair: [(pair[0][:D_MODEL], pair[1][:D_MODEL]), (pair[0][D_MODEL:], pair[1][D_MODEL:])]
    ws_split = both(_split_weight(ws.reshape(DEPTH * D_MODEL, LANES)))
    wr_split = both(_split_weight(wr.reshape(DEPTH * D_MODEL, LANES)))
    br = jnp.pad(jnp.concatenate([prm['moe_b_expert'], prm['moe_b_group']], axis=-1),
                 ((0, 0), (0, LANES - n_r)))[:, None, :]
    bf = lambda a: a.astype(BF16)
    prompt, sample = [], []
    for l in range(DEPTH):
        common = dict(ws=ws_split[l], wr=wr_split[l], br=br[l])
        if l == 0:
            wm_p, wo_p = tuple(_split_weight(wm[l])), tuple(_split_weight(prm['w_out'][l]))
        else:
            wm_p, wo_p = (bf(wm[l]),), (bf(prm['w_out'][l]),)
        prompt.append(dict(common, wm=wm_p, wo=wo_p, w1=bf(prm['moe_w1'][l]), w3=bf(prm['moe_w3'][l]),
                           w2=bf(prm['moe_w2'][l]), wg=bf(prm['ple_w_gate'][l]), wp=bf(prm['ple_w_proj'][l])))
        sample.append(dict(common, wm=(wm[l],), wo=(prm['w_out'][l],), w1=prm['moe_w1'][l], w3=prm['moe_w3'][l],
                           w2=prm['moe_w2'][l], wg=prm['ple_w_gate'][l], wp=prm['ple_w_proj'][l]))
    return prompt, sample


def _trunk(x, p, init_state, fox_cache, lb_all, prm, wts, tm, hp_layers):
    s_hgrn0, s_gdn0, c_gdn0, s_ssd0, c_ssd0 = init_state
    bsz, t, _ = x.shape
    n = bsz * t
    h = x.reshape(n, D_MODEL)
    outs = [[] for _ in range(8)]
    row = lambda a: a.reshape(1, -1)
    for l in range(DEPTH):
        hp_mix = hp_layers[l]
        w = wts[l]
        pm, ps = _inproj(h, row(prm['g_mix'][l]), w['wm'], w['ws'][0], w['ws'][1], tm)
        if fox_cache is None:
            o_c, k_c, lf_c = _fox_prompt(pm, ps, prm['fox_b_f'][l], prm['fox_q_norm'][l], prm['fox_k_norm'][l],
                                         prm['fox_out_norm'][l], bsz, t, hp_mix)
            o_c = o_c.reshape(bsz, t, GROUP_WIDTH)
            k_c = k_c.reshape(bsz, t, N_HEADS, HEAD_DIM)
            lf_c = lf_c.reshape(bsz, t, N_HEADS)
            v_c = pm[:, 2560:2816].reshape(bsz, t, N_HEADS, HEAD_DIM)
        rows_in = CHUNK if t % CHUNK == 0 else 8
        if rows_in == CHUNK:
            pmr, psr = pm, ps
        else:
            padr = lambda a: jnp.pad(a.reshape(bsz, t, -1), ((0, 0), (0, rows_in - t), (0, 0))).reshape(
                bsz * rows_in, -1)
            pmr, psr = padr(pm), padr(ps)
        unpad = lambda a: a.reshape(bsz, -1, GROUP_WIDTH)[:, :t]
        o_a, s_a = _hgrn(pmr, lb_all[l], prm['hgrn_norm'][l], s_hgrn0[l], bsz, t, rows_in, hp_mix)
        o_d, s_d = _ssd(pmr, psr, c_ssd0[l], s_ssd0[l], prm['ssd_conv_w'][l], prm['ssd_conv_b'][l],
                        prm['ssd_a_log'][l], prm['ssd_dt_bias'][l], prm['ssd_d'][l], prm['ssd_norm'][l],
                        bsz, t, rows_in, hp_mix)
        o_b, s_b = _gdn(pmr, psr, c_gdn0[l], s_gdn0[l], prm['gdn_conv_w'][l], prm['gdn_a_log'][l],
                        prm['gdn_dt_bias'][l], prm['gdn_norm'][l], bsz, t, rows_in, hp_mix)
        o_a, o_b, o_d = unpad(o_a), unpad(o_b), unpad(o_d)
        pm = pm.reshape(bsz, t, N_MAIN)
        ps = ps.reshape(bsz, t, LANES)
        c_d = jnp.concatenate([c_ssd0[l], pm[:, max(0, t - 3):, 3328:4096]], axis=1)[:, -(CONV_WIDTH - 1):]
        c_b = jnp.concatenate([c_gdn0[l], pm[:, max(0, t - 3):, 1024:1792]], axis=1)[:, -(CONV_WIDTH - 1):]
        if fox_cache is not None:
            cache_k, cache_v, cache_logf, page_table = fox_cache
            o_c, k_c, lf_c = _fox_sample(pmr, psr, cache_k, cache_v, cache_logf, page_table, l,
                                         prm['fox_b_f'][l], prm['fox_q_norm'][l], prm['fox_k_norm'][l],
                                         prm['fox_out_norm'][l])
            o_c = o_c.reshape(bsz, t, GROUP_WIDTH)
            k_c = k_c.reshape(bsz, t, N_HEADS, HEAD_DIM)
            lf_c = lf_c.reshape(bsz, t, N_HEADS)
            v_c = pm[..., 2560:2816].reshape(bsz, t, N_HEADS, HEAD_DIM)
        mix = jnp.concatenate([o_a, o_b, o_c, o_d], axis=-1).reshape(n, D_MODEL)
        h2, u2, cw = _outproj(h, mix, w['wo'], row(prm['g_ffn'][l]), w['wr'][0], w['wr'][1], w['br'], tm)
        y = _moe(u2, cw, w['w1'], w['w3'], w['w2'], MOE_TM if n % MOE_TM == 0 else tm)
        h = _ple(h2, y, p[l].reshape(n, PLE_DIM), row(prm['g_ple'][l]), w['wg'], w['wp'],
                 row(prm['g_final']), tm, final=(l == DEPTH - 1))
        for acc, val in zip(outs, (k_c, v_c, lf_c, s_a, s_b, c_b, s_d, c_d)):
            acc.append(val)
    return (h.reshape(bsz, t, D_MODEL),) + tuple(jnp.stack(acc) for acc in outs)


def _hgrn_lower_bounds(lb_param):
    sm = jax.nn.softmax(lb_param, axis=0)
    return jnp.concatenate([jnp.zeros_like(sm[:1]), jnp.cumsum(sm[1:], axis=0)], axis=0)


def kernel(x_prompt, x_sample, cache_fox_k, cache_fox_v, cache_fox_logf, state_hgrn, state_gdn,
           state_gdn_conv, state_ssd, state_ssd_conv, page_table, p_prompt, p_sample, w_in, w_out,
           g_mix, g_ffn, g_ple, g_final, hgrn_lb, hgrn_norm, gdn_conv_w, gdn_a_log, gdn_dt_bias,
           gdn_norm, fox_b_f, fox_q_norm, fox_k_norm, fox_out_norm, ssd_conv_w, ssd_conv_b, ssd_a_log,
           ssd_dt_bias, ssd_d, ssd_norm, moe_w_group, moe_b_group, moe_w_expert, moe_b_expert, moe_w1,
           moe_w3, moe_w2, ple_w_gate, ple_w_proj):
    prm = dict(w_in=w_in, w_out=w_out, g_mix=g_mix, g_ffn=g_ffn, g_ple=g_ple, g_final=g_final,
               hgrn_norm=hgrn_norm, gdn_conv_w=gdn_conv_w, gdn_a_log=gdn_a_log, gdn_dt_bias=gdn_dt_bias,
               gdn_norm=gdn_norm, fox_b_f=fox_b_f, fox_q_norm=fox_q_norm, fox_k_norm=fox_k_norm,
               fox_out_norm=fox_out_norm, ssd_conv_w=ssd_conv_w, ssd_conv_b=ssd_conv_b, ssd_a_log=ssd_a_log,
               ssd_dt_bias=ssd_dt_bias, ssd_d=ssd_d, ssd_norm=ssd_norm, moe_w_group=moe_w_group,
               moe_b_group=moe_b_group, moe_w_expert=moe_w_expert, moe_b_expert=moe_b_expert,
               moe_w1=moe_w1, moe_w3=moe_w3, moe_w2=moe_w2, ple_w_gate=ple_w_gate, ple_w_proj=ple_w_proj)
    wts_prompt, wts_sample = _prep_weights(prm)
    lb_all = _hgrn_lower_bounds(hgrn_lb)
    bp = x_prompt.shape[0]
    zero_state = (jnp.zeros((DEPTH, bp, N_HEADS, HEAD_DIM, HEAD_DIM), F32),
                  jnp.zeros((DEPTH, bp, N_HEADS, HEAD_DIM, HEAD_DIM), F32),
                  jnp.zeros((DEPTH, bp, CONV_WIDTH - 1, B_CONV_CH), F32),
                  jnp.zeros((DEPTH, bp, N_HEADS, D_STATE, HEAD_DIM), F32),
                  jnp.zeros((DEPTH, bp, CONV_WIDTH - 1, D_CONV_CH), F32))
    pr = _trunk(x_prompt, p_prompt, zero_state, None, lb_all, prm, wts_prompt, tm=512, hp_layers=(True, False))
    sm = _trunk(x_sample, p_sample, (state_hgrn, state_gdn, state_gdn_conv, state_ssd, state_ssd_conv),
                (cache_fox_k, cache_fox_v, cache_fox_logf, page_table), lb_all, prm, wts_sample, tm=32,
                hp_layers=(True, True))
    return (pr[0], sm[0]) + tuple(pr[1:]) + tuple(sm[1:])
```

```python
import functools
import math

import jax
import jax.numpy as jnp
import numpy as np
from jax import lax
from jax.experimental import pallas as pl
from jax.experimental.pallas import tpu as pltpu

F32 = jnp.float32
BF16 = jnp.bfloat16

D_MODEL = 1024
DEPTH = 2
PAGE_SIZE = 128
EPS = 1e-6
NEG_BIG = -1e30
LB_FLOOR = 1e-30
LB_CEIL = 1.0 - 1e-6
PLE_DIM = 256
GROUP_WIDTH = 256
HEAD_DIM = 64
N_HEADS = 4
D_GROUPS = 2
D_STATE = 128
CONV_WIDTH = 4
B_CONV_CH = 3 * GROUP_WIDTH
D_CONV_CH = GROUP_WIDTH + 2 * D_GROUPS * D_STATE
SIZE_A = 4 * GROUP_WIDTH
SIZE_B = 4 * GROUP_WIDTH + 2 * N_HEADS
SIZE_C = 4 * GROUP_WIDTH + N_HEADS
SIZE_D = GROUP_WIDTH + D_CONV_CH + N_HEADS
OFF_B = SIZE_A
OFF_C = OFF_B + SIZE_B
OFF_D = OFF_C + SIZE_C
N_IN = OFF_D + SIZE_D
CHUNK_A = 16
CHUNK_B = 64
CHUNK_D = 128
Q_BLOCK = 128
N_EXPERT_GROUPS = 4
EXPERTS_PER_GROUP = 4
N_EXPERTS = 16
D_EXPERT = 512

LANES = 128
N_MAIN = 4096
VMEM_LIMIT = 48 * 1024 * 1024

_MAIN_COLS = np.concatenate([
    np.arange(0, SIZE_A),
    np.arange(OFF_B, OFF_B + 4 * GROUP_WIDTH),
    np.arange(OFF_C, OFF_C + 4 * GROUP_WIDTH),
    np.arange(OFF_D, OFF_D + 4 * GROUP_WIDTH),
])
_SMALL_COLS = np.concatenate([
    np.arange(OFF_B + 4 * GROUP_WIDTH, OFF_B + SIZE_B),
    np.arange(OFF_C + 4 * GROUP_WIDTH, OFF_C + SIZE_C),
    np.arange(OFF_D + 4 * GROUP_WIDTH, OFF_D + SIZE_D),
])


def _split_bf16(x):
    hi = x.astype(BF16)
    lo = (x - hi.astype(F32)).astype(BF16)
    return hi, lo


def _dot(a, b):
    return jnp.dot(a, b, preferred_element_type=F32)


def _dot3(a_hi, a_lo, b_hi, b_lo):
    return _dot(a_hi, b_hi) + _dot(a_lo, b_hi) + _dot(a_hi, b_lo)


def _rms(x, g):
    return x * lax.rsqrt(jnp.mean(x * x, axis=-1, keepdims=True) + EPS) * g


def _wmode(w):
    if len(w) == 2:
        return 'x3'
    return 'f32' if w[0].dtype == F32 else 'x1'


def _wcount(mode):
    return 2 if mode == 'x3' else 1


def _wload(refs, mode, idx=None):
    get = (lambda r: r[...]) if idx is None else (lambda r: r[idx])
    if mode == 'x1':
        return get(refs[0]), None
    if mode == 'x3':
        return get(refs[0]), get(refs[1])
    return _split_bf16(get(refs[0]))


def _wdot(a, refs, mode, idx=None):
    w_hi, w_lo = _wload(refs, mode, idx)
    if w_lo is None:
        return _dot(a.astype(BF16), w_hi)
    a_hi, a_lo = _split_bf16(a)
    return _dot3(a_hi, a_lo, w_hi, w_lo)


def _split_kernel(w_ref, hi_ref, lo_ref):
    hi, lo = _split_bf16(w_ref[...])
    hi_ref[...] = hi
    lo_ref[...] = lo


def _split_weight(w):
    rows, cols = w.shape
    spec = pl.BlockSpec((256, cols), lambda i: (i, 0))
    return pl.pallas_call(
        _split_kernel,
        grid=(rows // 256,),
        in_specs=[spec],
        out_specs=[spec, spec],
        out_shape=[jax.ShapeDtypeStruct((rows, cols), BF16)] * 2,
        compiler_params=pltpu.CompilerParams(dimension_semantics=("parallel",)),
        name="split_weight",
    )(w)


INPROJ_TM = 1024


def _inproj_kernel(x_ref, g_ref, *refs, mode):
    nw = _wcount(mode)
    wm_refs = refs[:nw]
    wsh_ref, wsl_ref, om_ref, os_ref, uh_ref, ul_ref = refs[nw:]

    @pl.when(pl.program_id(1) == 0)
    def _():
        u = _rms(x_ref[...], g_ref[...])
        uh, ul = _split_bf16(u)
        uh_ref[...] = uh
        ul_ref[...] = ul
        os_ref[...] = _dot3(uh, ul, wsh_ref[...], wsl_ref[...])

    w_hi, w_lo = _wload(wm_refs, mode)
    if w_lo is None:
        om_ref[...] = _dot(uh_ref[...], w_hi)
    else:
        om_ref[...] = _dot3(uh_ref[...], ul_ref[...], w_hi, w_lo)


def _inproj(x, g, wm, wsh, wsl, tm):
    n = x.shape[0]
    tn = 1024
    return pl.pallas_call(
        functools.partial(_inproj_kernel, mode=_wmode(wm)),
        grid=(n // tm, N_MAIN // tn),
        in_specs=[
            pl.BlockSpec((tm, D_MODEL), lambda i, j: (i, 0)),
            pl.BlockSpec((1, D_MODEL), lambda i, j: (0, 0)),
        ] + [pl.BlockSpec((D_MODEL, tn), lambda i, j: (0, j))] * len(wm) + [
            pl.BlockSpec((D_MODEL, LANES), lambda i, j: (0, 0)),
            pl.BlockSpec((D_MODEL, LANES), lambda i, j: (0, 0)),
        ],
        out_specs=[
            pl.BlockSpec((tm, tn), lambda i, j: (i, j)),
            pl.BlockSpec((tm, LANES), lambda i, j: (i, 0)),
        ],
        out_shape=[
            jax.ShapeDtypeStruct((n, N_MAIN), F32),
            jax.ShapeDtypeStruct((n, LANES), F32),
        ],
        scratch_shapes=[pltpu.VMEM((tm, D_MODEL), BF16), pltpu.VMEM((tm, D_MODEL), BF16)],
        compiler_params=pltpu.CompilerParams(
            dimension_semantics=("parallel", "arbitrary"), vmem_limit_bytes=VMEM_LIMIT),
        name="inproj",
    )(x, g, *wm, wsh, wsl)


def _route(logits):
    lane = lax.broadcasted_iota(jnp.int32, logits.shape, 1)
    gmask = (lane >= N_EXPERTS) & (lane < N_EXPERTS + N_EXPERT_GROUPS)
    gl = jnp.where(gmask, logits, -jnp.inf)
    gmax = jnp.max(gl, axis=-1, keepdims=True)
    gidx = jnp.min(jnp.where(gl == gmax, lane, 4 * LANES), axis=-1, keepdims=True) - N_EXPERTS
    gw = 1.0 / jnp.sum(jnp.where(gmask, jnp.exp(gl - gmax), 0.0), axis=-1, keepdims=True)
    lo = gidx * EXPERTS_PER_GROUP
    emask = (lane >= lo) & (lane < lo + EXPERTS_PER_GROUP)
    el = jnp.where(emask, logits, -jnp.inf)
    m1 = jnp.max(el, axis=-1, keepdims=True)
    i1 = jnp.min(jnp.where(el == m1, lane, 4 * LANES), axis=-1, keepdims=True)
    el2 = jnp.where(lane == i1, -jnp.inf, el)
    m2 = jnp.max(el2, axis=-1, keepdims=True)
    i2 = jnp.min(jnp.where(el2 == m2, lane, 4 * LANES), axis=-1, keepdims=True)
    e2 = jnp.exp(m2 - m1)
    den = 1.0 + e2
    g1 = gw / den
    g2 = gw * e2 / den
    return jnp.where(lane == i1, g1, jnp.where(lane == i2, g2, 0.0))


def _outproj_kernel(h_ref, mix_ref, *refs, mode):
    nw = _wcount(mode)
    g_ref, wrh_ref, wrl_ref, br_ref, h2_ref, u2_ref, cw_ref = refs[nw:]
    h2 = h_ref[...] + _wdot(mix_ref[...], refs[:nw], mode)
    h2_ref[...] = h2
    u = _rms(h2, g_ref[...])
    u2_ref[...] = u
    uh, ul = _split_bf16(u)
    logits = _dot3(uh, ul, wrh_ref[...], wrl_ref[...]) + br_ref[...]
    cw_ref[...] = _route(logits)


def _outproj(h, mix, wo, g, wrh, wrl, br, tm):
    n = h.shape[0]
    row = lambda i: (i, 0)
    fixed = lambda i: (0, 0)
    return pl.pallas_call(
        functools.partial(_outproj_kernel, mode=_wmode(wo)),
        grid=(n // tm,),
        in_specs=[
            pl.BlockSpec((tm, D_MODEL), row),
            pl.BlockSpec((tm, D_MODEL), row),
        ] + [pl.BlockSpec((D_MODEL, D_MODEL), fixed)] * len(wo) + [
            pl.BlockSpec((1, D_MODEL), fixed),
            pl.BlockSpec((D_MODEL, LANES), fixed),
            pl.BlockSpec((D_MODEL, LANES), fixed),
            pl.BlockSpec((1, LANES), fixed),
        ],
        out_specs=[
            pl.BlockSpec((tm, D_MODEL), row),
            pl.BlockSpec((tm, D_MODEL), row),
            pl.BlockSpec((tm, LANES), row),
        ],
        out_shape=[
            jax.ShapeDtypeStruct((n, D_MODEL), F32),
            jax.ShapeDtypeStruct((n, D_MODEL), F32),
            jax.ShapeDtypeStruct((n, LANES), F32),
        ],
        compiler_params=pltpu.CompilerParams(
            dimension_semantics=("parallel",), vmem_limit_bytes=VMEM_LIMIT),
        name="outproj",
    )(h, mix, *wo, g, wrh, wrl, br)


MOE_TM = 1024


def _moe_kernel(x_ref, cw_ref, w1_ref, w3_ref, w2_ref, y_ref, xh_ref, xl_ref, *, mode):
    e = pl.program_id(1)

    @pl.when(e == 0)
    def _():
        y_ref[...] = jnp.zeros_like(y_ref)
        xh, xl = _split_bf16(x_ref[...])
        xh_ref[...] = xh
        xl_ref[...] = xl

    cw = cw_ref[...]
    lane = lax.broadcasted_iota(jnp.int32, cw.shape, 1)
    col = jnp.sum(jnp.where(lane == e, cw, 0.0), axis=-1, keepdims=True)
    xh = xh_ref[...]
    w1_hi, w1_lo = _wload((w1_ref,), mode, 0)
    w3_hi, w3_lo = _wload((w3_ref,), mode, 0)
    if w1_lo is None:
        a = _dot(xh, w1_hi)
        b = _dot(xh, w3_hi)
    else:
        xl = xl_ref[...]
        a = _dot3(xh, xl, w1_hi, w1_lo)
        b = _dot3(xh, xl, w3_hi, w3_lo)
    hid = (a * jax.nn.sigmoid(a)) * b
    y_ref[...] += col * _wdot(hid, (w2_ref,), mode, 0)


def _moe(u2, cw, w1, w3, w2, tm):
    n = u2.shape[0]
    mode = 'f32' if w1.dtype == F32 else 'x1'
    return pl.pallas_call(
        functools.partial(_moe_kernel, mode=mode),
        grid=(n // tm, N_EXPERTS),
        in_specs=[
            pl.BlockSpec((tm, D_MODEL), lambda i, e: (i, 0)),
            pl.BlockSpec((tm, LANES), lambda i, e: (i, 0)),
            pl.BlockSpec((1, D_MODEL, D_EXPERT), lambda i, e: (e, 0, 0)),
            pl.BlockSpec((1, D_MODEL, D_EXPERT), lambda i, e: (e, 0, 0)),
            pl.BlockSpec((1, D_EXPERT, D_MODEL), lambda i, e: (e, 0, 0)),
        ],
        out_specs=pl.BlockSpec((tm, D_MODEL), lambda i, e: (i, 0)),
        out_shape=jax.ShapeDtypeStruct((n, D_MODEL), F32),
        scratch_shapes=[pltpu.VMEM((tm, D_MODEL), BF16), pltpu.VMEM((tm, D_MODEL), BF16)],
        compiler_params=pltpu.CompilerParams(
            dimension_semantics=("parallel", "arbitrary"), vmem_limit_bytes=VMEM_LIMIT),
        name="moe",
    )(u2, cw, w1, w3, w2)


def _ple_kernel(h_ref, y_ref, p_ref, g_ref, wg_ref, wp_ref, gf_ref, o_ref, *, final, mode):
    h3 = h_ref[...] + y_ref[...]
    u = _rms(h3, g_ref[...])
    gate = jax.nn.sigmoid(_wdot(u, (wg_ref,), mode))
    h4 = h3 + gate * _wdot(p_ref[...], (wp_ref,), mode)
    if final:
        h4 = _rms(h4, gf_ref[...])
    o_ref[...] = h4


def _ple(h2, y, p, g, wg, wp, gf, tm, final):
    n = h2.shape[0]
    row = lambda i: (i, 0)
    fixed = lambda i: (0, 0)
    mode = 'f32' if wg.dtype == F32 else 'x1'
    return pl.pallas_call(
        functools.partial(_ple_kernel, final=final, mode=mode),
        grid=(n // tm,),
        in_specs=[
            pl.BlockSpec((tm, D_MODEL), row),
            pl.BlockSpec((tm, D_MODEL), row),
            pl.BlockSpec((tm, PLE_DIM), row),
            pl.BlockSpec((1, D_MODEL), fixed),
            pl.BlockSpec((D_MODEL, D_MODEL), fixed),
            pl.BlockSpec((PLE_DIM, D_MODEL), fixed),
            pl.BlockSpec((1, D_MODEL), fixed),
        ],
        out_specs=pl.BlockSpec((tm, D_MODEL), row),
        out_shape=jax.ShapeDtypeStruct((n, D_MODEL), F32),
        compiler_params=pltpu.CompilerParams(
            dimension_semantics=("parallel",), vmem_limit_bytes=VMEM_LIMIT),
        name="ple",
    )(h2, y, p, g, wg, wp, gf)


FOX_LF_LANE = 8
FOX_TB = 256
FOX_TQ = 256


def _split3_bf16(x):
    hi = x.astype(BF16)
    r = x - hi.astype(F32)
    mid = r.astype(BF16)
    lo = (r - mid.astype(F32)).astype(BF16)
    return hi, mid, lo


def _log_sigmoid(x):
    return jnp.minimum(x, 0.0) - jnp.log1p(jnp.exp(-jnp.abs(x)))


def _group_mean(x2, ones_blk, width):
    hi, mid, lo = _split3_bf16(x2)
    return (_dot(hi, ones_blk) + _dot(mid, ones_blk) + _dot(lo, ones_blk)) * (1.0 / width)


def _fox_layout(hp):
    ka = 256 if hp else 128
    a0 = 192 if hp else 64
    nx = 512 + 3 * LANES
    mq = np.zeros((nx, 4 * ka), np.float32)
    mk = np.zeros((nx, 4 * ka), np.float32)
    rq = np.zeros((1, 4 * ka), np.float32)
    rk = np.zeros((1, 4 * ka), np.float32)
    mv = np.zeros((512, 4 * LANES), np.float32)
    for h in range(N_HEADS):
        for d in range(HEAD_DIM):
            src_hi, src_lo = 64 * h + d, 256 + 64 * h + d
            mq[src_hi, h * ka + d] = 1.0
            mk[src_hi, h * ka + d] = 1.0
            if hp:
                mq[src_lo, h * ka + 64 + d] = 1.0
                mq[src_hi, h * ka + 128 + d] = 1.0
                mk[src_hi, h * ka + 64 + d] = 1.0
                mk[src_lo, h * ka + 128 + d] = 1.0
            mv[src_hi, h * LANES + d] = 1.0
            if hp:
                mv[src_lo, h * LANES + 64 + d] = 1.0
        for part in range(3):
            src = 512 + part * LANES + FOX_LF_LANE + h
            mq[src, h * ka + a0 + part] = 1.0
            mk[src, h * ka + a0 + 3 + part] = -1.0
            rq[0, h * ka + a0 + 3 + part] = 1.0
            rk[0, h * ka + a0 + part] = 1.0
    blk = (np.arange(256)[:, None] // HEAD_DIM == np.arange(256)[None, :] // HEAD_DIM).astype(np.float32)
    tri = np.tril(np.ones((FOX_TB, FOX_TB), np.float32))
    as_bf = lambda a: jnp.asarray(a, BF16)
    return dict(ka=ka, mq=as_bf(mq), mk=as_bf(mk), mv=as_bf(mv), rq=jnp.asarray(rq), rk=jnp.asarray(rk),
                blk=as_bf(blk), tri=as_bf(tri))


def _fox_prep_kernel(pc_ref, ps_ref, bf_ref, qn_ref, kn_ref, blk_ref, tri_ref, mq_ref, mk_ref, mv_ref,
                     rq_ref, rk_ref, krow_ref, lf_ref, qa_ref, ka_ref, vv_ref, cb_ref, carry_ref, *, ka):
    @pl.when(pl.program_id(1) == 0)
    def _():
        carry_ref[...] = jnp.zeros_like(carry_ref)

    blk = blk_ref[...]
    q = pc_ref[:, 0:256]
    k = pc_ref[:, 256:512]
    v = pc_ref[:, 512:768]
    qn = q * lax.rsqrt(_group_mean(q * q, blk, HEAD_DIM) + EPS) * qn_ref[...]
    kn = k * lax.rsqrt(_group_mean(k * k, blk, HEAD_DIM) + EPS) * kn_ref[...]
    krow_ref[...] = kn
    lf = _log_sigmoid(ps_ref[...] + bf_ref[...])
    lf_ref[...] = lf
    l_hi, l_mid, l_lo = _split3_bf16(lf)
    tri = tri_ref[...]
    c = _dot(tri, l_hi) + _dot(tri, l_mid) + _dot(tri, l_lo) + carry_ref[...]
    carry_ref[...] = c[FOX_TB - 1:FOX_TB, :]
    r8 = lax.broadcasted_iota(jnp.int32, (8, LANES), 0)
    cb_ref[...] = jnp.where(r8 == 0, c[0:1, :], jnp.where(r8 == 1, c[FOX_TB - 1:FOX_TB, :], 0.0))
    c_hi, c_mid, c_lo = _split3_bf16(c)
    q_hi, q_lo = _split_bf16(qn * (HEAD_DIM ** -0.5))
    k_hi, k_lo = _split_bf16(kn)
    xq = jnp.concatenate([q_hi, q_lo, c_hi, c_mid, c_lo], axis=-1)
    xk = jnp.concatenate([k_hi, k_lo, c_hi, c_mid, c_lo], axis=-1)
    qa = (_dot(xq, mq_ref[...]) + rq_ref[...]).astype(BF16)
    kk = (_dot(xk, mk_ref[...]) + rk_ref[...]).astype(BF16)
    v_hi, v_lo = _split_bf16(v)
    vv = _dot(jnp.concatenate([v_hi, v_lo], axis=-1), mv_ref[...]).astype(BF16)
    for h in range(N_HEADS):
        qa_ref[0, h] = qa[:, h * ka:(h + 1) * ka]
        ka_ref[0, h] = kk[:, h * ka:(h + 1) * ka]
        vv_ref[0, h] = vv[:, h * LANES:(h + 1) * LANES]


def _fox_prep(pm, ps, bf_row, qn_row, kn_row, lay, bsz, t):
    ka = lay['ka']
    nt = t // FOX_TB
    fixed = lambda b, i: (0, 0)
    rows = lambda b, i: (b * nt + i, 0)
    hm = lambda b, i: (b, 0, i, 0)
    return pl.pallas_call(
        functools.partial(_fox_prep_kernel, ka=ka),
        grid=(bsz, nt),
        in_specs=[
            pl.BlockSpec((FOX_TB, 1024), lambda b, i: (b * nt + i, 2)),
            pl.BlockSpec((FOX_TB, LANES), rows),
            pl.BlockSpec((1, LANES), fixed),
            pl.BlockSpec((1, 256), fixed),
            pl.BlockSpec((1, 256), fixed),
            pl.BlockSpec((256, 256), fixed),
            pl.BlockSpec((FOX_TB, FOX_TB), fixed),
            pl.BlockSpec(lay['mq'].shape, fixed),
            pl.BlockSpec(lay['mk'].shape, fixed),
            pl.BlockSpec(lay['mv'].shape, fixed),
            pl.BlockSpec((1, 4 * ka), fixed),
            pl.BlockSpec((1, 4 * ka), fixed),
        ],
        out_specs=[
            pl.BlockSpec((FOX_TB, 256), rows),
            pl.BlockSpec((FOX_TB, LANES), rows),
            pl.BlockSpec((1, N_HEADS, FOX_TB, ka), hm),
            pl.BlockSpec((1, N_HEADS, FOX_TB, ka), hm),
            pl.BlockSpec((1, N_HEADS, FOX_TB, LANES), hm),
            pl.BlockSpec((8, LANES), rows),
        ],
        out_shape=[
            jax.ShapeDtypeStruct((bsz * t, 256), F32),
            jax.ShapeDtypeStruct((bsz * t, LANES), F32),
            jax.ShapeDtypeStruct((bsz, N_HEADS, t, ka), BF16),
            jax.ShapeDtypeStruct((bsz, N_HEADS, t, ka), BF16),
            jax.ShapeDtypeStruct((bsz, N_HEADS, t, LANES), BF16),
            jax.ShapeDtypeStruct((bsz * nt * 8, LANES), F32),
        ],
        scratch_shapes=[pltpu.VMEM((1, LANES), F32)],
        compiler_params=pltpu.CompilerParams(
            dimension_semantics=("parallel", "arbitrary"), vmem_limit_bytes=VMEM_LIMIT),
        name="fox_prep",
    )(pm, ps, bf_row, qn_row, kn_row, lay['blk'], lay['tri'], lay['mq'], lay['mk'], lay['mv'],
      lay['rq'], lay['rk'])


def _fox_flash_kernel(j0_ref, qa_ref, ka_ref, vv_ref, g_ref, on_ref, o_ref, m_s, l_s, acc_s, *, hp):
    i = pl.program_id(1)
    j_first = j0_ref[pl.program_id(0) * pl.num_programs(1) + i]
    tq = FOX_TQ
    row = lax.broadcasted_iota(jnp.int32, (tq, tq), 0)
    col = lax.broadcasted_iota(jnp.int32, (tq, tq), 1)
    m_s[...] = jnp.full(m_s.shape, NEG_BIG, F32)
    l_s[...] = jnp.zeros(l_s.shape, F32)
    acc_s[...] = jnp.zeros(acc_s.shape, F32)

    def tile(j, masked):
        start = pl.multiple_of(j * tq, tq)
        for h in range(N_HEADS):
            kt = ka_ref[0, h, pl.ds(start, tq), :]
            s = lax.dot_general(qa_ref[0, h], kt, (((1,), (1,)), ((), ())), preferred_element_type=F32)
            if masked:
                s = jnp.where(row >= col, s, NEG_BIG)
            m = m_s[h]
            m_new = jnp.maximum(m, jnp.max(s, axis=-1, keepdims=True))
            alpha = jnp.exp(m - m_new)
            p = jnp.exp(s - m_new)
            l_s[h] = alpha * l_s[h] + jnp.sum(p, axis=-1, keepdims=True)
            m_s[h] = m_new
            vt = vv_ref[0, h, pl.ds(start, tq), :]
            if hp:
                p_hi, p_lo = _split_bf16(p)
                pv = _dot(jnp.concatenate([p_hi, p_lo], axis=-1), jnp.concatenate([vt, vt], axis=0))
            else:
                pv = _dot(p.astype(BF16), vt)
            acc_s[h] = alpha * acc_s[h] + pv

    def body(jj, carry):
        tile(j_first + jj, False)
        return carry

    lax.fori_loop(0, i - j_first, body, 0)
    tile(i, True)
    outs = []
    for h in range(N_HEADS):
        acc = acc_s[h]
        o = (acc[:, :HEAD_DIM] + acc[:, HEAD_DIM:]) / l_s[h]
        o = o * lax.rsqrt(jnp.mean(o * o, axis=-1, keepdims=True) + EPS)
        sl = slice(h * HEAD_DIM, (h + 1) * HEAD_DIM)
        outs.append(o * on_ref[:, sl] * jax.nn.sigmoid(g_ref[:, sl]))
    o_ref[...] = jnp.concatenate(outs, axis=-1)


def _fox_flash(j0, qa, ka, vv, pm, on_row, bsz, t, hp):
    kad = qa.shape[-1]
    nq = t // FOX_TQ
    whole = lambda b, i, j0: (b, 0, 0, 0)
    return pl.pallas_call(
        functools.partial(_fox_flash_kernel, hp=hp),
        grid_spec=pltpu.PrefetchScalarGridSpec(
            num_scalar_prefetch=1,
            grid=(bsz, nq),
            in_specs=[
                pl.BlockSpec((1, N_HEADS, FOX_TQ, kad), lambda b, i, j0: (b, 0, i, 0)),
                pl.BlockSpec((1, N_HEADS, t, kad), whole, pipeline_mode=pl.Buffered(1)),
                pl.BlockSpec((1, N_HEADS, t, LANES), whole, pipeline_mode=pl.Buffered(1)),
                pl.BlockSpec((FOX_TQ, 256), lambda b, i, j0: (b * nq + i, 11)),
                pl.BlockSpec((1, 256), lambda b, i, j0: (0, 0)),
            ],
            out_specs=pl.BlockSpec((FOX_TQ, 256), lambda b, i, j0: (b * nq + i, 0)),
            scratch_shapes=[pltpu.VMEM((N_HEADS, FOX_TQ, 1), F32), pltpu.VMEM((N_HEADS, FOX_TQ, 1), F32),
                            pltpu.VMEM((N_HEADS, FOX_TQ, LANES), F32)],
        ),
        out_shape=jax.ShapeDtypeStruct((bsz * t, 256), F32),
        compiler_params=pltpu.CompilerParams(
            dimension_semantics=("parallel", "arbitrary"), vmem_limit_bytes=56 * 1024 * 1024),
        name="fox_flash",
    )(j0.reshape(-1), qa, ka, vv, pm, on_row)


FOX_UNDERFLOW = -120.0


def _fox_first_tile(cb, q_norm, k_norm, bsz, nq):
    cb = cb.reshape(bsz, nq, 8, LANES)[:, :, :, FOX_LF_LANE:FOX_LF_LANE + N_HEADS]
    c_first, c_last = cb[:, :, 0], cb[:, :, 1]
    qk_bound = math.sqrt(HEAD_DIM) * jnp.max(jnp.abs(q_norm)) * jnp.max(jnp.abs(k_norm)) * 1.01
    gap = c_first[:, :, None, :] - c_last[:, None, :, :]
    needed = jnp.any(gap >= FOX_UNDERFLOW - 2.0 * qk_bound, axis=-1)
    tiles = jnp.arange(nq)
    needed = needed | (tiles[None, None, :] >= tiles[None, :, None])
    return jnp.sum(jnp.cumsum(needed.astype(jnp.int32), axis=-1) == 0, axis=-1).astype(jnp.int32)


def _fox_prompt(pm, ps, b_f, q_norm, k_norm, out_norm, bsz, t, hp):
    lay = _fox_layout(hp)
    bf_row = jnp.zeros((1, LANES), F32).at[0, FOX_LF_LANE:FOX_LF_LANE + N_HEADS].set(b_f)
    tile4 = lambda g: jnp.tile(g, N_HEADS).reshape(1, 256)
    krow, lf, qa, ka, vv, cb = _fox_prep(pm, ps, bf_row, tile4(q_norm), tile4(k_norm), lay, bsz, t)
    j0 = _fox_first_tile(cb, q_norm, k_norm, bsz, t // FOX_TQ)
    o = _fox_flash(j0, qa, ka, vv, pm, out_norm.reshape(1, 256), bsz, t, hp)
    return o, krow, lf[:, FOX_LF_LANE:FOX_LF_LANE + N_HEADS]


FOX_PAGES_PER_STEP = 32
_UST = np.tril(np.ones((PAGE_SIZE, PAGE_SIZE), np.float32), k=-1)


def _row_to_col(row, eye):
    return jnp.sum(jnp.where(eye, row, 0.0), axis=-1, keepdims=True)


def _col_to_row(col, eye):
    return jnp.sum(jnp.where(eye, col, 0.0), axis=0, keepdims=True)


def _per_head_rows(x8, rows_per_head, width):
    return jnp.concatenate([jnp.broadcast_to(x8[h:h + 1, :], (rows_per_head, width)) for h in range(N_HEADS)],
                           axis=0)


def _fox_dec_kernel(pt_ref, pc_ref, ps_ref, bf_ref, qn_ref, kn_ref, on_ref, blk_ref, ust_ref, *refs):
    npp = FOX_PAGES_PER_STEP
    k_refs, v_refs, lf_refs = refs[0:npp], refs[npp:2 * npp], refs[2 * npp:3 * npp]
    o_ref, krow_ref, lfrow_ref, qb_s, m_s, l_s, acc_s, carry_s = refs[3 * npp:]
    j = pl.program_id(1)
    hrow = lax.broadcasted_iota(jnp.int32, (8, 256), 0)
    lane = lax.broadcasted_iota(jnp.int32, (8, 256), 1)
    hm = (lane >= hrow * HEAD_DIM) & (lane < (hrow + 1) * HEAD_DIM)
    eye = lax.broadcasted_iota(jnp.int32, (256, 256), 0) == lax.broadcasted_iota(jnp.int32, (256, 256), 1)
    r8 = lax.broadcasted_iota(jnp.int32, (8, LANES), 0)
    l8 = lax.broadcasted_iota(jnp.int32, (8, LANES), 1)
    blk = blk_ref[...]

    @pl.when(j == 0)
    def _():
        q = pc_ref[:, 0:256]
        k = pc_ref[:, 256:512]
        v = pc_ref[:, 512:768]
        qn = q * lax.rsqrt(_group_mean(q * q, blk, HEAD_DIM) + EPS) * qn_ref[...]
        kn = k * lax.rsqrt(_group_mean(k * k, blk, HEAD_DIM) + EPS) * kn_ref[...]
        krow_ref[...] = kn
        lf = _log_sigmoid(ps_ref[...] + bf_ref[...])
        lfrow_ref[...] = lf
        q_row = qn[0:1, :] * (HEAD_DIM ** -0.5)
        qb_s[...] = jnp.broadcast_to(_row_to_col(q_row, eye), (256, LANES))
        m_s[...] = jnp.sum(jnp.where(hm, q_row * kn[0:1, :], 0.0), axis=-1, keepdims=True)
        l_s[...] = jnp.ones_like(l_s)
        lane_full = lax.broadcasted_iota(jnp.int32, (256, LANES), 1)
        acc_s[...] = jnp.where(lane_full == 0, _row_to_col(v[0:1, :], eye), 0.0)
        carry_s[...] = jnp.sum(jnp.where(l8 == r8 + FOX_LF_LANE, lf[0:1, :], 0.0), axis=-1, keepdims=True)

    qb = qb_s[...]
    m, l, carry = m_s[...], l_s[...], carry_s[...]
    lf_all = jnp.concatenate([lf_refs[r][...] for r in range(npp)], axis=0)
    suffix = _mm_exact_r(lf_all, ust_ref[...])
    totals = jnp.sum(lf_all, axis=-1, keepdims=True)
    scores = []
    for r in reversed(range(npp)):
        prod = k_refs[r][...] * qb
        s = jnp.zeros((8, LANES), F32)
        for h in range(N_HEADS):
            s_h = jnp.sum(prod[h * HEAD_DIM:(h + 1) * HEAD_DIM, :], axis=0, keepdims=True)
            s = s + jnp.where(r8 == h, s_h, 0.0)
        scores.append(s + suffix[8 * r:8 * r + 8, :] + carry)
        carry = carry + totals[8 * r:8 * r + 8, :]
    m_new = m
    for s in scores:
        m_new = jnp.maximum(m_new, jnp.max(s, axis=-1, keepdims=True))
    alpha = jnp.exp(m - m_new)
    l = alpha * l
    acc = acc_s[...] * _per_head_rows(alpha, HEAD_DIM, 1)
    for idx, r in enumerate(reversed(range(npp))):
        p = jnp.exp(scores[idx] - m_new)
        l = l + jnp.sum(p, axis=-1, keepdims=True)
        acc = acc + _per_head_rows(p, HEAD_DIM, LANES) * v_refs[r][...]
    m_s[...], l_s[...], acc_s[...], carry_s[...] = m_new, l, acc, carry

    @pl.when(j == pl.num_programs(1) - 1)
    def _():
        o = _col_to_row(jnp.sum(acc, axis=-1, keepdims=True), eye)
        l_row = jnp.sum(jnp.where(hm, l, 0.0), axis=0, keepdims=True)
        o = jnp.broadcast_to(o / l_row, (8, 256))
        o = o * lax.rsqrt(_group_mean(o * o, blk, HEAD_DIM) + EPS)
        o_ref[...] = o * on_ref[...] * jax.nn.sigmoid(pc_ref[:, 768:1024])


def _fox_sample(pmr, psr, cache_k, cache_v, cache_logf, page_table, l, b_f, q_norm, k_norm, out_norm):
    bsz, n_pages = page_table.shape
    npp = FOX_PAGES_PER_STEP
    nsteps = n_pages // npp
    depth, n_phys = cache_k.shape[0], cache_k.shape[1]
    ck = jnp.transpose(cache_k, (0, 1, 3, 4, 2)).reshape(depth, n_phys, 256, PAGE_SIZE)
    cv = jnp.transpose(cache_v, (0, 1, 3, 4, 2)).reshape(depth, n_phys, 256, PAGE_SIZE)
    clf = jnp.pad(jnp.swapaxes(cache_logf, 2, 3), ((0, 0), (0, 0), (0, 8 - N_HEADS), (0, 0)))
    bf_row = jnp.zeros((1, LANES), F32).at[0, FOX_LF_LANE:FOX_LF_LANE + N_HEADS].set(b_f)
    tile4 = lambda g: jnp.tile(g, N_HEADS).reshape(1, 256)
    fixed = lambda b, j, pt: (0, 0)

    def page_spec(r, width):
        return pl.BlockSpec((None, None, width[0], width[1]),
                            lambda b, j, pt, r=r: (l, pt[b, (nsteps - 1 - j) * npp + r], 0, 0))

    in_specs = [
        pl.BlockSpec((8, 1024), lambda b, j, pt: (b, 2)),
        pl.BlockSpec((8, LANES), lambda b, j, pt: (b, 0)),
        pl.BlockSpec((1, LANES), fixed),
        pl.BlockSpec((1, 256), fixed),
        pl.BlockSpec((1, 256), fixed),
        pl.BlockSpec((1, 256), fixed),
        pl.BlockSpec((256, 256), fixed),
        pl.BlockSpec((PAGE_SIZE, PAGE_SIZE), fixed),
    ]
    in_specs += [page_spec(r, (256, PAGE_SIZE)) for r in range(npp)]
    in_specs += [page_spec(r, (256, PAGE_SIZE)) for r in range(npp)]
    in_specs += [page_spec(r, (8, PAGE_SIZE)) for r in range(npp)]
    rows = lambda b, j, pt: (b, 0)
    o, krow, lfrow = pl.pallas_call(
        _fox_dec_kernel,
        grid_spec=pltpu.PrefetchScalarGridSpec(
            num_scalar_prefetch=1,
            grid=(bsz, nsteps),
            in_specs=in_specs,
            out_specs=[pl.BlockSpec((8, 256), rows), pl.BlockSpec((8, 256), rows),
                       pl.BlockSpec((8, LANES), rows)],
            scratch_shapes=[pltpu.VMEM((256, LANES), F32), pltpu.VMEM((8, 1), F32), pltpu.VMEM((8, 1), F32),
                            pltpu.VMEM((256, LANES), F32), pltpu.VMEM((8, 1), F32)],
        ),
        out_shape=[jax.ShapeDtypeStruct((bsz * 8, 256), F32), jax.ShapeDtypeStruct((bsz * 8, 256), F32),
                   jax.ShapeDtypeStruct((bsz * 8, LANES), F32)],
        compiler_params=pltpu.CompilerParams(
            dimension_semantics=("parallel", "arbitrary"), vmem_limit_bytes=VMEM_LIMIT),
        name="fox_sample",
    )(page_table, pmr, psr, bf_row, tile4(q_norm), tile4(k_norm), out_norm.reshape(1, 256),
      jnp.asarray(_BLK256, BF16), jnp.asarray(_UST, BF16), *([ck] * npp), *([cv] * npp), *([clf] * npp))
    first = lambda a: a.reshape(bsz, 8, -1)[:, 0]
    return first(o), first(krow), first(lfrow)[:, FOX_LF_LANE:FOX_LF_LANE + N_HEADS]


CHUNK = 128
ROW0 = 8


def _mm(a, b, hp):
    if hp:
        a_hi, a_lo = _split_bf16(a)
        b_hi, b_lo = _split_bf16(b)
        return _dot3(a_hi, a_lo, b_hi, b_lo)
    return _dot(a.astype(BF16), b.astype(BF16))


def _mm_nt(a, b, hp):
    dn = (((1,), (1,)), ((), ()))
    if hp:
        a_hi, a_lo = _split_bf16(a)
        b_hi, b_lo = _split_bf16(b)
        a3 = jnp.concatenate([a_hi, a_lo, a_hi], axis=-1)
        b3 = jnp.concatenate([b_hi, b_hi, b_lo], axis=-1)
        return lax.dot_general(a3, b3, dn, preferred_element_type=F32)
    return lax.dot_general(a.astype(BF16), b.astype(BF16), dn, preferred_element_type=F32)


def _mm_exact(sel, x):
    hi, mid, lo = _split3_bf16(x)
    return _dot(sel, hi) + _dot(sel, mid) + _dot(sel, lo)


def _mm_exact_r(x, sel):
    hi, mid, lo = _split3_bf16(x)
    return _dot(hi, sel) + _dot(mid, sel) + _dot(lo, sel)


def _softplus(x):
    return jnp.maximum(x, 0.0) + jnp.log1p(jnp.exp(-jnp.abs(x)))


def _silu(x):
    return x * jax.nn.sigmoid(x)


def _stage_rows(buf, blk_ref, lo, hi, prev_ref, rows_in, first):
    @pl.when(first)
    def _():
        buf[ROW0 - 3:ROW0, :] = prev_ref[0]
        if rows_in < CHUNK:
            buf[ROW0 + rows_in:ROW0 + CHUNK, :] = jnp.zeros((CHUNK - rows_in, hi - lo), F32)

    @pl.when(jnp.logical_not(first))
    def _():
        buf[ROW0 - 3:ROW0, :] = buf[ROW0 + CHUNK - 3:ROW0 + CHUNK, :]

    buf[ROW0:ROW0 + rows_in, :] = blk_ref[:, lo:hi]


def _conv4(buf, w_ref):
    acc = w_ref[0:1, :] * buf[ROW0 - 3:ROW0 - 3 + CHUNK, :]
    for j in range(1, CONV_WIDTH):
        acc = acc + w_ref[j:j + 1, :] * buf[ROW0 - 3 + j:ROW0 - 3 + j + CHUNK, :]
    return acc


def _pad_rows(x, rows_in):
    if rows_in == CHUNK:
        return x
    return jnp.concatenate([x, jnp.zeros((CHUNK - rows_in, x.shape[1]), x.dtype)], axis=0)


def _head_expand(first_lane, width):
    e = np.zeros((LANES, N_HEADS * width), np.float32)
    for h in range(N_HEADS):
        e[first_lane + h, h * width:(h + 1) * width] = 1.0
    return jnp.asarray(e, BF16)


_TRI = np.tril(np.ones((CHUNK, CHUNK), np.float32))


SSD_DT_LANE = 12


def _ssd_kernel(pd_ref, ps_ref, cprev_ref, s0_ref, w_ref, cb_ref, dtb_ref, alog_ref, dsk_ref, gn_ref,
                e4_ref, tri_ref, y_ref, sfin_ref, buf, s_scr, *, rows_in, t_valid, hp):
    i = pl.program_id(1)
    first = i == 0

    @pl.when(first)
    def _():
        s_scr[...] = s0_ref[0]

    _stage_rows(buf, pd_ref, 256, 1024, cprev_ref, rows_in, first)
    xbc = _silu(_conv4(buf, w_ref) + cb_ref[...])
    xs = xbc[:, 0:256]
    bm = xbc[:, 256:512]
    cm = xbc[:, 512:768]
    z = _pad_rows(pd_ref[:, 0:256], rows_in)
    pre = _mm_exact_r(_pad_rows(ps_ref[...], rows_in), e4_ref[...])
    dt = _softplus(pre + dtb_ref[...])
    tpos = i * CHUNK + lax.broadcasted_iota(jnp.int32, (CHUNK, 1), 0)
    dt = jnp.where(tpos < t_valid, dt, 0.0)
    la = -jnp.exp(alog_ref[...]) * dt
    b = _mm_exact(tri_ref[...], la)
    row = lax.broadcasted_iota(jnp.int32, (CHUNK, CHUNK), 0)
    col = lax.broadcasted_iota(jnp.int32, (CHUNK, CHUNK), 1)
    causal = row >= col
    lane = lax.broadcasted_iota(jnp.int32, (1, 256), 1)
    s_prev = s_scr[...]
    cb = [_mm_nt(cm[:, g * 128:(g + 1) * 128], bm[:, g * 128:(g + 1) * 128], hp) for g in range(D_GROUPS)]
    y = jnp.zeros((CHUNK, 256), F32)
    s_new = jnp.zeros((D_STATE, 256), F32)
    for h in range(N_HEADS):
        g = h // (N_HEADS // D_GROUPS)
        bh = b[:, h * 128:(h + 1) * 128]
        dth = dt[:, h * 128:(h + 1) * 128]
        hmask = (lane >= h * HEAD_DIM) & (lane < (h + 1) * HEAD_DIM)
        dt2 = jnp.concatenate([dth, dth], axis=-1)
        xdt = jnp.where(hmask, xs * dt2, 0.0)
        rel = jnp.where(causal, jnp.exp(jnp.where(causal, bh - bh.T, 0.0)), 0.0)
        y = y + _mm(cb[g] * rel, xdt, hp)
        eb = jnp.exp(bh)
        y = y + _mm(cm[:, g * 128:(g + 1) * 128] * eb, jnp.where(hmask, s_prev, 0.0), hp)
        b_last = bh[CHUNK - 1:CHUNK, :]
        kdec = bm[:, g * 128:(g + 1) * 128] * jnp.exp(b_last - bh)
        s_new = s_new + _mm(kdec.T, xdt, hp)
        a2 = jnp.exp(jnp.concatenate([b_last, b_last], axis=-1))
        s_new = s_new + jnp.where(hmask, a2 * s_prev, 0.0)
    s_scr[...] = s_new
    sfin_ref[0] = s_new
    y = (y + xs * dsk_ref[...]) * _silu(z)
    outs = []
    for g in range(D_GROUPS):
        yg = y[:, g * 128:(g + 1) * 128]
        outs.append(yg * lax.rsqrt(jnp.mean(yg * yg, axis=-1, keepdims=True) + EPS))
    yn = jnp.concatenate(outs, axis=-1) * gn_ref[...]
    y_ref[...] = yn[0:rows_in, :]


def _ssd(pm, ps, conv_prev, s0, conv_w, conv_b, a_log, dt_bias, d_skip, norm_g, bsz, t_valid, rows_in, hp):
    nt = pm.shape[0] // (bsz * rows_in)
    fixed = lambda b, i: (0, 0)
    rows = lambda b, i: (b * nt + i, 0)
    rep128 = lambda v: jnp.repeat(v, 128).reshape(1, 512)
    s0l = jnp.transpose(s0, (0, 2, 1, 3)).reshape(bsz, D_STATE, 256)
    y, sfin = pl.pallas_call(
        functools.partial(_ssd_kernel, rows_in=rows_in, t_valid=t_valid, hp=hp),
        grid=(bsz, nt),
        in_specs=[
            pl.BlockSpec((rows_in, 1024), lambda b, i: (b * nt + i, 3)),
            pl.BlockSpec((rows_in, LANES), rows),
            pl.BlockSpec((1, 3, D_CONV_CH), lambda b, i: (b, 0, 0)),
            pl.BlockSpec((1, D_STATE, 256), lambda b, i: (b, 0, 0)),
            pl.BlockSpec((CONV_WIDTH, D_CONV_CH), fixed),
            pl.BlockSpec((1, D_CONV_CH), fixed),
            pl.BlockSpec((1, 512), fixed),
            pl.BlockSpec((1, 512), fixed),
            pl.BlockSpec((1, 256), fixed),
            pl.BlockSpec((1, 256), fixed),
            pl.BlockSpec((LANES, 512), fixed),
            pl.BlockSpec((CHUNK, CHUNK), fixed),
        ],
        out_specs=[
            pl.BlockSpec((rows_in, 256), rows),
            pl.BlockSpec((1, D_STATE, 256), lambda b, i: (b, 0, 0)),
        ],
        out_shape=[
            jax.ShapeDtypeStruct((pm.shape[0], 256), F32),
            jax.ShapeDtypeStruct((bsz, D_STATE, 256), F32),
        ],
        scratch_shapes=[pltpu.VMEM((ROW0 + CHUNK, D_CONV_CH), F32), pltpu.VMEM((D_STATE, 256), F32)],
        compiler_params=pltpu.CompilerParams(
            dimension_semantics=("parallel", "arbitrary"), vmem_limit_bytes=VMEM_LIMIT),
        name="ssd",
    )(pm, ps, conv_prev, s0l, conv_w, conv_b.reshape(1, -1), rep128(dt_bias), rep128(a_log),
      jnp.repeat(d_skip, HEAD_DIM).reshape(1, 256), norm_g.reshape(1, 256),
      _head_expand(SSD_DT_LANE, 128), jnp.asarray(_TRI, BF16))
    return y, jnp.transpose(sfin.reshape(bsz, D_STATE, N_HEADS, HEAD_DIM), (0, 2, 1, 3))


SUB = CHUNK_A
_SUB_ID = np.arange(CHUNK) // SUB
_SAME_SUB = (_SUB_ID[:, None] == _SUB_ID[None, :]).astype(np.float32)
_BLK256 = (np.arange(256)[:, None] // HEAD_DIM == np.arange(256)[None, :] // HEAD_DIM).astype(np.float32)


def _mm_tn(a, b, hp):
    dn = (((0,), (0,)), ((), ()))
    if hp:
        a_hi, a_lo = _split_bf16(a)
        b_hi, b_lo = _split_bf16(b)
        a3 = jnp.concatenate([a_hi, a_lo, a_hi], axis=0)
        b3 = jnp.concatenate([b_hi, b_hi, b_lo], axis=0)
        return lax.dot_general(a3, b3, dn, preferred_element_type=F32)
    return lax.dot_general(a.astype(BF16), b.astype(BF16), dn, preferred_element_type=F32)


def _group_sum(x, ones_blk, hp):
    if hp:
        return _mm_exact_r(x, ones_blk)
    return _dot(x.astype(BF16), ones_blk)


def _hgrn_kernel(pa_ref, s0_ref, c1_ref, c2_ref, oml_ref, gn_ref, blk_ref, t16_ref, l16_ref,
                 o_ref, sfin_ref, kbuf, vbuf, lbuf, st_scr, *, rows_in, t_valid, hp):
    i = pl.program_id(1)

    @pl.when(i == 0)
    def _():
        st_scr[...] = s0_ref[0]
        zeros = jnp.zeros((SUB, 256), F32)
        kbuf[0:SUB, :] = zeros
        vbuf[0:SUB, :] = zeros
        lbuf[0:SUB, :] = zeros

    q = _pad_rows(pa_ref[:, 0:256], rows_in) * (HEAD_DIM ** -0.5)
    zf = _pad_rows(pa_ref[:, 256:512], rows_in)
    v = _pad_rows(pa_ref[:, 512:768], rows_in)
    g = _pad_rows(pa_ref[:, 768:1024], rows_in)
    la = c1_ref[...]
    lb = c2_ref[...] + _log_sigmoid(zf)
    lf = jnp.maximum(la, lb) + jnp.log1p(jnp.exp(-jnp.abs(la - lb)))
    k = oml_ref[...] * jax.nn.sigmoid(-zf)
    tpos = i * CHUNK + lax.broadcasted_iota(jnp.int32, (CHUNK, 1), 0)
    valid = tpos < t_valid
    lf = jnp.where(valid, lf, 0.0)
    k = jnp.where(valid, k, 0.0)
    v = jnp.where(valid, v, 0.0)
    kbuf[SUB:SUB + CHUNK, :] = k
    vbuf[SUB:SUB + CHUNK, :] = v
    lbuf[SUB:SUB + CHUNK, :] = lf

    blk = blk_ref[...]
    sub = lax.broadcasted_iota(jnp.int32, (CHUNK, 1), 0) % SUB
    o = _group_sum(q * k, blk, hp) * v
    bd = jnp.zeros((CHUNK, 256), F32)
    for d in range(1, min(SUB, t_valid)):
        bd = bd + lbuf[SUB - d + 1:SUB - d + 1 + CHUNK, :]
        m = sub >= d
        ks = kbuf[SUB - d:SUB - d + CHUNK, :]
        tmp = jnp.where(m, q * ks * jnp.exp(jnp.where(m, bd, 0.0)), 0.0)
        o = o + _group_sum(tmp, blk, hp) * vbuf[SUB - d:SUB - d + CHUNK, :]

    b = _mm_exact(t16_ref[...], lf)
    bl = _mm_exact(l16_ref[...], lf)
    qe = q * jnp.exp(b)
    kd = k * jnp.exp(bl - b)
    st = st_scr[...]
    bdmask = blk > 0
    o_rows = []
    for n in range(CHUNK // SUB):
        r = slice(n * SUB, (n + 1) * SUB)
        o_rows.append(_mm_nt(qe[r], st, hp))
        ds = _mm_tn(v[r], kd[r], hp)
        st = st * jnp.exp(bl[n * SUB:n * SUB + 1, :]) + jnp.where(bdmask, ds, 0.0)
    st_scr[...] = st
    sfin_ref[0] = st
    o = o + jnp.concatenate(o_rows, axis=0)
    o = o * lax.rsqrt(_group_mean(o * o, blk, HEAD_DIM) + EPS) * gn_ref[...] * _silu(g)
    o_ref[...] = o[0:rows_in, :]


def _hgrn(pm, lb, norm_g, s0, bsz, t_valid, rows_in, hp):
    nt = pm.shape[0] // (bsz * rows_in)
    fixed = lambda b, i: (0, 0)
    rows = lambda b, i: (b * nt + i, 0)
    lb = jnp.clip(lb, 0.0, LB_CEIL)
    c1 = jnp.log(jnp.maximum(lb, LB_FLOOR)).reshape(1, 256)
    c2 = jnp.log1p(-lb).reshape(1, 256)
    oml = (1.0 - lb).reshape(1, 256)
    eye = jnp.eye(N_HEADS, dtype=bool)[None, :, None, :, None]
    st0 = jnp.where(eye, jnp.swapaxes(s0, 2, 3)[:, :, :, None, :], 0.0).reshape(bsz, 256, 256)
    o, sfin = pl.pallas_call(
        functools.partial(_hgrn_kernel, rows_in=rows_in, t_valid=t_valid, hp=hp),
        grid=(bsz, nt),
        in_specs=[
            pl.BlockSpec((rows_in, 1024), lambda b, i: (b * nt + i, 0)),
            pl.BlockSpec((1, 256, 256), lambda b, i: (b, 0, 0)),
            pl.BlockSpec((1, 256), fixed),
            pl.BlockSpec((1, 256), fixed),
            pl.BlockSpec((1, 256), fixed),
            pl.BlockSpec((1, 256), fixed),
            pl.BlockSpec((256, 256), fixed),
            pl.BlockSpec((CHUNK, CHUNK), fixed),
            pl.BlockSpec((CHUNK, CHUNK), fixed),
        ],
        out_specs=[
            pl.BlockSpec((rows_in, 256), rows),
            pl.BlockSpec((1, 256, 256), lambda b, i: (b, 0, 0)),
        ],
        out_shape=[
            jax.ShapeDtypeStruct((pm.shape[0], 256), F32),
            jax.ShapeDtypeStruct((bsz, 256, 256), F32),
        ],
        scratch_shapes=[pltpu.VMEM((SUB + CHUNK, 256), F32)] * 3 + [pltpu.VMEM((256, 256), F32)],
        compiler_params=pltpu.CompilerParams(
            dimension_semantics=("parallel", "arbitrary"), vmem_limit_bytes=VMEM_LIMIT),
        name="hgrn",
    )(pm, st0, c1, c2, oml, norm_g.reshape(1, 256), jnp.asarray(_BLK256, BF16),
      jnp.asarray(_TRI * _SAME_SUB, BF16), jnp.asarray(_SAME_SUB, BF16))
    sf = sfin.reshape(bsz, N_HEADS, HEAD_DIM, N_HEADS, HEAD_DIM)
    sf = jnp.stack([sf[:, h, :, h, :] for h in range(N_HEADS)], axis=1)
    return o, jnp.swapaxes(sf, 2, 3)


GDN_BETA_LANE = 0
GDN_DT_LANE = 4


def _block_diag_state(s0, bsz):
    eye = jnp.eye(N_HEADS, dtype=bool)[None, :, None, :, None]
    return jnp.where(eye, jnp.swapaxes(s0, 2, 3)[:, :, :, None, :], 0.0).reshape(bsz, 256, 256)


def _unblock_diag_state(st, bsz):
    sf = st.reshape(bsz, N_HEADS, HEAD_DIM, N_HEADS, HEAD_DIM)
    sf = jnp.stack([sf[:, h, :, h, :] for h in range(N_HEADS)], axis=1)
    return jnp.swapaxes(sf, 2, 3)


def _gdn_kernel(pb_ref, ps_ref, cprev_ref, s0_ref, w_ref, dtb_ref, alog_ref, dtb64_ref, alog64_ref, gn_ref,
                blk_ref, tri_ref, eb128_ref, ed128_ref, eb64_ref, ed64_ref,
                o_ref, sfin_ref, buf, st_scr, *, rows_in, t_valid, hp):
    i = pl.program_id(1)
    first = i == 0

    @pl.when(first)
    def _():
        st_scr[...] = s0_ref[0]

    _stage_rows(buf, pb_ref, 0, 768, cprev_ref, rows_in, first)
    qkv = _silu(_conv4(buf, w_ref))
    blk = blk_ref[...]
    q = qkv[:, 0:256]
    k = qkv[:, 256:512]
    v = qkv[:, 512:768]
    q = q * lax.rsqrt(_group_mean(q * q, blk, 1) + EPS) * (HEAD_DIM ** -0.5)
    k = k * lax.rsqrt(_group_mean(k * k, blk, 1) + EPS)
    gate = _pad_rows(pb_ref[:, 768:1024], rows_in)
    ps = _pad_rows(ps_ref[...], rows_in)
    tpos = i * CHUNK + lax.broadcasted_iota(jnp.int32, (CHUNK, 1), 0)
    valid = tpos < t_valid
    tri = tri_ref[...]
    beta128 = jnp.where(valid, jax.nn.sigmoid(_mm_exact_r(ps, eb128_ref[...])), 0.0)
    la128 = jnp.where(valid, -jnp.exp(alog_ref[...]) * _softplus(_mm_exact_r(ps, ed128_ref[...]) + dtb_ref[...]), 0.0)
    b128 = _mm_exact(tri, la128)
    beta64 = jnp.where(valid, jax.nn.sigmoid(_mm_exact_r(ps, eb64_ref[...])), 0.0)
    la64 = jnp.where(valid, -jnp.exp(alog64_ref[...]) * _softplus(_mm_exact_r(ps, ed64_ref[...]) + dtb64_ref[...]), 0.0)
    b64 = _mm_exact(tri, la64)
    eb64 = jnp.exp(b64)
    b_last64 = b64[CHUNK - 1:CHUNK, :]
    kb = k * beta64
    rv = v * beta64
    rk = kb * eb64
    q_dec = q * eb64
    k_dec = k * jnp.exp(b_last64 - b64)

    row = lax.broadcasted_iota(jnp.int32, (CHUNK, CHUNK), 0)
    col = lax.broadcasted_iota(jnp.int32, (CHUNK, CHUNK), 1)
    causal = row >= col
    strict = row > col
    lane = lax.broadcasted_iota(jnp.int32, (1, 256), 1)
    a_mats, rhs, atts = [], [], []
    for h in range(N_HEADS):
        hs = slice(h * HEAD_DIM, (h + 1) * HEAD_DIM)
        hmask = (lane >= h * HEAD_DIM) & (lane < (h + 1) * HEAD_DIM)
        bh = b128[:, h * 128:(h + 1) * 128]
        decay = jnp.where(causal, jnp.exp(jnp.where(causal, bh - bh.T, 0.0)), 0.0)
        k_h = jnp.where(hmask, k, 0.0)
        a_mats.append(-jnp.where(strict, _mm_nt(jnp.where(hmask, kb, 0.0), k_h, True) * decay, 0.0))
        atts.append(_mm_nt(jnp.where(hmask, q, 0.0), k_h, hp) * decay)
        rhs.append(jnp.concatenate([rv[:, hs], rk[:, hs]], axis=-1))
    zero = jnp.zeros((CHUNK, CHUNK), F32)
    xs = []
    for pair in range(N_HEADS // 2):
        h0, h1 = 2 * pair, 2 * pair + 1
        a = jnp.concatenate([jnp.concatenate([a_mats[h0], zero], axis=-1),
                             jnp.concatenate([zero, a_mats[h1]], axis=-1)], axis=0)
        x = jnp.concatenate([rhs[h0], rhs[h1]], axis=0)
        if t_valid > 1:
            x = x + _mm(a, x, True)
            for _ in range(6):
                a = _mm(a, a, True)
                x = x + _mm(a, x, True)
        xs += [x[:CHUNK], x[CHUNK:]]
    u_all = jnp.concatenate([x[:, :HEAD_DIM] for x in xs], axis=-1)
    w_all = jnp.concatenate([x[:, HEAD_DIM:] for x in xs], axis=-1)
    st = st_scr[...]
    v_new = u_all - _mm_nt(w_all, st, hp)
    o = _mm_nt(q_dec, st, hp)
    for h in range(N_HEADS):
        hmask = (lane >= h * HEAD_DIM) & (lane < (h + 1) * HEAD_DIM)
        o = o + _mm(atts[h], jnp.where(hmask, v_new, 0.0), hp)
    st = st * jnp.exp(b_last64) + jnp.where(blk > 0, _mm_tn(v_new, k_dec, hp), 0.0)
    st_scr[...] = st
    sfin_ref[0] = st
    o = o * lax.rsqrt(_group_mean(o * o, blk, HEAD_DIM) + EPS) * gn_ref[...] * _silu(gate)
    o_ref[...] = o[0:rows_in, :]


def _gdn(pm, ps, conv_prev, s0, conv_w, a_log, dt_bias, norm_g, bsz, t_valid, rows_in, hp):
    nt = pm.shape[0] // (bsz * rows_in)
    fixed = lambda b, i: (0, 0)
    rows = lambda b, i: (b * nt + i, 0)
    rep = lambda v, w: jnp.repeat(v, w).reshape(1, N_HEADS * w)
    o, sfin = pl.pallas_call(
        functools.partial(_gdn_kernel, rows_in=rows_in, t_valid=t_valid, hp=hp),
        grid=(bsz, nt),
        in_specs=[
            pl.BlockSpec((rows_in, 1024), lambda b, i: (b * nt + i, 1)),
            pl.BlockSpec((rows_in, LANES), rows),
            pl.BlockSpec((1, 3, B_CONV_CH), lambda b, i: (b, 0, 0)),
            pl.BlockSpec((1, 256, 256), lambda b, i: (b, 0, 0)),
            pl.BlockSpec((CONV_WIDTH, B_CONV_CH), fixed),
            pl.BlockSpec((1, 512), fixed),
            pl.BlockSpec((1, 512), fixed),
            pl.BlockSpec((1, 256), fixed),
            pl.BlockSpec((1, 256), fixed),
            pl.BlockSpec((1, 256), fixed),
            pl.BlockSpec((256, 256), fixed),
            pl.BlockSpec((CHUNK, CHUNK), fixed),
            pl.BlockSpec((LANES, 512), fixed),
            pl.BlockSpec((LANES, 512), fixed),
            pl.BlockSpec((LANES, 256), fixed),
            pl.BlockSpec((LANES, 256), fixed),
        ],
        out_specs=[
            pl.BlockSpec((rows_in, 256), rows),
            pl.BlockSpec((1, 256, 256), lambda b, i: (b, 0, 0)),
        ],
        out_shape=[
            jax.ShapeDtypeStruct((pm.shape[0], 256), F32),
            jax.ShapeDtypeStruct((bsz, 256, 256), F32),
        ],
        scratch_shapes=[pltpu.VMEM((ROW0 + CHUNK, B_CONV_CH), F32), pltpu.VMEM((256, 256), F32)],
        compiler_params=pltpu.CompilerParams(
            dimension_semantics=("parallel", "arbitrary"), vmem_limit_bytes=VMEM_LIMIT),
        name="gdn",
    )(pm, ps, conv_prev, _block_diag_state(s0, bsz), conv_w, rep(dt_bias, 128), rep(a_log, 128),
      rep(dt_bias, 64), rep(a_log, 64), norm_g.reshape(1, 256), jnp.asarray(_BLK256, BF16),
      jnp.asarray(_TRI, BF16), _head_expand(GDN_BETA_LANE, 128), _head_expand(GDN_DT_LANE, 128),
      _head_expand(GDN_BETA_LANE, 64), _head_expand(GDN_DT_LANE, 64))
    return o, _unblock_diag_state(sfin, bsz)


def _prep_weights(prm):
    w_in = prm['w_in']
    wm = w_in[:, :, _MAIN_COLS]
    ws = jnp.pad(w_in[:, :, _SMALL_COLS], ((0, 0), (0, 0), (0, LANES - len(_SMALL_COLS))))
    wr = jnp.concatenate([prm['moe_w_expert'], prm['moe_w_group']], axis=-1)
    n_r = N_EXPERTS + N_EXPERT_GROUPS
    wr = jnp.pad(wr, ((0, 0), (0, 0), (0, LANES - n_r)))
    both = lambda pair: [(pair[0][:D_MODEL], pair[1][:D_MODEL]), (pair[0][D_MODEL:], pair[1][D_MODEL:])]
    ws_split = both(_split_weight(ws.reshape(DEPTH * D_MODEL, LANES)))
    wr_split = both(_split_weight(wr.reshape(DEPTH * D_MODEL, LANES)))
    br = jnp.pad(jnp.concatenate([prm['moe_b_expert'], prm['moe_b_group']], axis=-1),
                 ((0, 0), (0, LANES - n_r)))[:, None, :]
    bf = lambda a: a.astype(BF16)
    prompt, sample = [], []
    for l in range(DEPTH):
        common = dict(ws=ws_split[l], wr=wr_split[l], br=br[l])
        if l == 0:
            wm_p, wo_p = tuple(_split_weight(wm[l])), tuple(_split_weight(prm['w_out'][l]))
        else:
            wm_p, wo_p = (bf(wm[l]),), (bf(prm['w_out'][l]),)
        prompt.append(dict(common, wm=wm_p, wo=wo_p, w1=bf(prm['moe_w1'][l]), w3=bf(prm['moe_w3'][l]),
                           w2=bf(prm['moe_w2'][l]), wg=bf(prm['ple_w_gate'][l]), wp=bf(prm['ple_w_proj'][l])))
        sample.append(dict(common, wm=(wm[l],), wo=(prm['w_out'][l],), w1=prm['moe_w1'][l], w3=prm['moe_w3'][l],
                           w2=prm['moe_w2'][l], wg=prm['ple_w_gate'][l], wp=prm['ple_w_proj'][l]))
    return prompt, sample


def _trunk(x, p, init_state, fox_cache, lb_all, prm, wts, tm, hp_layers):
    s_hgrn0, s_gdn0, c_gdn0, s_ssd0, c_ssd0 = init_state
    bsz, t, _ = x.shape
    n = bsz * t
    h = x.reshape(n, D_MODEL)
    outs = [[] for _ in range(8)]
    row = lambda a: a.reshape(1, -1)
    for l in range(DEPTH):
        hp_mix = hp_layers[l]
        w = wts[l]
        pm, ps = _inproj(h, row(prm['g_mix'][l]), w['wm'], w['ws'][0], w['ws'][1],
                         INPROJ_TM if n % INPROJ_TM == 0 else tm)
        if fox_cache is None:
            o_c, k_c, lf_c = _fox_prompt(pm, ps, prm['fox_b_f'][l], prm['fox_q_norm'][l], prm['fox_k_norm'][l],
                                         prm['fox_out_norm'][l], bsz, t, hp_mix)
            o_c = o_c.reshape(bsz, t, GROUP_WIDTH)
            k_c = k_c.reshape(bsz, t, N_HEADS, HEAD_DIM)
            lf_c = lf_c.reshape(bsz, t, N_HEADS)
            v_c = pm[:, 2560:2816].reshape(bsz, t, N_HEADS, HEAD_DIM)
        rows_in = CHUNK if t % CHUNK == 0 else 8
        if rows_in == CHUNK:
            pmr, psr = pm, ps
        else:
            padr = lambda a: jnp.pad(a.reshape(bsz, t, -1), ((0, 0), (0, rows_in - t), (0, 0))).reshape(
                bsz * rows_in, -1)
            pmr, psr = padr(pm), padr(ps)
        unpad = lambda a: a.reshape(bsz, -1, GROUP_WIDTH)[:, :t]
        o_a, s_a = _hgrn(pmr, lb_all[l], prm['hgrn_norm'][l], s_hgrn0[l], bsz, t, rows_in, hp_mix)
        o_d, s_d = _ssd(pmr, psr, c_ssd0[l], s_ssd0[l], prm['ssd_conv_w'][l], prm['ssd_conv_b'][l],
                        prm['ssd_a_log'][l], prm['ssd_dt_bias'][l], prm['ssd_d'][l], prm['ssd_norm'][l],
                        bsz, t, rows_in, hp_mix)
        o_b, s_b = _gdn(pmr, psr, c_gdn0[l], s_gdn0[l], prm['gdn_conv_w'][l], prm['gdn_a_log'][l],
                        prm['gdn_dt_bias'][l], prm['gdn_norm'][l], bsz, t, rows_in, hp_mix)
        o_a, o_b, o_d = unpad(o_a), unpad(o_b), unpad(o_d)
        pm = pm.reshape(bsz, t, N_MAIN)
        ps = ps.reshape(bsz, t, LANES)
        c_d = jnp.concatenate([c_ssd0[l], pm[:, max(0, t - 3):, 3328:4096]], axis=1)[:, -(CONV_WIDTH - 1):]
        c_b = jnp.concatenate([c_gdn0[l], pm[:, max(0, t - 3):, 1024:1792]], axis=1)[:, -(CONV_WIDTH - 1):]
        if fox_cache is not None:
            cache_k, cache_v, cache_logf, page_table = fox_cache
            o_c, k_c, lf_c = _fox_sample(pmr, psr, cache_k, cache_v, cache_logf, page_table, l,
                                         prm['fox_b_f'][l], prm['fox_q_norm'][l], prm['fox_k_norm'][l],
                                         prm['fox_out_norm'][l])
            o_c = o_c.reshape(bsz, t, GROUP_WIDTH)
            k_c = k_c.reshape(bsz, t, N_HEADS, HEAD_DIM)
            lf_c = lf_c.reshape(bsz, t, N_HEADS)
            v_c = pm[..., 2560:2816].reshape(bsz, t, N_HEADS, HEAD_DIM)
        mix = jnp.concatenate([o_a, o_b, o_c, o_d], axis=-1).reshape(n, D_MODEL)
        h2, u2, cw = _outproj(h, mix, w['wo'], row(prm['g_ffn'][l]), w['wr'][0], w['wr'][1], w['br'], tm)
        y = _moe(u2, cw, w['w1'], w['w3'], w['w2'], MOE_TM if n % MOE_TM == 0 else tm)
        h = _ple(h2, y, p[l].reshape(n, PLE_DIM), row(prm['g_ple'][l]), w['wg'], w['wp'],
                 row(prm['g_final']), tm, final=(l == DEPTH - 1))
        for acc, val in zip(outs, (k_c, v_c, lf_c, s_a, s_b, c_b, s_d, c_d)):
            acc.append(val)
    return (h.reshape(bsz, t, D_MODEL),) + tuple(jnp.stack(acc) for acc in outs)


def _hgrn_lower_bounds(lb_param):
    sm = jax.nn.softmax(lb_param, axis=0)
    return jnp.concatenate([jnp.zeros_like(sm[:1]), jnp.cumsum(sm[1:], axis=0)], axis=0)


def kernel(x_prompt, x_sample, cache_fox_k, cache_fox_v, cache_fox_logf, state_hgrn, state_gdn,
           state_gdn_conv, state_ssd, state_ssd_conv, page_table, p_prompt, p_sample, w_in, w_out,
           g_mix, g_ffn, g_ple, g_final, hgrn_lb, hgrn_norm, gdn_conv_w, gdn_a_log, gdn_dt_bias,
           gdn_norm, fox_b_f, fox_q_norm, fox_k_norm, fox_out_norm, ssd_conv_w, ssd_conv_b, ssd_a_log,
           ssd_dt_bias, ssd_d, ssd_norm, moe_w_group, moe_b_group, moe_w_expert, moe_b_expert, moe_w1,
           moe_w3, moe_w2, ple_w_gate, ple_w_proj):
    prm = dict(w_in=w_in, w_out=w_out, g_mix=g_mix, g_ffn=g_ffn, g_ple=g_ple, g_final=g_final,
               hgrn_norm=hgrn_norm, gdn_conv_w=gdn_conv_w, gdn_a_log=gdn_a_log, gdn_dt_bias=gdn_dt_bias,
               gdn_norm=gdn_norm, fox_b_f=fox_b_f, fox_q_norm=fox_q_norm, fox_k_norm=fox_k_norm,
               fox_out_norm=fox_out_norm, ssd_conv_w=ssd_conv_w, ssd_conv_b=ssd_conv_b, ssd_a_log=ssd_a_log,
               ssd_dt_bias=ssd_dt_bias, ssd_d=ssd_d, ssd_norm=ssd_norm, moe_w_group=moe_w_group,
               moe_b_group=moe_b_group, moe_w_expert=moe_w_expert, moe_b_expert=moe_b_expert,
               moe_w1=moe_w1, moe_w3=moe_w3, moe_w2=moe_w2, ple_w_gate=ple_w_gate, ple_w_proj=ple_w_proj)
    wts_prompt, wts_sample = _prep_weights(prm)
    lb_all = _hgrn_lower_bounds(hgrn_lb)
    bp = x_prompt.shape[0]
    zero_state = (jnp.zeros((DEPTH, bp, N_HEADS, HEAD_DIM, HEAD_DIM), F32),
                  jnp.zeros((DEPTH, bp, N_HEADS, HEAD_DIM, HEAD_DIM), F32),
                  jnp.zeros((DEPTH, bp, CONV_WIDTH - 1, B_CONV_CH), F32),
                  jnp.zeros((DEPTH, bp, N_HEADS, D_STATE, HEAD_DIM), F32),
                  jnp.zeros((DEPTH, bp, CONV_WIDTH - 1, D_CONV_CH), F32))
    pr = _trunk(x_prompt, p_prompt, zero_state, None, lb_all, prm, wts_prompt, tm=512, hp_layers=(True, False))
    sm = _trunk(x_sample, p_sample, (state_hgrn, state_gdn, state_gdn_conv, state_ssd, state_ssd_conv),
                (cache_fox_k, cache_fox_v, cache_fox_logf, page_table), lb_all, prm, wts_sample, tm=32,
                hp_layers=(True, True))
    return (pr[0], sm[0]) + tuple(pr[1:]) + tuple(sm[1:])
```
